```python
import math
import functools
import jax
import jax.numpy as jnp
from jax import lax
import numpy as np

D_MODEL = 1024
BATCH = 4
SEQ = 4096
DEPTH = 2
DEC_BATCH = 128
DEC_SEQ = 8
PAST_LEN = 2048
PAGE_SIZE = 128

F32 = jnp.float32
D_MIX = D_MODEL
GROUP_W = D_MIX // 4
HEAD_DIM = 64
RW_HEADS = GROUP_W // HEAD_DIM
RW_DECAY_LORA = 64
RW_A_LORA = 64
RW_GATE_LORA = 128
RW_COLS = 3 * GROUP_W + RW_DECAY_LORA + RW_A_LORA + RW_GATE_LORA
RW_SPLITS = [GROUP_W, GROUP_W + RW_DECAY_LORA, 2 * GROUP_W + RW_DECAY_LORA, 3 * GROUP_W + RW_DECAY_LORA, 3 * GROUP_W + RW_DECAY_LORA + RW_A_LORA]
RW_GN_EPS = 64e-5
ML_HEADS = GROUP_W // HEAD_DIM
ML_CONV = 4
ML_CHUNK = 64
ML_COLS = 4 * GROUP_W + 2 * ML_HEADS
ML_SPLITS = [2 * GROUP_W, 3 * GROUP_W, 4 * GROUP_W, 4 * GROUP_W + ML_HEADS]
ML_NORM_EPS = 1e-6
POOL_WINDOWS = (2, 4, 8, 16)
POOL_GROUPS = len(POOL_WINDOWS)
POOL_GC = GROUP_W // POOL_GROUPS
POOL_HIST = max(POOL_WINDOWS) - 1
AT_HEADS = GROUP_W // HEAD_DIM
AT_COLS = 3 * GROUP_W
MOBA_BLOCK = 256
MOBA_TOPK = 3
MOBA_QCHUNK = 64
N_IN = RW_COLS + ML_COLS + GROUP_W + AT_COLS
D_FF = 4 * D_MODEL
NORM_EPS = 1e-6

kernel_name = 'hymba_rwkv7_mlstm_pool_moba_decode_step'


def rmsnorm(x, g):
    xf = x.astype(F32)
    y = xf * lax.rsqrt(jnp.mean(xf * xf, -1, keepdims=True) + NORM_EPS)
    return (y * g.astype(F32)).astype(x.dtype)


def head_norm(x, gain, bias, eps):
    mu = jnp.mean(x, -1, keepdims=True)
    xc = x - mu
    y = xc * lax.rsqrt(jnp.mean(xc * xc, -1, keepdims=True) + eps) * gain
    return y if bias is None else y + bias


def alibi_slopes(n_heads):
    return 2.0 ** (-8.0 * (jnp.arange(n_heads, dtype=F32) + 1.0) / n_heads)


def rwkv7_recurrence(r, decay, k, v, kk, a, s0):
    def step(S, inp):
        r_t, w_t, k_t, v_t, kk_t, a_t = inp
        sa = jnp.einsum('bhvk,bhk->bhv', S, -kk_t)
        S = S * w_t[:, :, None, :] + sa[..., None] * (kk_t * a_t)[:, :, None, :] + v_t[..., None] * k_t[:, :, None, :]
        return S, jnp.einsum('bhvk,bhk->bhv', S, r_t)
    xs = tuple(jnp.moveaxis(t, 1, 0) for t in (r, decay, k, v, kk, a))
    S, ys = lax.scan(step, s0.astype(F32), xs)
    return jnp.moveaxis(ys, 0, 1), S


def rwkv7_mixer(u, prev_row, wkv0, lw):
    B, T, _ = u.shape
    uf = u.astype(F32)
    prev = jnp.concatenate([prev_row.astype(F32)[:, None], uf[:, :-1]], 1)
    xs = uf + (prev - uf) * lw['rwkv_mu']
    r, wc, k, v, ac, gc = jnp.split(xs, RW_SPLITS, axis=-1)
    w_log = -jax.nn.softplus(-(lw['rwkv_w0'] + jnp.tanh(wc) @ lw['rwkv_w_up'])) - 0.5
    a = jax.nn.sigmoid(lw['rwkv_a0'] + ac @ lw['rwkv_a_up'])
    g = jax.nn.sigmoid(gc) @ lw['rwkv_g_up']
    heads = lambda t: t.astype(F32).reshape(B, T, RW_HEADS, HEAD_DIM)
    r, k, v, a = heads(r), heads(k), heads(v), heads(a)
    decay = heads(jnp.exp(-jnp.exp(w_log)))
    kk = k * lw['rwkv_k_k'].reshape(RW_HEADS, HEAD_DIM)
    kk = kk / jnp.maximum(jnp.linalg.norm(kk, axis=-1, keepdims=True), 1e-12)
    k = k * (1.0 + (a - 1.0) * lw['rwkv_k_a'].reshape(RW_HEADS, HEAD_DIM))
    y, wkv = rwkv7_recurrence(r, decay, k, v, kk, a, wkv0)
    y = head_norm(y, lw['rwkv_ln_g'].reshape(RW_HEADS, HEAD_DIM), lw['rwkv_ln_b'].reshape(RW_HEADS, HEAD_DIM), RW_GN_EPS)
    y = y + jnp.sum(r * k * lw['rwkv_r_k'], -1, keepdims=True) * v
    y = y.reshape(B, T, GROUP_W) * g
    return y, u[:, -1], wkv


def mlstm_chunkwise(q, k, v, ig, fg, C0, n0, m0):
    B, T, H, d = q.shape
    L = math.gcd(T, ML_CHUNK)
    nc = T // L

    def chunks(t):
        t = jnp.swapaxes(t, 1, 2)
        return jnp.moveaxis(t.reshape((B, H, nc, L) + t.shape[3:]), 2, 0)

    lf = jax.nn.log_sigmoid(fg)
    causal = jnp.tril(jnp.ones((L, L), bool))

    def step(carry, xs):
        C, n, m = carry
        qc, kc, vc, ic, lfc = xs
        b = jnp.cumsum(lfc, -1)
        logD = jnp.where(causal, b[..., :, None] - b[..., None, :] + ic[..., None, :], -jnp.inf)
        inter = b + m[..., None]
        m_row = jnp.maximum(jnp.max(logD, -1), inter)
        s = jnp.einsum('bhtd,bhsd->bhts', qc, kc) * jnp.exp(logD - m_row[..., None])
        w_int = jnp.exp(inter - m_row)
        num = jnp.einsum('bhts,bhsd->bhtd', s, vc) + w_int[..., None] * jnp.einsum('bhvk,bhtk->bhtv', C, qc)
        den = jnp.sum(s, -1) + w_int * jnp.einsum('bhk,bhtk->bht', n, qc)
        h = num / jnp.maximum(jnp.abs(den), jnp.exp(-m_row))[..., None]
        g = b[..., -1]
        logw = g[..., None] - b + ic
        m_new = jnp.maximum(g + m, jnp.max(logw, -1))
        wts = jnp.exp(logw - m_new[..., None])
        dec = jnp.exp(g + m - m_new)
        C = dec[..., None, None] * C + jnp.einsum('bhs,bhsv,bhsk->bhvk', wts, vc, kc)
        n = dec[..., None] * n + jnp.einsum('bhs,bhsk->bhk', wts, kc)
        return (C, n, m_new), h

    (C, n, m), h = lax.scan(step, (C0, n0, m0), tuple(chunks(t) for t in (q, k, v, ig, lf)))
    h = jnp.moveaxis(h, 0, 2).reshape(B, H, T, d)
    return jnp.swapaxes(h, 1, 2), C, n, m


def mlstm_mixer(u, conv_buf, C0, n0, m0, lw):
    B, T, _ = u.shape
    qk, v, o, ig, fg = jnp.split(u, ML_SPLITS, axis=-1)
    z = jnp.concatenate([conv_buf.astype(u.dtype), qk], 1)
    conv = lw['mlstm_conv_b'].astype(F32)
    for j in range(ML_CONV):
        conv = conv + z[:, j:j + T].astype(F32) * lw['mlstm_conv_w'][j]
    q, k = jnp.split(jax.nn.silu(conv), 2, axis=-1)
    heads = lambda t: t.astype(F32).reshape(B, T, ML_HEADS, HEAD_DIM)
    h, C, n, m = mlstm_chunkwise(heads(q), heads(k) * HEAD_DIM ** -0.5, heads(v),
                                 ig.astype(F32) + lw['mlstm_i_b'], fg.astype(F32) + lw['mlstm_f_b'],
                                 C0.astype(F32), n0.astype(F32), m0.astype(F32))
    h = head_norm(h, lw['mlstm_norm_g'].reshape(ML_HEADS, HEAD_DIM), None, ML_NORM_EPS)
    y = h.reshape(B, T, GROUP_W) * jax.nn.sigmoid(o.astype(F32))
    return y, z[:, -(ML_CONV - 1):], C, n, m


def pool_mixer(u, hist, pos0, pool_w, pool_scale):
    B, T, _ = u.shape
    uf = u.astype(F32)
    z = jnp.concatenate([hist.astype(F32), uf], 1)
    cs = jnp.pad(jnp.cumsum(z, 1), ((0, 0), (1, 0), (0, 0)))
    pos = pos0 + jnp.arange(T)
    end = POOL_HIST + 1 + jnp.arange(T)
    parts = []
    for gi, w in enumerate(POOL_WINDOWS):
        lo, hi = gi * POOL_GC, (gi + 1) * POOL_GC
        win = cs[:, end, lo:hi] - cs[:, end - w, lo:hi]
        cnt = jnp.minimum(pos + 1, w).astype(F32)[None, :, None]
        parts.append(win / cnt - uf[..., lo:hi])
    pooled = jnp.stack(parts, 2)
    y = jnp.einsum('btgc,gcd->btgd', pooled, pool_w).reshape(B, T, GROUP_W) * pool_scale
    return y, z[:, -POOL_HIST:].astype(u.dtype)


def moba_core(q, q_pos, own_k, own_v, own_pos, sel_k=None, sel_v=None, sel_pos=None, sel_valid=None):
    B, H, Q, d = q.shape
    scale = d ** -0.5
    slopes = alibi_slopes(H)
    dist_own = (q_pos[:, None] - own_pos[None, :]).astype(F32)
    s_own = jnp.einsum('bhqd,bhkd->bhqk', q, own_k) * scale - slopes[:, None, None] * dist_own
    s_own = jnp.where(own_pos[None, :] <= q_pos[:, None], s_own, -jnp.inf)
    if sel_k is None:
        return jnp.einsum('bhqk,bhkd->bhqd', jax.nn.softmax(s_own, -1), own_v)
    S, L = sel_k.shape[3], sel_k.shape[4]
    dist_sel = (q_pos[None, None, :, None, None] - sel_pos).astype(F32)
    s_sel = jnp.einsum('bhqd,bhqsld->bhqsl', q, sel_k) * scale - slopes[None, :, None, None, None] * dist_sel
    s_sel = jnp.where(sel_valid[None, None, :, :, None], s_sel, -jnp.inf)
    p = jax.nn.softmax(jnp.concatenate([s_sel.reshape(B, H, Q, S * L), s_own], -1), -1)
    return (jnp.einsum('bhqk,bhqkd->bhqd', p[..., :S * L], sel_v.reshape(B, H, Q, S * L, d))
            + jnp.einsum('bhqk,bhkd->bhqd', p[..., S * L:], own_v))


def moba_prompt(q, k, v):
    B, T, H, d = q.shape
    nblk = -(-T // MOBA_BLOCK)
    pad = nblk * MOBA_BLOCK - T
    qh = jnp.swapaxes(q, 1, 2).astype(F32)
    blocks = lambda t: jnp.pad(jnp.swapaxes(t, 1, 2).astype(F32), ((0, 0), (0, 0), (0, pad), (0, 0))).reshape(B, H, nblk, MOBA_BLOCK, d)
    kb, vb = blocks(k), blocks(v)
    qc = math.gcd(T, MOBA_QCHUNK)
    n_sel = min(MOBA_TOPK, nblk - 1)
    pos = jnp.arange(T)
    blk_off = jnp.arange(MOBA_BLOCK)
    bi = jnp.arange(B)[:, None, None, None]
    hi = jnp.arange(H)[None, :, None, None]
    if n_sel > 0:
        n_past = pos // MOBA_BLOCK
        gate = jnp.einsum('bhtd,bhnd->bhtn', qh, jnp.mean(kb, 3))
        gate = jnp.where(jnp.arange(nblk)[None, :] < n_past[:, None], gate, -jnp.inf)
        _, sel = lax.top_k(gate, n_sel)
        sel_valid = jnp.arange(n_sel)[None, :] < n_past[:, None]

    def chunk(t0):
        q_c = lax.dynamic_slice_in_dim(qh, t0, qc, axis=2)
        q_pos = t0 + jnp.arange(qc)
        blk = t0 // MOBA_BLOCK
        own_k = lax.dynamic_index_in_dim(kb, blk, axis=2, keepdims=False)
        own_v = lax.dynamic_index_in_dim(vb, blk, axis=2, keepdims=False)
        own_pos = blk * MOBA_BLOCK + blk_off
        if n_sel == 0:
            return moba_core(q_c, q_pos, own_k, own_v, own_pos)
        idx = lax.dynamic_slice_in_dim(sel, t0, qc, axis=2)
        return moba_core(q_c, q_pos, own_k, own_v, own_pos,
                         kb[bi, hi, idx], vb[bi, hi, idx],
                         idx[..., None] * MOBA_BLOCK + blk_off,
                         lax.dynamic_slice_in_dim(sel_valid, t0, qc, axis=0))

    out = lax.map(chunk, jnp.arange(0, T, qc))
    return jnp.transpose(out, (1, 0, 3, 2, 4)).reshape(B, T, H * d)


def moba_sample(q, k, v, pool_k, pool_v, page_table):
    DB, T, H, d = q.shape
    ppb = MOBA_BLOCK // PAGE_SIZE
    n_full = PAST_LEN // MOBA_BLOCK
    own_pages = PAST_LEN // PAGE_SIZE - n_full * ppb
    qh = jnp.swapaxes(q, 1, 2).astype(F32)
    kn = jnp.swapaxes(k, 1, 2).astype(F32)
    vn = jnp.swapaxes(v, 1, 2).astype(F32)

    def paged_rows(pool, pages):
        return jnp.swapaxes(pool[pages].reshape(DB, -1, H, d), 1, 2).astype(F32)

    own_k, own_v = kn, vn
    if own_pages > 0:
        op = page_table[:, n_full * ppb:]
        own_k = jnp.concatenate([paged_rows(pool_k, op), kn], 2)
        own_v = jnp.concatenate([paged_rows(pool_v, op), vn], 2)
    own_pos = n_full * MOBA_BLOCK + jnp.arange(own_k.shape[2])
    n_sel = min(MOBA_TOPK, n_full)
    if n_sel == 0:
        out = moba_core(qh, PAST_LEN + jnp.arange(T), own_k, own_v, own_pos)
        return jnp.swapaxes(out, 1, 2).reshape(DB, T, H * d)
    k_past = paged_rows(pool_k, page_table[:, :n_full * ppb])
    means = jnp.mean(k_past.reshape(DB, H, n_full, MOBA_BLOCK, d), 3)
    _, sel = lax.top_k(jnp.einsum('bhtd,bhnd->bhtn', qh, means), n_sel)
    bi = jnp.arange(DB)[:, None, None, None]
    hi = jnp.arange(H)[None, :, None, None]
    page_off = jnp.arange(ppb)
    blk_off = jnp.arange(MOBA_BLOCK)
    valid = jnp.ones((1, n_sel), bool)

    def one_query(t):
        idx = lax.dynamic_index_in_dim(sel, t, axis=2, keepdims=False)
        pages = page_table[bi, idx[..., None] * ppb + page_off]
        sk = pool_k[pages, :, hi].reshape(DB, H, 1, n_sel, MOBA_BLOCK, d).astype(F32)
        sv = pool_v[pages, :, hi].reshape(DB, H, 1, n_sel, MOBA_BLOCK, d).astype(F32)
        spos = (idx[..., None] * MOBA_BLOCK + blk_off)[:, :, None]
        q_t = lax.dynamic_slice_in_dim(qh, t, 1, axis=2)
        q_pos = PAST_LEN + t + jnp.arange(1)
        return moba_core(q_t, q_pos, own_k, own_v, own_pos, sk, sv, spos, valid)[:, :, 0]

    out = lax.map(one_query, jnp.arange(T))
    return jnp.transpose(out, (1, 0, 2, 3)).reshape(DB, T, H * d)


def token_mixers(h, lw, st, pos0, attend):
    shift0, wkv0, conv0, C0, n0, m0, pool0 = st
    B, T, _ = h.shape
    proj = h @ lw['w_in']
    u_rw, u_ml, u_pool, u_at = jnp.split(proj, [RW_COLS, RW_COLS + ML_COLS, RW_COLS + ML_COLS + GROUP_W], axis=-1)
    y_rw, shift1, wkv1 = rwkv7_mixer(u_rw, shift0, wkv0, lw)
    y_ml, conv1, C1, n1, m1 = mlstm_mixer(u_ml, conv0, C0, n0, m0, lw)
    y_pool, pool1 = pool_mixer(u_pool, pool0, pos0, lw['pool_w'], lw['pool_scale'])
    q, k, v = [t.reshape(B, T, AT_HEADS, HEAD_DIM) for t in jnp.split(u_at, 3, axis=-1)]
    y_at = attend(q, k, v)
    y = jnp.concatenate([y_rw, y_ml, y_pool, y_at], -1).astype(h.dtype) @ lw['w_out']
    return y, (shift1, wkv1, conv1, C1, n1, m1, pool1), k, v


def trunk_layer(x, c, lw, st, pos0, attend):
    mod = jax.nn.silu(c) @ lw['ada_w'] + lw['ada_b']
    sh1, sc1, g1, sh2, sc2, g2 = [m[:, None] for m in jnp.split(mod, 6, axis=-1)]
    h = rmsnorm(x, lw['norm1_g']) * (1.0 + sc1) + sh1
    y, st1, k, v = token_mixers(h, lw, st, pos0, attend)
    x = x + g1 * y
    h = rmsnorm(x, lw['norm2_g']) * (1.0 + sc2) + sh2
    x = x + g2 * (jnp.square(jax.nn.relu(h @ lw['mlp_up'])) @ lw['mlp_down'])
    return x, st1, k, v


def setup_inputs(seed: int = 0) -> dict:
    key = jax.random.key(seed)
    keys = iter(jax.random.split(key, 64))

    def nrm(shape, scale):
        return scale * jax.random.normal(next(keys), shape, F32)

    L = DEPTH
    n_pages = PAST_LEN // PAGE_SIZE
    n_phys = (5 * DEC_BATCH * n_pages) // 4
    page_table = jax.random.permutation(next(keys), n_phys)[:DEC_BATCH * n_pages].reshape(DEC_BATCH, n_pages).astype(jnp.int32)
    inputs = {
        'x_prompt': nrm((BATCH, SEQ, D_MODEL), 1.0),
        'x_sample': nrm((DEC_BATCH, DEC_SEQ, D_MODEL), 1.0),
        'c_prompt': nrm((BATCH, D_MODEL), 1.0),
        'c_sample': nrm((DEC_BATCH, D_MODEL), 1.0),
        'state_rwkv_shift': nrm((L, DEC_BATCH, RW_COLS), 1.0),
        'state_rwkv_wkv': nrm((L, DEC_BATCH, RW_HEADS, HEAD_DIM, HEAD_DIM), 0.5),
        'state_mlstm_conv': nrm((L, DEC_BATCH, ML_CONV - 1, 2 * GROUP_W), 1.0),
        'state_mlstm_c': nrm((L, DEC_BATCH, ML_HEADS, HEAD_DIM, HEAD_DIM), 1.0),
        'state_mlstm_n': nrm((L, DEC_BATCH, ML_HEADS, HEAD_DIM), 1.0),
        'state_mlstm_m': nrm((L, DEC_BATCH, ML_HEADS), 1.0),
        'state_pool': nrm((L, DEC_BATCH, POOL_HIST, GROUP_W), 1.0),
        'cache_k': nrm((L, n_phys, PAGE_SIZE, AT_HEADS, HEAD_DIM), 1.0),
        'cache_v': nrm((L, n_phys, PAGE_SIZE, AT_HEADS, HEAD_DIM), 1.0),
        'page_table': page_table,
        'ada_w': nrm((L, D_MODEL, 6 * D_MODEL), 0.5 * D_MODEL ** -0.5),
        'ada_b': nrm((L, 6 * D_MODEL), 0.02),
        'norm1_g': 1.0 + nrm((L, D_MODEL), 0.02),
        'norm2_g': 1.0 + nrm((L, D_MODEL), 0.02),
        'w_in': nrm((L, D_MODEL, N_IN), D_MODEL ** -0.5),
        'w_out': nrm((L, D_MIX, D_MODEL), D_MIX ** -0.5),
        'rwkv_mu': jax.random.uniform(next(keys), (L, RW_COLS), F32),
        'rwkv_w0': jnp.linspace(-6.0, -1.0, GROUP_W, dtype=F32)[None] + nrm((L, GROUP_W), 0.1),
        'rwkv_w_up': nrm((L, RW_DECAY_LORA, GROUP_W), 0.1),
        'rwkv_a0': nrm((L, GROUP_W), 0.1),
        'rwkv_a_up': nrm((L, RW_A_LORA, GROUP_W), 0.5 * RW_A_LORA ** -0.5),
        'rwkv_g_up': nrm((L, RW_GATE_LORA, GROUP_W), RW_GATE_LORA ** -0.5),
        'rwkv_k_k': 1.0 + nrm((L, GROUP_W), 0.1),
        'rwkv_k_a': 1.0 + nrm((L, GROUP_W), 0.1),
        'rwkv_r_k': nrm((L, RW_HEADS, HEAD_DIM), 0.1),
        'rwkv_ln_g': 1.0 + nrm((L, GROUP_W), 0.02),
        'rwkv_ln_b': nrm((L, GROUP_W), 0.02),
        'mlstm_conv_w': nrm((L, ML_CONV, 2 * GROUP_W), 0.5),
        'mlstm_conv_b': nrm((L, 2 * GROUP_W), 0.02),
        'mlstm_i_b': nrm((L, ML_HEADS), 0.1),
        'mlstm_f_b': jnp.linspace(3.0, 6.0, ML_HEADS, dtype=F32)[None] + nrm((L, ML_HEADS), 0.1),
        'mlstm_norm_g': 1.0 + nrm((L, GROUP_W), 0.02),
        'pool_w': nrm((L, POOL_GROUPS, POOL_GC, POOL_GC), POOL_GC ** -0.5),
        'pool_scale': 1.0 + nrm((L, GROUP_W), 0.02),
        'mlp_up': nrm((L, D_MODEL, D_FF), D_MODEL ** -0.5),
        'mlp_down': nrm((L, D_FF, D_MODEL), D_FF ** -0.5),
        'final_norm_g': 1.0 + nrm((D_MODEL,), 0.02),
    }
    return inputs


def reference(x_prompt, x_sample, c_prompt, c_sample,
              state_rwkv_shift, state_rwkv_wkv, state_mlstm_conv, state_mlstm_c,
              state_mlstm_n, state_mlstm_m, state_pool, cache_k, cache_v, page_table,
              ada_w, ada_b, norm1_g, norm2_g, w_in, w_out,
              rwkv_mu, rwkv_w0, rwkv_w_up, rwkv_a0, rwkv_a_up, rwkv_g_up,
              rwkv_k_k, rwkv_k_a, rwkv_r_k, rwkv_ln_g, rwkv_ln_b,
              mlstm_conv_w, mlstm_conv_b, mlstm_i_b, mlstm_f_b, mlstm_norm_g,
              pool_w, pool_scale, mlp_up, mlp_down, final_norm_g):
    B = x_prompt.shape[0]
    st_prompt0 = (jnp.zeros((B, RW_COLS), x_prompt.dtype),
                  jnp.zeros((B, RW_HEADS, HEAD_DIM, HEAD_DIM), F32),
                  jnp.zeros((B, ML_CONV - 1, 2 * GROUP_W), x_prompt.dtype),
                  jnp.zeros((B, ML_HEADS, HEAD_DIM, HEAD_DIM), F32),
                  jnp.zeros((B, ML_HEADS, HEAD_DIM), F32),
                  jnp.zeros((B, ML_HEADS), F32),
                  jnp.zeros((B, POOL_HIST, GROUP_W), x_prompt.dtype))
    xp, xs = x_prompt, x_sample
    new_p = [[] for _ in range(9)]
    new_s = [[] for _ in range(9)]
    for l in range(DEPTH):
        lw = {'ada_w': ada_w[l], 'ada_b': ada_b[l], 'norm1_g': norm1_g[l], 'norm2_g': norm2_g[l],
              'w_in': w_in[l], 'w_out': w_out[l],
              'rwkv_mu': rwkv_mu[l], 'rwkv_w0': rwkv_w0[l], 'rwkv_w_up': rwkv_w_up[l],
              'rwkv_a0': rwkv_a0[l], 'rwkv_a_up': rwkv_a_up[l], 'rwkv_g_up': rwkv_g_up[l],
              'rwkv_k_k': rwkv_k_k[l], 'rwkv_k_a': rwkv_k_a[l], 'rwkv_r_k': rwkv_r_k[l],
              'rwkv_ln_g': rwkv_ln_g[l], 'rwkv_ln_b': rwkv_ln_b[l],
              'mlstm_conv_w': mlstm_conv_w[l], 'mlstm_conv_b': mlstm_conv_b[l],
              'mlstm_i_b': mlstm_i_b[l], 'mlstm_f_b': mlstm_f_b[l], 'mlstm_norm_g': mlstm_norm_g[l],
              'pool_w': pool_w[l], 'pool_scale': pool_scale[l],
              'mlp_up': mlp_up[l], 'mlp_down': mlp_down[l]}
        xp, st_p, kp, vp = trunk_layer(xp, c_prompt, lw, st_prompt0, 0, moba_prompt)
        st_s_in = (state_rwkv_shift[l], state_rwkv_wkv[l], state_mlstm_conv[l], state_mlstm_c[l],
                   state_mlstm_n[l], state_mlstm_m[l], state_pool[l])
        attend_s = functools.partial(moba_sample, pool_k=cache_k[l], pool_v=cache_v[l], page_table=page_table)
        xs, st_s, ks, vs = trunk_layer(xs, c_sample, lw, st_s_in, PAST_LEN, attend_s)
        for lst, val in zip(new_p, st_p + (kp, vp)):
            lst.append(val)
        for lst, val in zip(new_s, st_s + (ks, vs)):
            lst.append(val)
    p_shift, p_wkv, p_conv, p_c, p_n, p_m, p_pool, p_k, p_v = [jnp.stack(a) for a in new_p]
    s_shift, s_wkv, s_conv, s_c, s_n, s_m, s_pool, s_k, s_v = [jnp.stack(a) for a in new_s]
    y_prompt = rmsnorm(xp, final_norm_g)
    y_sample = rmsnorm(xs, final_norm_g)
    return (y_prompt, y_sample,
            p_shift, p_wkv, p_conv, p_c, p_n, p_m, p_pool, p_k, p_v,
            s_shift, s_wkv, s_conv, s_c, s_n, s_m, s_pool, s_k, s_v)
```

```python
import functools
import math

import jax
import jax.numpy as jnp
import numpy as np
from jax import lax
from jax.experimental import pallas as pl
from jax.experimental.pallas import tpu as pltpu

F32 = jnp.float32
BF16 = jnp.bfloat16

D_MODEL = 1024
GROUP_W = 256
HEAD_DIM = 64
N_HEADS = 4
RW_COLS = 1024
ML_MAIN = 1024
ML_GATES = 8
GATE_PAD = 128
N_IN = 3080
MOBA_BLOCK = 256
MOBA_TOPK = 3
ML_CHUNK = 64
RW_CHUNK = 64
POOL_WINDOWS = (2, 4, 8, 16)
POOL_HIST = 15
NORM_EPS = 1e-6
RW_GN_EPS = 64e-5
ML_NORM_EPS = 1e-6
ROW_TILE = 512
FF_TILE = 1024
VMEM_LIMIT = 48 * 1024 * 1024

NN = (((1,), (0,)), ((), ()))
NT = (((1,), (1,)), ((), ()))
TN = (((0,), (0,)), ((), ()))


def _dot(a, b, dn=NN):
    return lax.dot_general(a, b, dn, preferred_element_type=F32)


def _bdot(a, b, dn=NN):
    return _dot(a.astype(BF16), b.astype(BF16), dn)


def _split2(x):
    hi = x.astype(BF16)
    lo = (x - hi.astype(F32)).astype(BF16)
    return hi, lo


def _split3(x):
    hi = x.astype(BF16)
    r = x - hi.astype(F32)
    mid = r.astype(BF16)
    lo = (r - mid.astype(F32)).astype(BF16)
    return hi, mid, lo


def _dot3(a, b, dn=NN):
    ah, al = _split2(a)
    bh, bl = _split2(b)
    return _dot(ah, bh, dn) + (_dot(ah, bl, dn) + _dot(al, bh, dn))


def _dot_exact_rhs(a, b01, dn=NN):
    h, m, l = _split3(a)
    b = b01.astype(BF16)
    return _dot(h, b, dn) + (_dot(m, b, dn) + _dot(l, b, dn))


def _dot_exact_lhs(a01, b, dn=NN):
    h, m, l = _split3(b)
    a = a01.astype(BF16)
    return _dot(a, h, dn) + (_dot(a, m, dn) + _dot(a, l, dn))


def _iota(shape, dim):
    return lax.broadcasted_iota(jnp.int32, shape, dim)


def _sigmoid(x):
    return jax.nn.sigmoid(x)


def _softplus(x):
    return jnp.maximum(x, 0.0) + jnp.log(1.0 + jnp.exp(-jnp.abs(x)))


def _block_ones():
    return (_iota((GROUP_W, GROUP_W), 0) // HEAD_DIM == _iota((GROUP_W, GROUP_W), 1) // HEAD_DIM).astype(F32)


def _head_sum(x, bo):
    return _dot_exact_rhs(x, bo)


def _params(sem):
    return pltpu.CompilerParams(dimension_semantics=sem, vmem_limit_bytes=VMEM_LIMIT)


def _row_blocks(B, T):
    if T >= ROW_TILE:
        assert T % ROW_TILE == 0
        return 1, ROW_TILE
    bb = max(1, min(B, ROW_TILE // T))
    while B % bb:
        bb -= 1
    return bb, T


def _ada_kernel(c_ref, w_ref, b_ref, o_ref):
    c = c_ref[...]
    o_ref[...] = _bdot(c * _sigmoid(c), w_ref[...]) + b_ref[...]


def _ada_call(c_all, ada_w, ada_b):
    Ld, D, N6 = ada_w.shape
    NB = c_all.shape[0]
    tn = 1024
    return pl.pallas_call(
        _ada_kernel,
        grid=(Ld, N6 // tn),
        in_specs=[pl.BlockSpec((NB, D), lambda l, j: (0, 0)),
                  pl.BlockSpec((None, D, tn), lambda l, j: (l, 0, j)),
                  pl.BlockSpec((None, 1, tn), lambda l, j: (l, 0, j))],
        out_specs=pl.BlockSpec((None, NB, tn), lambda l, j: (l, 0, j)),
        out_shape=jax.ShapeDtypeStruct((Ld, NB, N6), F32),
        compiler_params=_params(("arbitrary", "arbitrary")),
        name="ada_mod",
    )(c_all, ada_w, ada_b.reshape(Ld, 1, N6))


IN_WIDTHS = (RW_COLS, ML_MAIN, GATE_PAD, GROUP_W, GROUP_W, GROUP_W, GROUP_W)


def _in_kernel(x_ref, sc_ref, sh_ref, g_ref, w_ref, *out_refs):
    x = x_ref[...]
    bb, tt, D = x.shape
    y = x * lax.rsqrt(jnp.mean(x * x, -1, keepdims=True) + NORM_EPS) * g_ref[...]
    h = y * (1.0 + sc_ref[...]) + sh_ref[...]
    hb = h.reshape(bb * tt, D).astype(BF16)
    off = 0
    for ref in out_refs:
        n = ref.shape[-1]
        ref[...] = _dot(hb, w_ref[:, off:off + n]).reshape(bb, tt, n)
        off += n


def _in_call(x, sc, sh, g, wpack):
    B, T, D = x.shape
    bb, tt = _row_blocks(B, T)
    row = lambda n: pl.BlockSpec((bb, tt, n), lambda b, j: (b, j, 0))
    mod = pl.BlockSpec((bb, 1, D), lambda b, j: (b, 0, 0))
    return pl.pallas_call(
        _in_kernel,
        grid=(B // bb, T // tt),
        in_specs=[row(D), mod, mod,
                  pl.BlockSpec((1, D), lambda b, j: (0, 0)),
                  pl.BlockSpec(wpack.shape, lambda b, j: (0, 0))],
        out_specs=[row(n) for n in IN_WIDTHS],
        out_shape=[jax.ShapeDtypeStruct((B, T, n), F32) for n in IN_WIDTHS],
        compiler_params=_params(("arbitrary", "arbitrary")),
        name="in_proj",
    )(x, sc, sh, g, wpack)


def _rwkv_kernel(u_ref, s0_ref, wkv0_ref, mu_ref, vec_ref, wup_ref, aup_ref, gup_ref,
                 y_ref, wkv_ref, zbuf, s_sc, *, L):
    j = pl.program_id(1)
    G, HD = GROUP_W, HEAD_DIM

    @pl.when(j == 0)
    def _():
        zbuf[7:8, :] = s0_ref[0]
        s_sc[...] = wkv0_ref[0]

    u = u_ref[0]
    zbuf[8:8 + L, :] = u
    prev = zbuf[7:7 + L, :]
    xs = u + (prev - u) * mu_ref[...]
    zbuf[7:8, :] = u[L - 1:L, :]

    r = xs[:, 0:G]
    k = xs[:, G:2 * G]
    v = xs[:, 2 * G:3 * G]
    wc = xs[:, 3 * G:3 * G + 64]
    ac = xs[:, 3 * G + 64:3 * G + 128]
    gc = xs[:, 3 * G + 128:4 * G]
    w0, a0, k_k, k_a = vec_ref[0:1, :], vec_ref[1:2, :], vec_ref[2:3, :], vec_ref[3:4, :]
    ln_g, ln_b, r_k = vec_ref[4:5, :], vec_ref[5:6, :], vec_ref[6:7, :]

    w_log = -_softplus(-(w0 + _bdot(jnp.tanh(wc), wup_ref[...]))) - 0.5
    lw = -jnp.exp(w_log)
    a = _sigmoid(a0 + _bdot(ac, aup_ref[...]))
    g = _bdot(_sigmoid(gc), gup_ref[...])
    bo = _block_ones()
    kk = k * k_k
    kk = kk / jnp.maximum(jnp.sqrt(_head_sum(kk * kk, bo)), 1e-12)
    k2 = k * (1.0 + (a - 1.0) * k_a)
    beta = kk * a

    row = _iota((L, L), 0)
    col = _iota((L, L), 1)
    incl = row >= col
    strict = row > col
    c = _dot_exact_lhs(incl.astype(F32), lw)
    cl = c[L - 1:L, :]
    e_ex = jnp.exp(c - lw)
    e_in = jnp.exp(c)
    e_inv = jnp.exp(-c)
    e_tail = jnp.exp(cl - c)
    a_t = -kk * e_ex
    b_t = beta * e_inv
    k_t = k2 * e_inv
    r_t = r * e_in
    b_l = beta * e_tail
    k_l = k2 * e_tail
    e_cl = jnp.exp(cl)
    eye = (row == col).astype(F32)
    n_double = max(1, int(math.ceil(math.log2(L)))) - 1

    ys = []
    for h in range(N_HEADS):
        sl = slice(h * HD, (h + 1) * HD)
        a_h, b_h, k_h, r_h, v_h = a_t[:, sl], b_t[:, sl], k_t[:, sl], r_t[:, sl], v[:, sl]
        n_ab = jnp.where(strict, _dot3(a_h, b_h, NT), 0.0)
        n_ak = jnp.where(strict, _dot3(a_h, k_h, NT), 0.0)
        n_rb = jnp.where(incl, _dot3(r_h, b_h, NT), 0.0)
        n_rk = jnp.where(incl, _dot3(r_h, k_h, NT), 0.0)
        p = eye + n_ab
        m = n_ab
        for _ in range(n_double):
            m = _dot3(m, m)
            p = p + _dot3(m, p)
        w_t = _dot3(p, a_h)
        u_t = _dot3(p, _dot3(n_ak, v_h))
        s_h = s_sc[h]
        e = _dot3(w_t, s_h, NT) + u_t
        ys.append(_dot3(r_h, s_h, NT) + _dot3(n_rb, e) + _dot3(n_rk, v_h))
        s_sc[h] = s_h * e_cl[:, sl] + _dot3(e, b_l[:, sl], TN) + _dot3(v_h, k_l[:, sl], TN)

    y = jnp.concatenate(ys, axis=-1)
    mu_y = _head_sum(y, bo) * (1.0 / HD)
    yc = y - mu_y
    var = _head_sum(yc * yc, bo) * (1.0 / HD)
    yn = yc * lax.rsqrt(var + RW_GN_EPS) * ln_g + ln_b
    bonus = _head_sum(r * k2 * r_k, bo) * v
    y_ref[0] = (yn + bonus) * g

    @pl.when(j == pl.num_programs(1) - 1)
    def _():
        wkv_ref[0] = s_sc[...]


def _rwkv_call(u_rw, shift0, wkv0, mu, vec, wup, aup, gup):
    B, T, _ = u_rw.shape
    L = math.gcd(T, RW_CHUNK)
    full = lambda a: pl.BlockSpec(a.shape, lambda b, j: (0,) * a.ndim)
    return pl.pallas_call(
        functools.partial(_rwkv_kernel, L=L),
        grid=(B, T // L),
        in_specs=[pl.BlockSpec((1, L, RW_COLS), lambda b, j: (b, j, 0)),
                  pl.BlockSpec((1, 1, RW_COLS), lambda b, j: (b, 0, 0)),
                  pl.BlockSpec((1, N_HEADS, HEAD_DIM, HEAD_DIM), lambda b, j: (b, 0, 0, 0)),
                  full(mu), full(vec), full(wup), full(aup), full(gup)],
        out_specs=[pl.BlockSpec((1, L, GROUP_W), lambda b, j: (b, j, 0)),
                   pl.BlockSpec((1, N_HEADS, HEAD_DIM, HEAD_DIM), lambda b, j: (b, 0, 0, 0))],
        out_shape=[jax.ShapeDtypeStruct((B, T, GROUP_W), F32),
                   jax.ShapeDtypeStruct((B, N_HEADS, HEAD_DIM, HEAD_DIM), F32)],
        scratch_shapes=[pltpu.VMEM((8 + L, RW_COLS), F32),
                        pltpu.VMEM((N_HEADS, HEAD_DIM, HEAD_DIM), F32)],
        compiler_params=_params(("arbitrary", "arbitrary")),
        name="rwkv7",
    )(u_rw, shift0, wkv0, mu, vec, wup, aup, gup)


def _mlstm_kernel(u_ref, gt_ref, cb_ref, c0_ref, n0_ref, m0_ref, cw_ref, cbias_ref, gbias_ref, ng_ref,
                  y_ref, c_ref, n_ref, m_ref, zbuf, c_sc, n_sc, m_sc, *, L):
    j = pl.program_id(1)
    G, HD = GROUP_W, HEAD_DIM

    @pl.when(j == 0)
    def _():
        zbuf[5:8, :] = cb_ref[0]
        c_sc[...] = c0_ref[0]
        n_sc[...] = n0_ref[0]
        m_sc[...] = m0_ref[0]

    u = u_ref[0]
    zbuf[8:8 + L, :] = u[:, 0:2 * G]
    conv = cbias_ref[...]
    for t in range(4):
        conv = conv + zbuf[5 + t:5 + t + L, :] * cw_ref[t:t + 1, :]
    tail = zbuf[5 + L:8 + L, :]
    zbuf[5:8, :] = tail
    sq = conv * _sigmoid(conv)
    q = sq[:, 0:G]
    k = sq[:, G:2 * G] * (HD ** -0.5)
    v = u[:, 2 * G:3 * G]
    o = u[:, 3 * G:4 * G]

    gates = gt_ref[0] + gbias_ref[...]
    lane = _iota((1, GATE_PAD), 1)
    gl = jnp.where(lane < N_HEADS, gates, -_softplus(-gates))
    row = _iota((L, L), 0)
    col = _iota((L, L), 1)
    causal = row >= col
    bcol = _dot_exact_lhs(causal.astype(F32), gl)
    sel = jnp.concatenate([(row <= col).astype(F32), (row == col).astype(F32)], axis=1)
    brow = _dot_exact_rhs(gl, sel, TN)

    m_vec = m_sc[...]
    m_out = m_vec
    hs = []
    for h in range(N_HEADS):
        sl = slice(h * HD, (h + 1) * HD)
        bc = bcol[:, N_HEADS + h:N_HEADS + h + 1]
        br = brow[N_HEADS + h:N_HEADS + h + 1, 0:L]
        ir = brow[h:h + 1, L:2 * L]
        ic = gl[:, h:h + 1]
        m = m_vec[:, h:h + 1]
        q_h, k_h, v_h = q[:, sl], k[:, sl], v[:, sl]
        log_d = jnp.where(causal, bc - br + ir, -jnp.inf)
        inter = bc + m
        m_row = jnp.maximum(jnp.max(log_d, -1, keepdims=True), inter)
        s = _dot3(q_h, k_h, NT) * jnp.exp(log_d - m_row)
        w_int = jnp.exp(inter - m_row)
        c_h = c_sc[h]
        n_h = n_sc[h:h + 1, :]
        num = _dot3(s, v_h) + w_int * _dot3(q_h, c_h, NT)
        den = jnp.sum(s, -1, keepdims=True) + w_int * jnp.sum(q_h * n_h, -1, keepdims=True)
        hs.append(num / jnp.maximum(jnp.abs(den), jnp.exp(-m_row)))
        g_tot = bc[L - 1:L, :]
        logw = g_tot - bc + ic
        m_new = jnp.maximum(g_tot + m, jnp.max(logw, 0, keepdims=True))
        wts = jnp.exp(logw - m_new)
        dec = jnp.exp(g_tot + m - m_new)
        c_sc[h] = dec * c_h + _dot3(v_h * wts, k_h, TN)
        n_sc[h:h + 1, :] = dec * n_h + jnp.sum(k_h * wts, 0, keepdims=True)
        m_out = jnp.where(lane == h, m_new, m_out)
    m_sc[...] = m_out

    hcat = jnp.concatenate(hs, axis=-1)
    bo = _block_ones()
    mu_h = _head_sum(hcat, bo) * (1.0 / HD)
    hc = hcat - mu_h
    var = _head_sum(hc * hc, bo) * (1.0 / HD)
    y_ref[0] = hc * lax.rsqrt(var + ML_NORM_EPS) * ng_ref[...] * _sigmoid(o)

    @pl.when(j == pl.num_programs(1) - 1)
    def _():
        c_ref[0] = c_sc[...]
        n_ref[0] = n_sc[...]
        m_ref[0] = m_sc[...]


def _mlstm_call(u_ml, gates, conv0, c0, n0, m0, cw, cbias, gbias, ng):
    B, T, _ = u_ml.shape
    L = math.gcd(T, ML_CHUNK)
    full = lambda a: pl.BlockSpec(a.shape, lambda b, j: (0,) * a.ndim)
    st4 = pl.BlockSpec((1, N_HEADS, HEAD_DIM, HEAD_DIM), lambda b, j: (b, 0, 0, 0))
    st3 = pl.BlockSpec((1, N_HEADS, HEAD_DIM), lambda b, j: (b, 0, 0))
    stm = pl.BlockSpec((1, 1, GATE_PAD), lambda b, j: (b, 0, 0))
    return pl.pallas_call(
        functools.partial(_mlstm_kernel, L=L),
        grid=(B, T // L),
        in_specs=[pl.BlockSpec((1, L, ML_MAIN), lambda b, j: (b, j, 0)),
                  pl.BlockSpec((1, L, GATE_PAD), lambda b, j: (b, j, 0)),
                  pl.BlockSpec((1, 3, 2 * GROUP_W), lambda b, j: (b, 0, 0)),
                  st4, st3, stm, full(cw), full(cbias), full(gbias), full(ng)],
        out_specs=[pl.BlockSpec((1, L, GROUP_W), lambda b, j: (b, j, 0)), st4, st3, stm],
        out_shape=[jax.ShapeDtypeStruct((B, T, GROUP_W), F32),
                   jax.ShapeDtypeStruct((B, N_HEADS, HEAD_DIM, HEAD_DIM), F32),
                   jax.ShapeDtypeStruct((B, N_HEADS, HEAD_DIM), F32),
                   jax.ShapeDtypeStruct((B, 1, GATE_PAD), F32)],
        scratch_shapes=[pltpu.VMEM((8 + L, 2 * GROUP_W), F32),
                        pltpu.VMEM((N_HEADS, HEAD_DIM, HEAD_DIM), F32),
                        pltpu.VMEM((N_HEADS, HEAD_DIM), F32),
                        pltpu.VMEM((1, GATE_PAD), F32)],
        compiler_params=_params(("arbitrary", "arbitrary")),
        name="mlstm",
    )(u_ml, gates, conv0, c0, n0, m0, cw, cbias, gbias, ng)


def _pool_kernel(u_ref, hist_ref, w_ref, scale_ref, y_ref, zbuf, *, pos0):
    j = pl.program_id(1)
    bb, tt, G = u_ref.shape
    gc = G // len(POOL_WINDOWS)

    @pl.when(j == 0)
    def _():
        zbuf[:, 0:1, :] = jnp.zeros((bb, 1, G), F32)
        zbuf[:, 1:16, :] = hist_ref[...]

    u = u_ref[...]
    zbuf[:, 16:16 + tt, :] = u
    lane = _iota((1, 1, G), 2)
    pos = pos0 + j * tt + _iota((1, tt, 1), 1)
    acc = jnp.zeros((bb, tt, G), F32)
    pooled = jnp.zeros((bb, tt, G), F32)
    for t in range(max(POOL_WINDOWS)):
        acc = acc + zbuf[:, 16 - t:16 - t + tt, :]
        if (t + 1) in POOL_WINDOWS:
            gi = POOL_WINDOWS.index(t + 1)
            cnt = jnp.minimum(pos + 1, t + 1).astype(F32)
            pooled = jnp.where(lane // gc == gi, acc / cnt, pooled)
    tail = zbuf[:, tt:tt + 16, :]
    zbuf[:, 0:16, :] = tail
    pooled = (pooled - u).reshape(bb * tt, G)
    y_ref[...] = (_bdot(pooled, w_ref[...]) * scale_ref[...]).reshape(bb, tt, G)


def _pool_call(u_pool, hist, wbd, scale, pos0):
    B, T, G = u_pool.shape
    bb, tt = _row_blocks(B, T)
    return pl.pallas_call(
        functools.partial(_pool_kernel, pos0=pos0),
        grid=(B // bb, T // tt),
        in_specs=[pl.BlockSpec((bb, tt, G), lambda b, j: (b, j, 0)),
                  pl.BlockSpec((bb, POOL_HIST, G), lambda b, j: (b, 0, 0)),
                  pl.BlockSpec((G, G), lambda b, j: (0, 0)),
                  pl.BlockSpec((1, G), lambda b, j: (0, 0))],
        out_specs=pl.BlockSpec((bb, tt, G), lambda b, j: (b, j, 0)),
        out_shape=jax.ShapeDtypeStruct((B, T, G), F32),
        scratch_shapes=[pltpu.VMEM((bb, 16 + tt, G), F32)],
        compiler_params=_params(("arbitrary", "arbitrary")),
        name="pool",
    )(u_pool, hist, wbd, scale)


def _slope_of(head):
    return jnp.where(head == 0, 2.0 ** -2, jnp.where(head == 1, 2.0 ** -4, jnp.where(head == 2, 2.0 ** -6, 2.0 ** -8)))


def _topk_mask(gate, valid, nblk):
    blk = _iota((1, nblk), 1)
    cnt = jnp.zeros(gate.shape, F32)
    for n in range(nblk):
        gn = gate[:, n:n + 1]
        ahead = (gn > gate) | ((gn == gate) & (n < blk))
        cnt = cnt + jnp.where(ahead, 1.0, 0.0)
    return valid & (cnt < MOBA_TOPK)


def _moba_p_kernel(q_ref, k_ref, v_ref, o_ref, mean_sc, *, nblk):
    i = pl.program_id(1)
    BLK, G, HD = MOBA_BLOCK, GROUP_W, HEAD_DIM

    @pl.when(i == 0)
    def _():
        for n in range(nblk):
            mean_sc[n:n + 1, :] = jnp.mean(k_ref[0, n * BLK:(n + 1) * BLK, :], axis=0, keepdims=True)

    q = q_ref[0]
    means = mean_sc[...]
    lane_head = _iota((1, G), 1) // HD
    blk = _iota((1, nblk), 1)
    valid = blk < i
    t_minus_s = (_iota((BLK, BLK), 0) - _iota((BLK, BLK), 1)).astype(F32)
    causal = _iota((BLK, BLK), 0) >= _iota((BLK, BLK), 1)
    start = pl.multiple_of(i * BLK, BLK)
    k_own = k_ref[0, pl.ds(start, BLK), :].astype(BF16)
    v_own = v_ref[0, pl.ds(start, BLK), :].astype(BF16)
    scale = HD ** -0.5

    out = jnp.zeros((BLK, G), F32)
    for h in range(N_HEADS):
        slope = 2.0 ** (-2.0 * (h + 1))
        qm = jnp.where(lane_head == h, q, 0.0)
        gate = jnp.where(valid, _dot3(qm, means, NT), -jnp.inf)
        self_f = jnp.where(_topk_mask(gate, valid, nblk), 1.0, 0.0)
        qb = qm.astype(BF16)
        s = _dot(qb, k_own, NT) * scale - slope * t_minus_s
        s = jnp.where(causal, s, -jnp.inf)
        m0 = jnp.max(s, -1, keepdims=True)
        p = jnp.exp(s - m0)
        l0 = jnp.sum(p, -1, keepdims=True)
        acc0 = _dot(p.astype(BF16), v_own)

        def body(jb, carry):
            m, l, acc = carry
            st = pl.multiple_of(jb * BLK, BLK)
            kb = k_ref[0, pl.ds(st, BLK), :].astype(BF16)
            vb = v_ref[0, pl.ds(st, BLK), :].astype(BF16)
            picked = jnp.sum(jnp.where(blk == jb, self_f, 0.0), -1, keepdims=True) > 0.0
            off = ((i - jb) * BLK).astype(F32)
            s = _dot(qb, kb, NT) * scale - slope * (t_minus_s + off)
            s = jnp.where(picked, s, -jnp.inf)
            m2 = jnp.maximum(m, jnp.max(s, -1, keepdims=True))
            alpha = jnp.exp(m - m2)
            p = jnp.exp(s - m2)
            return m2, alpha * l + jnp.sum(p, -1, keepdims=True), alpha * acc + _dot(p.astype(BF16), vb)

        m, l, acc = lax.fori_loop(0, i, body, (m0, l0, acc0))
        out = jnp.where(lane_head == h, acc / l, out)
    o_ref[0] = out


def _moba_prompt_call(q, k, v):
    B, T, G = q.shape
    assert T % MOBA_BLOCK == 0
    nblk = T // MOBA_BLOCK
    return pl.pallas_call(
        functools.partial(_moba_p_kernel, nblk=nblk),
        grid=(B, nblk),
        in_specs=[pl.BlockSpec((1, MOBA_BLOCK, G), lambda b, i: (b, i, 0)),
                  pl.BlockSpec((1, T, G), lambda b, i: (b, 0, 0)),
                  pl.BlockSpec((1, T, G), lambda b, i: (b, 0, 0))],
        out_specs=pl.BlockSpec((1, MOBA_BLOCK, G), lambda b, i: (b, i, 0)),
        out_shape=jax.ShapeDtypeStruct((B, T, G), F32),
        scratch_shapes=[pltpu.VMEM((nblk, G), F32)],
        compiler_params=_params(("arbitrary", "arbitrary")),
        name="moba_prompt",
    )(q, k, v)


def _moba_s_kernel(pt_ref, q_ref, kn_ref, vn_ref, *refs, n_pages, page, past_len):
    kp = refs[:n_pages]
    vp = refs[n_pages:2 * n_pages]
    o_ref = refs[2 * n_pages]
    G, HD = GROUP_W, HEAD_DIM
    q = q_ref[0]
    ts = q.shape[0]
    R = N_HEADS * ts
    ppb = MOBA_BLOCK // page
    nb = n_pages // ppb
    lane_head = _iota((1, G), 1) // HD
    qbd = jnp.concatenate([jnp.where(lane_head == h, q, 0.0) for h in range(N_HEADS)], axis=0)
    row_head = _iota((R, 1), 0) // ts
    row_t = _iota((R, 1), 0) % ts
    slope = _slope_of(row_head)
    scale = HD ** -0.5
    qb = qbd.astype(BF16)

    raw = []
    sums = []
    for p in range(n_pages):
        kpg = kp[p][...]
        raw.append(_dot(qb, kpg.astype(BF16), NT))
        sums.append(jnp.sum(kpg, axis=0, keepdims=True))
    means = jnp.concatenate(
        [sum(sums[n * ppb:(n + 1) * ppb]) * (1.0 / MOBA_BLOCK) for n in range(nb)], axis=0)
    gate = _dot3(qbd, means, NT)
    all_valid = _iota((1, nb), 1) >= 0
    sel = jnp.where(_topk_mask(gate, all_valid, nb), 1.0, 0.0)

    q_pos = (past_len + row_t).astype(F32)
    off = _iota((1, page), 1).astype(F32)
    scores = []
    for p in range(n_pages):
        n = p // ppb
        dist = q_pos - (p * page + off)
        s = raw[p] * scale - slope * dist
        scores.append(jnp.where(sel[:, n:n + 1] > 0.0, s, -jnp.inf))
    kn = kn_ref[0]
    vn = vn_ref[0]
    t_new = _iota((1, ts), 1)
    s_own = _dot(qb, kn.astype(BF16), NT) * scale - slope * (row_t - t_new).astype(F32)
    s_own = jnp.where(t_new <= row_t, s_own, -jnp.inf)

    m = jnp.max(s_own, -1, keepdims=True)
    for s in scores:
        m = jnp.maximum(m, jnp.max(s, -1, keepdims=True))
    p_own = jnp.exp(s_own - m)
    l = jnp.sum(p_own, -1, keepdims=True)
    acc = _dot(p_own.astype(BF16), vn.astype(BF16))
    for p in range(n_pages):
        pr = jnp.exp(scores[p] - m)
        l = l + jnp.sum(pr, -1, keepdims=True)
        acc = acc + _dot(pr.astype(BF16), vp[p][...].astype(BF16))
    out = acc / l
    y = jnp.zeros((ts, G), F32)
    for h in range(N_HEADS):
        y = jnp.where(lane_head == h, out[h * ts:(h + 1) * ts, :], y)
    o_ref[0] = y


def _moba_sample_call(q, k, v, cache_k, cache_v, page_table, layer):
    DB, TS, G = q.shape
    Ld, n_phys, page, H, d = cache_k.shape
    n_pages = page_table.shape[1]
    past_len = n_pages * page
    assert past_len % MOBA_BLOCK == 0 and MOBA_BLOCK % page == 0 and past_len // MOBA_BLOCK >= MOBA_TOPK
    ck = cache_k.reshape(Ld, n_phys, page, H * d)
    cv = cache_v.reshape(Ld, n_phys, page, H * d)
    new = pl.BlockSpec((1, TS, G), lambda b, pt: (b, 0, 0))
    pg = [pl.BlockSpec((None, None, page, G), functools.partial(lambda b, pt, p: (layer, pt[b, p], 0, 0), p=p))
          for p in range(n_pages)]
    return pl.pallas_call(
        functools.partial(_moba_s_kernel, n_pages=n_pages, page=page, past_len=past_len),
        grid_spec=pltpu.PrefetchScalarGridSpec(
            num_scalar_prefetch=1, grid=(DB,),
            in_specs=[new, new, new] + pg + pg,
            out_specs=new),
        out_shape=jax.ShapeDtypeStruct((DB, TS, G), F32),
        compiler_params=_params(("arbitrary",)),
        name="moba_sample",
    )(page_table, q, k, v, *([ck] * n_pages), *([cv] * n_pages))


def _out_kernel(yrw_ref, yml_ref, ypl_ref, yat_ref, x_ref, g1_ref, sc2_ref, sh2_ref, g2_ref, n2g_ref, fng_ref,
                wout_ref, wup_ref, wdn_ref, o_ref, x1_sc, h2_sc, acc_sc, *, final):
    f = pl.program_id(2)
    bb, tt, D = x_ref.shape
    M = bb * tt

    @pl.when(f == 0)
    def _():
        ycat = jnp.concatenate([r[...].reshape(M, GROUP_W) for r in (yrw_ref, yml_ref, ypl_ref, yat_ref)], axis=-1)
        y = _dot(ycat.astype(BF16), wout_ref[...]).reshape(bb, tt, D)
        x1 = x_ref[...] + g1_ref[...] * y
        x1_sc[...] = x1
        hn = x1 * lax.rsqrt(jnp.mean(x1 * x1, -1, keepdims=True) + NORM_EPS) * n2g_ref[...]
        h2_sc[...] = (hn * (1.0 + sc2_ref[...]) + sh2_ref[...]).reshape(M, D).astype(BF16)
        acc_sc[...] = jnp.zeros((M, D), F32)

    a = _dot(h2_sc[...], wup_ref[...])
    a = jnp.square(jnp.maximum(a, 0.0)).astype(BF16)
    acc_sc[...] += _dot(a, wdn_ref[...])

    @pl.when(f == pl.num_programs(2) - 1)
    def _():
        x2 = x1_sc[...] + g2_ref[...] * acc_sc[...].reshape(bb, tt, D)
        if final:
            x2 = x2 * lax.rsqrt(jnp.mean(x2 * x2, -1, keepdims=True) + NORM_EPS) * fng_ref[...]
        o_ref[...] = x2


def _out_call(ys, x, g1, sc2, sh2, g2, n2g, fng, wout, wup, wdn, final):
    B, T, D = x.shape
    bb, tt = _row_blocks(B, T)
    FF = wup.shape[1]
    row = lambda n: pl.BlockSpec((bb, tt, n), lambda b, j, f: (b, j, 0))
    mod = pl.BlockSpec((bb, 1, D), lambda b, j, f: (b, 0, 0))
    vec = pl.BlockSpec((1, D), lambda b, j, f: (0, 0))
    return pl.pallas_call(
        functools.partial(_out_kernel, final=final),
        grid=(B // bb, T // tt, FF // FF_TILE),
        in_specs=[row(GROUP_W)] * 4 + [row(D), mod, mod, mod, mod, vec, vec,
                  pl.BlockSpec((D, D), lambda b, j, f: (0, 0)),
                  pl.BlockSpec((D, FF_TILE), lambda b, j, f: (0, f)),
                  pl.BlockSpec((FF_TILE, D), lambda b, j, f: (f, 0))],
        out_specs=row(D),
        out_shape=jax.ShapeDtypeStruct((B, T, D), F32),
        scratch_shapes=[pltpu.VMEM((bb, tt, D), F32),
                        pltpu.VMEM((bb * tt, D), BF16),
                        pltpu.VMEM((bb * tt, D), F32)],
        compiler_params=_params(("arbitrary", "arbitrary", "arbitrary")),
        name="out_mlp",
    )(*ys, x, g1, sc2, sh2, g2, n2g, fng, wout, wup, wdn)


_RW_PERM = np.concatenate([np.arange(0, 256), np.arange(320, 576), np.arange(576, 832),
                           np.arange(256, 320), np.arange(832, 896), np.arange(896, 1024)])
_RW_INV = np.argsort(_RW_PERM)


def _layer(x, mods, st, pos0, attend, lw, final):
    sh1, sc1, g1, sh2, sc2, g2 = mods
    shift0, wkv0, conv0, c0, n0, m0, pool0 = st
    B, T, D = x.shape
    u_rw, u_ml, gates, u_pool, q, k, v = _in_call(x, sc1, sh1, lw['norm1_g'], lw['wpack'])

    y_rw, wkv1 = _rwkv_call(u_rw, shift0[:, _RW_PERM][:, None, :], wkv0, lw['rw_mu'], lw['rw_vec'],
                            lw['rwkv_w_up'], lw['rwkv_a_up'], lw['rwkv_g_up'])
    shift1 = u_rw[:, -1, :][:, _RW_INV]

    m0p = jnp.pad(m0, ((0, 0), (0, GATE_PAD - N_HEADS)))[:, None, :]
    y_ml, c1, n1, m1p = _mlstm_call(u_ml, gates, conv0, c0, n0, m0p, lw['mlstm_conv_w'], lw['ml_cbias'],
                                    lw['ml_gbias'], lw['ml_ng'])
    m1 = m1p[:, 0, :N_HEADS]
    zc = u_ml[:, :, :2 * GROUP_W] if T >= 3 else jnp.concatenate([conv0, u_ml[:, :, :2 * GROUP_W]], 1)
    conv1 = zc[:, -3:]

    y_pool = _pool_call(u_pool, pool0, lw['pool_wbd'], lw['pool_scale'], pos0)
    zp = u_pool if T >= POOL_HIST else jnp.concatenate([pool0, u_pool], 1)
    pool1 = zp[:, -POOL_HIST:]

    y_at = attend(q, k, v)

    x_new = _out_call((y_rw, y_ml, y_pool, y_at), x, g1, sc2, sh2, g2, lw['norm2_g'], lw['final_g'],
                      lw['w_out'], lw['mlp_up'], lw['mlp_down'], final)
    kv_shape = (B, T, N_HEADS, HEAD_DIM)
    return x_new, (shift1, wkv1, conv1, c1, n1, m1, pool1, k.reshape(kv_shape), v.reshape(kv_shape))


def kernel(x_prompt, x_sample, c_prompt, c_sample, state_rwkv_shift, state_rwkv_wkv, state_mlstm_conv, state_mlstm_c, state_mlstm_n, state_mlstm_m, state_pool, cache_k, cache_v, page_table, ada_w, ada_b, norm1_g, norm2_g, w_in, w_out, rwkv_mu, rwkv_w0, rwkv_w_up, rwkv_a0, rwkv_a_up, rwkv_g_up, rwkv_k_k, rwkv_k_a, rwkv_r_k, rwkv_ln_g, rwkv_ln_b, mlstm_conv_w, mlstm_conv_b, mlstm_i_b, mlstm_f_b, mlstm_norm_g, pool_w, pool_scale, mlp_up, mlp_down, final_norm_g):
    B, T, D = x_prompt.shape
    DB = x_sample.shape[0]
    depth = ada_w.shape[0]
    G = GROUP_W
    assert D == D_MODEL and w_in.shape[-1] == N_IN

    mod = _ada_call(jnp.concatenate([c_prompt, c_sample], 0), ada_w, ada_b)

    st_p0 = (jnp.zeros((B, RW_COLS), F32), jnp.zeros((B, N_HEADS, HEAD_DIM, HEAD_DIM), F32),
             jnp.zeros((B, 3, 2 * G), F32), jnp.zeros((B, N_HEADS, HEAD_DIM, HEAD_DIM), F32),
             jnp.zeros((B, N_HEADS, HEAD_DIM), F32), jnp.zeros((B, N_HEADS), F32),
             jnp.zeros((B, POOL_HIST, G), F32))
    past_len = page_table.shape[1] * cache_k.shape[2]

    xp, xs = x_prompt, x_sample
    new_p, new_s = [], []
    for l in range(depth):
        wl = w_in[l]
        wpack = jnp.concatenate([
            wl[:, :RW_COLS][:, _RW_PERM],
            wl[:, RW_COLS:RW_COLS + ML_MAIN],
            jnp.pad(wl[:, RW_COLS + ML_MAIN:RW_COLS + ML_MAIN + ML_GATES], ((0, 0), (0, GATE_PAD - ML_GATES))),
            wl[:, RW_COLS + ML_MAIN + ML_GATES:],
        ], axis=1).astype(BF16)
        pool_wbd = jnp.zeros((G, G), F32)
        gcw = G // len(POOL_WINDOWS)
        for gi in range(len(POOL_WINDOWS)):
            pool_wbd = pool_wbd.at[gi * gcw:(gi + 1) * gcw, gi * gcw:(gi + 1) * gcw].set(pool_w[l, gi])
        row = lambda a: a.reshape(1, -1)
        lw = {
            'norm1_g': row(norm1_g[l]), 'norm2_g': row(norm2_g[l]), 'final_g': row(final_norm_g),
            'wpack': wpack, 'w_out': w_out[l].astype(BF16),
            'mlp_up': mlp_up[l].astype(BF16), 'mlp_down': mlp_down[l].astype(BF16),
            'rw_mu': row(rwkv_mu[l][_RW_PERM]),
            'rw_vec': jnp.stack([rwkv_w0[l], rwkv_a0[l], rwkv_k_k[l], rwkv_k_a[l], rwkv_ln_g[l], rwkv_ln_b[l],
                                 rwkv_r_k[l].reshape(-1), jnp.zeros((G,), F32)]),
            'rwkv_w_up': rwkv_w_up[l], 'rwkv_a_up': rwkv_a_up[l], 'rwkv_g_up': rwkv_g_up[l],
            'mlstm_conv_w': mlstm_conv_w[l], 'ml_cbias': row(mlstm_conv_b[l]),
            'ml_gbias': row(jnp.pad(jnp.concatenate([mlstm_i_b[l], mlstm_f_b[l]]), (0, GATE_PAD - ML_GATES))),
            'ml_ng': row(mlstm_norm_g[l]),
            'pool_wbd': pool_wbd, 'pool_scale': row(pool_scale[l]),
        }
        mods = [mod[l, :, i * D:(i + 1) * D][:, None, :] for i in range(6)]
        mods_p = [m[:B] for m in mods]
        mods_s = [m[B:] for m in mods]
        final = l == depth - 1

        xp, st_p = _layer(xp, mods_p, st_p0, 0, _moba_prompt_call, lw, final)
        st_s_in = (state_rwkv_shift[l], state_rwkv_wkv[l], state_mlstm_conv[l], state_mlstm_c[l],
                   state_mlstm_n[l], state_mlstm_m[l], state_pool[l])
        attend_s = functools.partial(_moba_sample_call, cache_k=cache_k, cache_v=cache_v,
                                     page_table=page_table, layer=l)
        xs, st_s = _layer(xs, mods_s, st_s_in, past_len, attend_s, lw, final)
        new_p.append(st_p)
        new_s.append(st_s)

    stack = lambda lst: tuple(jnp.stack([st[i] for st in lst]) for i in range(9))
    return (xp, xs) + stack(new_p) + stack(new_s)
```

```python
import functools
import math

import jax
import jax.numpy as jnp
import numpy as np
from jax import lax
from jax.experimental import pallas as pl
from jax.experimental.pallas import tpu as pltpu

F32 = jnp.float32
BF16 = jnp.bfloat16

D_MODEL = 1024
GROUP_W = 256
HEAD_DIM = 64
N_HEADS = 4
RW_COLS = 1024
ML_MAIN = 1024
ML_GATES = 8
GATE_PAD = 128
N_IN = 3080
MOBA_BLOCK = 256
MOBA_TOPK = 3
ML_CHUNK = 64
RW_CHUNK = 64
RW_CHUNKS_PER_STEP = 4
ML_CHUNKS_PER_STEP = 4
POOL_WINDOWS = (2, 4, 8, 16)
POOL_HIST = 15
NORM_EPS = 1e-6
RW_GN_EPS = 64e-5
ML_NORM_EPS = 1e-6
ROW_TILE = 512
FF_TILE = 1024
VMEM_LIMIT = 48 * 1024 * 1024

NN = (((1,), (0,)), ((), ()))
NT = (((1,), (1,)), ((), ()))
TN = (((0,), (0,)), ((), ()))


def _dot(a, b, dn=NN):
    return lax.dot_general(a, b, dn, preferred_element_type=F32)


def _bdot(a, b, dn=NN):
    return _dot(a.astype(BF16), b.astype(BF16), dn)


def _split2(x):
    hi = x.astype(BF16)
    lo = (x - hi.astype(F32)).astype(BF16)
    return hi, lo


def _split3(x):
    hi = x.astype(BF16)
    r = x - hi.astype(F32)
    mid = r.astype(BF16)
    lo = (r - mid.astype(F32)).astype(BF16)
    return hi, mid, lo


def _dot3(a, b, dn=NN):
    ah, al = _split2(a)
    bh, bl = _split2(b)
    return _dot(ah, bh, dn) + (_dot(ah, bl, dn) + _dot(al, bh, dn))


def _dot3s(a_s, b_s, dn=NN):
    return _dot(a_s[0], b_s[0], dn) + (_dot(a_s[0], b_s[1], dn) + _dot(a_s[1], b_s[0], dn))


def _dot_exact_rhs(a, b01, dn=NN):
    h, m, l = _split3(a)
    b = b01.astype(BF16)
    return _dot(h, b, dn) + (_dot(m, b, dn) + _dot(l, b, dn))


def _dot_exact_lhs(a01, b, dn=NN):
    h, m, l = _split3(b)
    a = a01.astype(BF16)
    return _dot(a, h, dn) + (_dot(a, m, dn) + _dot(a, l, dn))


def _iota(shape, dim):
    return lax.broadcasted_iota(jnp.int32, shape, dim)


def _sigmoid(x):
    return jax.nn.sigmoid(x)


def _softplus(x):
    return jnp.maximum(x, 0.0) + jnp.log(1.0 + jnp.exp(-jnp.abs(x)))


def _block_ones():
    return (_iota((GROUP_W, GROUP_W), 0) // HEAD_DIM == _iota((GROUP_W, GROUP_W), 1) // HEAD_DIM).astype(F32)


def _head_sum(x, bo):
    return _dot_exact_rhs(x, bo)


def _params(sem):
    return pltpu.CompilerParams(dimension_semantics=sem, vmem_limit_bytes=VMEM_LIMIT)


def _row_blocks(B, T):
    if T >= ROW_TILE:
        assert T % ROW_TILE == 0
        return 1, ROW_TILE
    bb = max(1, min(B, ROW_TILE // T))
    while B % bb:
        bb -= 1
    return bb, T


def _ada_kernel(c_ref, w_ref, b_ref, o_ref):
    c = c_ref[...]
    o_ref[...] = _bdot(c * _sigmoid(c), w_ref[...]) + b_ref[...]


def _ada_call(c_all, ada_w, ada_b):
    Ld, D, N6 = ada_w.shape
    NB = c_all.shape[0]
    tn = 1024
    return pl.pallas_call(
        _ada_kernel,
        grid=(Ld, N6 // tn),
        in_specs=[pl.BlockSpec((NB, D), lambda l, j: (0, 0)),
                  pl.BlockSpec((None, D, tn), lambda l, j: (l, 0, j)),
                  pl.BlockSpec((None, 1, tn), lambda l, j: (l, 0, j))],
        out_specs=pl.BlockSpec((None, NB, tn), lambda l, j: (l, 0, j)),
        out_shape=jax.ShapeDtypeStruct((Ld, NB, N6), F32),
        compiler_params=_params(("arbitrary", "arbitrary")),
        name="ada_mod",
    )(c_all, ada_w, ada_b.reshape(Ld, 1, N6))


IN_WIDTHS = (RW_COLS, ML_MAIN, GATE_PAD, GROUP_W, GROUP_W, GROUP_W, GROUP_W)


def _in_kernel(x_ref, sc_ref, sh_ref, g_ref, w_ref, *out_refs):
    x = x_ref[...]
    bb, tt, D = x.shape
    y = x * lax.rsqrt(jnp.mean(x * x, -1, keepdims=True) + NORM_EPS) * g_ref[...]
    h = y * (1.0 + sc_ref[...]) + sh_ref[...]
    hb = h.reshape(bb * tt, D).astype(BF16)
    off = 0
    for ref in out_refs:
        n = ref.shape[-1]
        ref[...] = _dot(hb, w_ref[:, off:off + n]).reshape(bb, tt, n)
        off += n


def _in_call(x, sc, sh, g, wpack):
    B, T, D = x.shape
    bb, tt = _row_blocks(B, T)
    row = lambda n: pl.BlockSpec((bb, tt, n), lambda b, j: (b, j, 0))
    mod = pl.BlockSpec((bb, 1, D), lambda b, j: (b, 0, 0))
    return pl.pallas_call(
        _in_kernel,
        grid=(B // bb, T // tt),
        in_specs=[row(D), mod, mod,
                  pl.BlockSpec((1, D), lambda b, j: (0, 0)),
                  pl.BlockSpec(wpack.shape, lambda b, j: (0, 0))],
        out_specs=[row(n) for n in IN_WIDTHS],
        out_shape=[jax.ShapeDtypeStruct((B, T, n), F32) for n in IN_WIDTHS],
        compiler_params=_params(("arbitrary", "arbitrary")),
        name="in_proj",
    )(x, sc, sh, g, wpack)


def _rwkv_kernel(u_ref, s0_ref, wkv0_ref, mu_ref, vec_ref, wup_ref, aup_ref, gup_ref,
                 y_ref, wkv_ref, zbuf, s_sc, *, L, bb, nc):
    j = pl.program_id(1)
    G, HD = GROUP_W, HEAD_DIM
    TT = nc * L
    R = bb * TT

    @pl.when(j == 0)
    def _():
        zbuf[:, 7:8, :] = s0_ref[...]
        s_sc[...] = wkv0_ref[...]

    u3 = u_ref[...]
    zbuf[:, 8:8 + TT, :] = u3
    prev = zbuf[:, 7:7 + TT, :]
    xs = (u3 + (prev - u3) * mu_ref[...]).reshape(R, RW_COLS)
    zbuf[:, 7:8, :] = u3[:, TT - 1:TT, :]

    r = xs[:, 0:G]
    k = xs[:, G:2 * G]
    v = xs[:, 2 * G:3 * G]
    wc = xs[:, 3 * G:3 * G + 64]
    ac = xs[:, 3 * G + 64:3 * G + 128]
    gc = xs[:, 3 * G + 128:4 * G]
    w0, a0, k_k, k_a = vec_ref[0:1, :], vec_ref[1:2, :], vec_ref[2:3, :], vec_ref[3:4, :]
    ln_g, ln_b, r_k = vec_ref[4:5, :], vec_ref[5:6, :], vec_ref[6:7, :]

    w_log = -_softplus(-(w0 + _bdot(jnp.tanh(wc), wup_ref[...]))) - 0.5
    lw = -jnp.exp(w_log)
    a = _sigmoid(a0 + _bdot(ac, aup_ref[...]))
    g = _bdot(_sigmoid(gc), gup_ref[...])
    bo = _block_ones()
    kk = k * k_k
    kk = kk / jnp.maximum(jnp.sqrt(_head_sum(kk * kk, bo)), 1e-12)
    k2 = k * (1.0 + (a - 1.0) * k_a)
    beta = kk * a

    rr = _iota((R, R), 0)
    rc = _iota((R, R), 1)
    same = (rr // L) == (rc // L)
    c = _dot_exact_lhs((same & (rr >= rc)).astype(F32), lw)
    cl = _dot_exact_lhs(same.astype(F32), lw)
    e_inv = jnp.exp(-c)
    e_tail = jnp.exp(cl - c)
    a_t = -kk * jnp.exp(c - lw)
    b_t = beta * e_inv
    k_t = k2 * e_inv
    r_t = r * jnp.exp(c)
    b_l = beta * e_tail
    k_l = k2 * e_tail
    e_cl = jnp.exp(cl)

    row = _iota((L, L), 0)
    col = _iota((L, L), 1)
    incl = row >= col
    strict = row > col
    eye = (row == col).astype(F32)
    n_double = max(1, int(math.ceil(math.log2(L)))) - 1

    chains = [(ci, h) for ci in range(bb * nc) for h in range(N_HEADS)]

    def cut(x, ch):
        ci, h = ch
        return x[ci * L:(ci + 1) * L, h * HD:(h + 1) * HD]

    a_s = {ch: _split2(cut(a_t, ch)) for ch in chains}
    r_s = {ch: _split2(cut(r_t, ch)) for ch in chains}
    b_s = {ch: _split2(cut(b_t, ch)) for ch in chains}
    k_s = {ch: _split2(cut(k_t, ch)) for ch in chains}
    v_s = {ch: _split2(cut(v, ch)) for ch in chains}
    n_ab = {ch: jnp.where(strict, _dot3s(a_s[ch], b_s[ch], NT), 0.0) for ch in chains}
    n_ak = {ch: _split2(jnp.where(strict, _dot3s(a_s[ch], k_s[ch], NT), 0.0)) for ch in chains}
    n_rb = {ch: _split2(jnp.where(incl, _dot3s(r_s[ch], b_s[ch], NT), 0.0)) for ch in chains}
    n_rk = {ch: _split2(jnp.where(incl, _dot3s(r_s[ch], k_s[ch], NT), 0.0)) for ch in chains}
    p = {ch: eye + n_ab[ch] for ch in chains}
    m_s = {ch: _split2(n_ab[ch]) for ch in chains}
    for _ in range(n_double):
        m_s = {ch: _split2(_dot3s(m_s[ch], m_s[ch])) for ch in chains}
        p = {ch: p[ch] + _dot3s(m_s[ch], _split2(p[ch])) for ch in chains}
    p_s = {ch: _split2(p[ch]) for ch in chains}
    akv = {ch: _split2(_dot3s(n_ak[ch], v_s[ch])) for ch in chains}
    y_v = {ch: _dot3s(n_rk[ch], v_s[ch]) for ch in chains}
    w_s = {ch: _split2(_dot3s(p_s[ch], a_s[ch])) for ch in chains}
    u_t = {ch: _dot3s(p_s[ch], akv[ch]) for ch in chains}
    bl_s = {ch: _split2(cut(b_l, ch)) for ch in chains}
    kl_s = {ch: _split2(cut(k_l, ch)) for ch in chains}

    heads = range(N_HEADS)
    ys = []
    for b in range(bb):
        s = [s_sc[b, h] for h in heads]
        for cc in range(nc):
            ci = b * nc + cc
            s_s = [_split2(s[h]) for h in heads]
            e_s = [_split2(_dot3s(w_s[ci, h], s_s[h], NT) + u_t[ci, h]) for h in heads]
            s = [s[h] * e_cl[ci * L:ci * L + 1, h * HD:(h + 1) * HD]
                 + _dot3s(e_s[h], bl_s[ci, h], TN) + _dot3s(v_s[ci, h], kl_s[ci, h], TN) for h in heads]
            yh = [_dot3s(r_s[ci, h], s_s[h], NT) + _dot3s(n_rb[ci, h], e_s[h]) + y_v[ci, h] for h in heads]
            ys.append(jnp.concatenate(yh, axis=-1))
        for h in heads:
            s_sc[b, h] = s[h]

    y = jnp.concatenate(ys, axis=0) if len(ys) > 1 else ys[0]
    mu_y = _head_sum(y, bo) * (1.0 / HD)
    yc = y - mu_y
    var = _head_sum(yc * yc, bo) * (1.0 / HD)
    yn = yc * lax.rsqrt(var + RW_GN_EPS) * ln_g + ln_b
    bonus = _head_sum(r * k2 * r_k, bo) * v
    y_ref[...] = ((yn + bonus) * g).reshape(bb, TT, G)

    @pl.when(j == pl.num_programs(1) - 1)
    def _():
        wkv_ref[...] = s_sc[...]


def _rwkv_call(u_rw, shift0, wkv0, mu, vec, wup, aup, gup):
    B, T, _ = u_rw.shape
    L = math.gcd(T, RW_CHUNK)
    nc = math.gcd(T // L, RW_CHUNKS_PER_STEP)
    bb = 1 if nc > 1 else math.gcd(B, RW_CHUNKS_PER_STEP)
    TT = nc * L
    full = lambda a: pl.BlockSpec(a.shape, lambda b, j: (0,) * a.ndim)
    return pl.pallas_call(
        functools.partial(_rwkv_kernel, L=L, bb=bb, nc=nc),
        grid=(B // bb, T // TT),
        in_specs=[pl.BlockSpec((bb, TT, RW_COLS), lambda b, j: (b, j, 0)),
                  pl.BlockSpec((bb, 1, RW_COLS), lambda b, j: (b, 0, 0)),
                  pl.BlockSpec((bb, N_HEADS, HEAD_DIM, HEAD_DIM), lambda b, j: (b, 0, 0, 0)),
                  full(mu), full(vec), full(wup), full(aup), full(gup)],
        out_specs=[pl.BlockSpec((bb, TT, GROUP_W), lambda b, j: (b, j, 0)),
                   pl.BlockSpec((bb, N_HEADS, HEAD_DIM, HEAD_DIM), lambda b, j: (b, 0, 0, 0))],
        out_shape=[jax.ShapeDtypeStruct((B, T, GROUP_W), F32),
                   jax.ShapeDtypeStruct((B, N_HEADS, HEAD_DIM, HEAD_DIM), F32)],
        scratch_shapes=[pltpu.VMEM((bb, 8 + TT, RW_COLS), F32),
                        pltpu.VMEM((bb, N_HEADS, HEAD_DIM, HEAD_DIM), F32)],
        compiler_params=_params(("arbitrary", "arbitrary")),
        name="rwkv7",
    )(u_rw, shift0, wkv0, mu, vec, wup, aup, gup)


def _mlstm_kernel(u_ref, gt_ref, cb_ref, c0_ref, n0_ref, m0_ref, cw_ref, cbias_ref, gbias_ref, ng_ref,
                  y_ref, c_ref, n_ref, m_ref, zbuf, c_sc, n_sc, m_sc, *, L, bb, nc):
    j = pl.program_id(1)
    G, HD = GROUP_W, HEAD_DIM
    TT = nc * L
    R = bb * TT

    @pl.when(j == 0)
    def _():
        zbuf[:, 5:8, :] = cb_ref[...]
        c_sc[...] = c0_ref[...]
        n_sc[...] = n0_ref[...]
        m_sc[...] = m0_ref[...]

    u3 = u_ref[...]
    zbuf[:, 8:8 + TT, :] = u3[:, :, 0:2 * G]
    conv = cbias_ref[...]
    for t in range(4):
        conv = conv + zbuf[:, 5 + t:5 + t + TT, :] * cw_ref[t:t + 1, :]
    tail = zbuf[:, 5 + TT:8 + TT, :]
    zbuf[:, 5:8, :] = tail
    conv = conv.reshape(R, 2 * G)
    u = u3.reshape(R, ML_MAIN)
    sq = conv * _sigmoid(conv)
    q = sq[:, 0:G]
    k = sq[:, G:2 * G] * (HD ** -0.5)
    v = u[:, 2 * G:3 * G]
    o = u[:, 3 * G:4 * G]

    gates = gt_ref[...].reshape(R, GATE_PAD) + gbias_ref[...]
    lane = _iota((1, GATE_PAD), 1)
    gl = jnp.where(lane < N_HEADS, gates, -_softplus(-gates))
    rr = _iota((R, R), 0)
    rc = _iota((R, R), 1)
    same = (rr // L) == (rc // L)
    bcol = _dot_exact_lhs((same & (rr >= rc)).astype(F32), gl)
    gtot = _dot_exact_lhs(same.astype(F32), gl)
    sel = jnp.concatenate([(same & (rr <= rc)).astype(F32), (rr == rc).astype(F32)], axis=1)
    brow = _dot_exact_rhs(gl, sel, TN)
    causal = _iota((L, L), 0) >= _iota((L, L), 1)

    heads = range(N_HEADS)
    chains = [(ci, h) for ci in range(bb * nc) for h in heads]

    def cut(x, ch):
        ci, h = ch
        return x[ci * L:(ci + 1) * L, h * HD:(h + 1) * HD]

    def rows_of(x, ch, lane0):
        ci, h = ch
        return x[ci * L:(ci + 1) * L, lane0 + h:lane0 + h + 1]

    bc = {ch: rows_of(bcol, ch, N_HEADS) for ch in chains}
    ic = {ch: rows_of(gl, ch, 0) for ch in chains}
    g_tot = {(ci, h): gtot[ci * L:ci * L + 1, N_HEADS + h:N_HEADS + h + 1] for ci, h in chains}
    br = {(ci, h): brow[N_HEADS + h:N_HEADS + h + 1, ci * L:(ci + 1) * L] for ci, h in chains}
    ir = {(ci, h): brow[h:h + 1, R + ci * L:R + (ci + 1) * L] for ci, h in chains}
    q_f = {ch: cut(q, ch) for ch in chains}
    k_f = {ch: cut(k, ch) for ch in chains}
    q_s = {ch: _split2(q_f[ch]) for ch in chains}
    k_s = {ch: _split2(k_f[ch]) for ch in chains}
    v_f = {ch: cut(v, ch) for ch in chains}
    v_s = {ch: _split2(v_f[ch]) for ch in chains}
    log_d = {ch: jnp.where(causal, bc[ch] - br[ch] + ir[ch], -jnp.inf) for ch in chains}
    m_loc = {ch: jnp.max(log_d[ch], -1, keepdims=True) for ch in chains}
    s0 = {ch: _dot3s(q_s[ch], k_s[ch], NT) * jnp.exp(log_d[ch] - m_loc[ch]) for ch in chains}
    s_sum = {ch: jnp.sum(s0[ch], -1, keepdims=True) for ch in chains}
    sv = {ch: _dot3s(_split2(s0[ch]), v_s[ch]) for ch in chains}
    logw = {ch: g_tot[ch] - bc[ch] + ic[ch] for ch in chains}
    m_w = {ch: jnp.max(logw[ch], 0, keepdims=True) for ch in chains}
    w_loc = {ch: jnp.exp(logw[ch] - m_w[ch]) for ch in chains}
    kv = {ch: _dot3s(_split2(v_f[ch] * w_loc[ch]), k_s[ch], TN) for ch in chains}
    n_loc = {ch: jnp.sum(k_f[ch] * w_loc[ch], 0, keepdims=True) for ch in chains}

    hs = []
    for b in range(bb):
        m_vec = m_sc[b]
        m = [m_vec[:, h:h + 1] for h in heads]
        c = [c_sc[b, h] for h in heads]
        n = [n_sc[b, h:h + 1, :] for h in heads]
        for cc in range(nc):
            ci = b * nc + cc
            qc = [_dot3s(q_s[ci, h], _split2(c[h]), NT) for h in heads]
            hh = []
            for h in heads:
                ch = (ci, h)
                inter = bc[ch] + m[h]
                m_row = jnp.maximum(m_loc[ch], inter)
                f1 = jnp.exp(m_loc[ch] - m_row)
                w_int = jnp.exp(inter - m_row)
                num = f1 * sv[ch] + w_int * qc[h]
                den = f1 * s_sum[ch] + w_int * jnp.sum(q_f[ch] * n[h], -1, keepdims=True)
                hh.append(num / jnp.maximum(jnp.abs(den), jnp.exp(-m_row)))
                m_new = jnp.maximum(g_tot[ch] + m[h], m_w[ch])
                dec = jnp.exp(g_tot[ch] + m[h] - m_new)
                f2 = jnp.exp(m_w[ch] - m_new)
                c[h] = dec * c[h] + f2 * kv[ch]
                n[h] = dec * n[h] + f2 * n_loc[ch]
                m[h] = m_new
            hs.append(jnp.concatenate(hh, axis=-1))
        m_out = m_vec
        for h in heads:
            c_sc[b, h] = c[h]
            n_sc[b, h:h + 1, :] = n[h]
            m_out = jnp.where(lane == h, m[h], m_out)
        m_sc[b] = m_out

    hcat = jnp.concatenate(hs, axis=0) if len(hs) > 1 else hs[0]
    bo = _block_ones()
    mu_h = _head_sum(hcat, bo) * (1.0 / HD)
    hc = hcat - mu_h
    var = _head_sum(hc * hc, bo) * (1.0 / HD)
    y_ref[...] = (hc * lax.rsqrt(var + ML_NORM_EPS) * ng_ref[...] * _sigmoid(o)).reshape(bb, TT, G)

    @pl.when(j == pl.num_programs(1) - 1)
    def _():
        c_ref[...] = c_sc[...]
        n_ref[...] = n_sc[...]
        m_ref[...] = m_sc[...]


def _mlstm_call(u_ml, gates, conv0, c0, n0, m0, cw, cbias, gbias, ng):
    B, T, _ = u_ml.shape
    L = math.gcd(T, ML_CHUNK)
    nc = math.gcd(T // L, ML_CHUNKS_PER_STEP)
    bb = 1 if nc > 1 else math.gcd(B, ML_CHUNKS_PER_STEP)
    TT = nc * L
    full = lambda a: pl.BlockSpec(a.shape, lambda b, j: (0,) * a.ndim)
    st4 = pl.BlockSpec((bb, N_HEADS, HEAD_DIM, HEAD_DIM), lambda b, j: (b, 0, 0, 0))
    st3 = pl.BlockSpec((bb, N_HEADS, HEAD_DIM), lambda b, j: (b, 0, 0))
    stm = pl.BlockSpec((bb, 1, GATE_PAD), lambda b, j: (b, 0, 0))
    return pl.pallas_call(
        functools.partial(_mlstm_kernel, L=L, bb=bb, nc=nc),
        grid=(B // bb, T // TT),
        in_specs=[pl.BlockSpec((bb, TT, ML_MAIN), lambda b, j: (b, j, 0)),
                  pl.BlockSpec((bb, TT, GATE_PAD), lambda b, j: (b, j, 0)),
                  pl.BlockSpec((bb, 3, 2 * GROUP_W), lambda b, j: (b, 0, 0)),
                  st4, st3, stm, full(cw), full(cbias), full(gbias), full(ng)],
        out_specs=[pl.BlockSpec((bb, TT, GROUP_W), lambda b, j: (b, j, 0)), st4, st3, stm],
        out_shape=[jax.ShapeDtypeStruct((B, T, GROUP_W), F32),
                   jax.ShapeDtypeStruct((B, N_HEADS, HEAD_DIM, HEAD_DIM), F32),
                   jax.ShapeDtypeStruct((B, N_HEADS, HEAD_DIM), F32),
                   jax.ShapeDtypeStruct((B, 1, GATE_PAD), F32)],
        scratch_shapes=[pltpu.VMEM((bb, 8 + TT, 2 * GROUP_W), F32),
                        pltpu.VMEM((bb, N_HEADS, HEAD_DIM, HEAD_DIM), F32),
                        pltpu.VMEM((bb, N_HEADS, HEAD_DIM), F32),
                        pltpu.VMEM((bb, 1, GATE_PAD), F32)],
        compiler_params=_params(("arbitrary", "arbitrary")),
        name="mlstm",
    )(u_ml, gates, conv0, c0, n0, m0, cw, cbias, gbias, ng)


def _pool_kernel(u_ref, hist_ref, w_ref, scale_ref, y_ref, zbuf, *, pos0):
    j = pl.program_id(1)
    bb, tt, G = u_ref.shape
    gc = G // len(POOL_WINDOWS)

    @pl.when(j == 0)
    def _():
        zbuf[:, 0:1, :] = jnp.zeros((bb, 1, G), F32)
        zbuf[:, 1:16, :] = hist_ref[...]

    u = u_ref[...]
    zbuf[:, 16:16 + tt, :] = u
    lane = _iota((1, 1, G), 2)
    pos = pos0 + j * tt + _iota((1, tt, 1), 1)
    acc = jnp.zeros((bb, tt, G), F32)
    pooled = jnp.zeros((bb, tt, G), F32)
    for t in range(max(POOL_WINDOWS)):
        acc = acc + zbuf[:, 16 - t:16 - t + tt, :]
        if (t + 1) in POOL_WINDOWS:
            gi = POOL_WINDOWS.index(t + 1)
            cnt = jnp.minimum(pos + 1, t + 1).astype(F32)
            pooled = jnp.where(lane // gc == gi, acc / cnt, pooled)
    tail = zbuf[:, tt:tt + 16, :]
    zbuf[:, 0:16, :] = tail
    pooled = (pooled - u).reshape(bb * tt, G)
    y_ref[...] = (_bdot(pooled, w_ref[...]) * scale_ref[...]).reshape(bb, tt, G)


def _pool_call(u_pool, hist, wbd, scale, pos0):
    B, T, G = u_pool.shape
    bb, tt = _row_blocks(B, T)
    return pl.pallas_call(
        functools.partial(_pool_kernel, pos0=pos0),
        grid=(B // bb, T // tt),
        in_specs=[pl.BlockSpec((bb, tt, G), lambda b, j: (b, j, 0)),
                  pl.BlockSpec((bb, POOL_HIST, G), lambda b, j: (b, 0, 0)),
                  pl.BlockSpec((G, G), lambda b, j: (0, 0)),
                  pl.BlockSpec((1, G), lambda b, j: (0, 0))],
        out_specs=pl.BlockSpec((bb, tt, G), lambda b, j: (b, j, 0)),
        out_shape=jax.ShapeDtypeStruct((B, T, G), F32),
        scratch_shapes=[pltpu.VMEM((bb, 16 + tt, G), F32)],
        compiler_params=_params(("arbitrary", "arbitrary")),
        name="pool",
    )(u_pool, hist, wbd, scale)


def _slope_of(head):
    return jnp.where(head == 0, 2.0 ** -2, jnp.where(head == 1, 2.0 ** -4, jnp.where(head == 2, 2.0 ** -6, 2.0 ** -8)))


def _topk_mask(gate, valid, nblk):
    blk = _iota((1, nblk), 1)
    cnt = jnp.zeros(gate.shape, F32)
    for n in range(nblk):
        gn = gate[:, n:n + 1]
        ahead = (gn > gate) | ((gn == gate) & (n < blk))
        cnt = cnt + jnp.where(ahead, 1.0, 0.0)
    return valid & (cnt < MOBA_TOPK)


def _moba_p_kernel(q_ref, k_ref, v_ref, o_ref, mean_sc, kb_sc, vt_sc, sel_sc, *, nblk):
    i = pl.program_id(1)
    BLK, G, HD, H = MOBA_BLOCK, GROUP_W, HEAD_DIM, N_HEADS
    W = H * BLK

    @pl.when(i == 0)
    def _():
        for n in range(nblk):
            kblk = k_ref[0, n * BLK:(n + 1) * BLK, :]
            mean_sc[n:n + 1, :] = jnp.mean(kblk, axis=0, keepdims=True)
            kb_sc[n] = kblk.astype(BF16)
            vt_sc[n] = v_ref[0, n * BLK:(n + 1) * BLK, :].T.astype(BF16)

    q = q_ref[0]
    lane_head = _iota((1, G), 1) // HD
    qbd = jnp.concatenate([jnp.where(lane_head == h, q, 0.0) for h in range(H)], axis=0)
    qb = (qbd * (HD ** -0.5)).astype(BF16)
    col_head = _iota((1, W), 1) // BLK
    col_t = _iota((1, W), 1) % BLK
    slope = _slope_of(col_head).astype(F32)

    blk_row = _iota((nblk, 1), 0)
    valid = blk_row < i
    gate = jnp.where(valid, _dot3(mean_sc[...], qbd, NT), -jnp.inf)
    cnt = jnp.zeros((nblk, W), F32)
    for n in range(nblk):
        gn = gate[n:n + 1, :]
        ahead = (gn > gate) | ((gn == gate) & (n < blk_row))
        cnt = cnt + jnp.where(ahead, 1.0, 0.0)
    sel = jnp.where(valid & (cnt < MOBA_TOPK), 1.0, 0.0)
    for n in range(nblk):
        sel_sc[n] = sel[n:n + 1, :]

    s_idx = _iota((BLK, W), 0)
    base = slope * (col_t - s_idx).astype(F32)

    def pv(jb, pb):
        vt = vt_sc[jb]
        return [_dot(vt[h * HD:(h + 1) * HD, :], pb[:, h * BLK:(h + 1) * BLK]) for h in range(H)]

    s = _dot(kb_sc[i], qb, NT) - base
    s = jnp.where(s_idx <= col_t, s, -jnp.inf)
    m0 = jnp.max(s, 0, keepdims=True)
    p = jnp.exp(s - m0)
    l0 = jnp.sum(p, 0, keepdims=True)
    acc0 = pv(i, p.astype(BF16))

    def body(jb, carry):
        m, l, acc = carry
        off = ((i - jb) * BLK).astype(F32)
        row_off = jnp.where(sel_sc[jb] > 0.0, slope * off, jnp.inf)
        s = _dot(kb_sc[jb], qb, NT) - base - row_off
        m2 = jnp.maximum(m, jnp.max(s, 0, keepdims=True))
        alpha = jnp.exp(m - m2)
        p = jnp.exp(s - m2)
        new = pv(jb, p.astype(BF16))
        acc = [alpha[:, h * BLK:(h + 1) * BLK] * acc[h] + new[h] for h in range(H)]
        return m2, alpha * l + jnp.sum(p, 0, keepdims=True), acc

    m, l, acc = lax.fori_loop(0, i, body, (m0, l0, acc0))
    out_t = jnp.concatenate([acc[h] / l[:, h * BLK:(h + 1) * BLK] for h in range(H)], axis=0)
    o_ref[0] = out_t.T


def _moba_prompt_call(q, k, v):
    B, T, G = q.shape
    assert T % MOBA_BLOCK == 0
    nblk = T // MOBA_BLOCK
    return pl.pallas_call(
        functools.partial(_moba_p_kernel, nblk=nblk),
        grid=(B, nblk),
        in_specs=[pl.BlockSpec((1, MOBA_BLOCK, G), lambda b, i: (b, i, 0)),
                  pl.BlockSpec((1, T, G), lambda b, i: (b, 0, 0)),
                  pl.BlockSpec((1, T, G), lambda b, i: (b, 0, 0))],
        out_specs=pl.BlockSpec((1, MOBA_BLOCK, G), lambda b, i: (b, i, 0)),
        out_shape=jax.ShapeDtypeStruct((B, T, G), F32),
        scratch_shapes=[pltpu.VMEM((nblk, G), F32),
                        pltpu.VMEM((nblk, MOBA_BLOCK, G), BF16),
                        pltpu.VMEM((nblk, G, MOBA_BLOCK), BF16),
                        pltpu.VMEM((nblk, 1, N_HEADS * MOBA_BLOCK), F32)],
        compiler_params=_params(("arbitrary", "arbitrary")),
        name="moba_prompt",
    )(q, k, v)


def _moba_s_kernel(pt_ref, q_ref, kn_ref, vn_ref, *refs, n_pages, page, past_len):
    kp = refs[:n_pages]
    vp = refs[n_pages:2 * n_pages]
    o_ref = refs[2 * n_pages]
    G, HD = GROUP_W, HEAD_DIM
    q = q_ref[0]
    ts = q.shape[0]
    R = N_HEADS * ts
    ppb = MOBA_BLOCK // page
    nb = n_pages // ppb
    lane_head = _iota((1, G), 1) // HD
    qbd = jnp.concatenate([jnp.where(lane_head == h, q, 0.0) for h in range(N_HEADS)], axis=0)
    row_head = _iota((R, 1), 0) // ts
    row_t = _iota((R, 1), 0) % ts
    slope = _slope_of(row_head)
    scale = HD ** -0.5
    qb = qbd.astype(BF16)

    raw = []
    sums = []
    for p in range(n_pages):
        kpg = kp[p][...]
        raw.append(_dot(qb, kpg.astype(BF16), NT))
        sums.append(jnp.sum(kpg, axis=0, keepdims=True))
    means = jnp.concatenate(
        [sum(sums[n * ppb:(n + 1) * ppb]) * (1.0 / MOBA_BLOCK) for n in range(nb)], axis=0)
    gate = _dot3(qbd, means, NT)
    all_valid = _iota((1, nb), 1) >= 0
    sel = jnp.where(_topk_mask(gate, all_valid, nb), 1.0, 0.0)

    q_pos = (past_len + row_t).astype(F32)
    off = _iota((1, page), 1).astype(F32)
    scores = []
    for p in range(n_pages):
        n = p // ppb
        dist = q_pos - (p * page + off)
        s = raw[p] * scale - slope * dist
        scores.append(jnp.where(sel[:, n:n + 1] > 0.0, s, -jnp.inf))
    kn = kn_ref[0]
    vn = vn_ref[0]
    t_new = _iota((1, ts), 1)
    s_own = _dot(qb, kn.astype(BF16), NT) * scale - slope * (row_t - t_new).astype(F32)
    s_own = jnp.where(t_new <= row_t, s_own, -jnp.inf)

    m = jnp.max(s_own, -1, keepdims=True)
    for s in scores:
        m = jnp.maximum(m, jnp.max(s, -1, keepdims=True))
    p_own = jnp.exp(s_own - m)
    l = jnp.sum(p_own, -1, keepdims=True)
    acc = _dot(p_own.astype(BF16), vn.astype(BF16))
    for p in range(n_pages):
        pr = jnp.exp(scores[p] - m)
        l = l + jnp.sum(pr, -1, keepdims=True)
        acc = acc + _dot(pr.astype(BF16), vp[p][...].astype(BF16))
    out = acc / l
    y = jnp.zeros((ts, G), F32)
    for h in range(N_HEADS):
        y = jnp.where(lane_head == h, out[h * ts:(h + 1) * ts, :], y)
    o_ref[0] = y


def _moba_sample_call(q, k, v, cache_k, cache_v, page_table, layer):
    DB, TS, G = q.shape
    Ld, n_phys, page, H, d = cache_k.shape
    n_pages = page_table.shape[1]
    past_len = n_pages * page
    assert past_len % MOBA_BLOCK == 0 and MOBA_BLOCK % page == 0 and past_len // MOBA_BLOCK >= MOBA_TOPK
    ck = cache_k.reshape(Ld, n_phys, page, H * d)
    cv = cache_v.reshape(Ld, n_phys, page, H * d)
    new = pl.BlockSpec((1, TS, G), lambda b, pt: (b, 0, 0))
    pg = [pl.BlockSpec((None, None, page, G), functools.partial(lambda b, pt, p: (layer, pt[b, p], 0, 0), p=p))
          for p in range(n_pages)]
    return pl.pallas_call(
        functools.partial(_moba_s_kernel, n_pages=n_pages, page=page, past_len=past_len),
        grid_spec=pltpu.PrefetchScalarGridSpec(
            num_scalar_prefetch=1, grid=(DB,),
            in_specs=[new, new, new] + pg + pg,
            out_specs=new),
        out_shape=jax.ShapeDtypeStruct((DB, TS, G), F32),
        compiler_params=_params(("arbitrary",)),
        name="moba_sample",
    )(page_table, q, k, v, *([ck] * n_pages), *([cv] * n_pages))


def _out_kernel(yrw_ref, yml_ref, ypl_ref, yat_ref, x_ref, g1_ref, sc2_ref, sh2_ref, g2_ref, n2g_ref, fng_ref,
                wout_ref, wup_ref, wdn_ref, o_ref, x1_sc, h2_sc, acc_sc, *, final):
    f = pl.program_id(2)
    bb, tt, D = x_ref.shape
    M = bb * tt

    @pl.when(f == 0)
    def _():
        ycat = jnp.concatenate([r[...].reshape(M, GROUP_W) for r in (yrw_ref, yml_ref, ypl_ref, yat_ref)], axis=-1)
        y = _dot(ycat.astype(BF16), wout_ref[...]).reshape(bb, tt, D)
        x1 = x_ref[...] + g1_ref[...] * y
        x1_sc[...] = x1
        hn = x1 * lax.rsqrt(jnp.mean(x1 * x1, -1, keepdims=True) + NORM_EPS) * n2g_ref[...]
        h2_sc[...] = (hn * (1.0 + sc2_ref[...]) + sh2_ref[...]).reshape(M, D).astype(BF16)
        acc_sc[...] = jnp.zeros((M, D), F32)

    a = _dot(h2_sc[...], wup_ref[...])
    a = jnp.square(jnp.maximum(a, 0.0)).astype(BF16)
    acc_sc[...] += _dot(a, wdn_ref[...])

    @pl.when(f == pl.num_programs(2) - 1)
    def _():
        x2 = x1_sc[...] + g2_ref[...] * acc_sc[...].reshape(bb, tt, D)
        if final:
            x2 = x2 * lax.rsqrt(jnp.mean(x2 * x2, -1, keepdims=True) + NORM_EPS) * fng_ref[...]
        o_ref[...] = x2


def _out_call(ys, x, g1, sc2, sh2, g2, n2g, fng, wout, wup, wdn, final):
    B, T, D = x.shape
    bb, tt = _row_blocks(B, T)
    FF = wup.shape[1]
    row = lambda n: pl.BlockSpec((bb, tt, n), lambda b, j, f: (b, j, 0))
    mod = pl.BlockSpec((bb, 1, D), lambda b, j, f: (b, 0, 0))
    vec = pl.BlockSpec((1, D), lambda b, j, f: (0, 0))
    return pl.pallas_call(
        functools.partial(_out_kernel, final=final),
        grid=(B // bb, T // tt, FF // FF_TILE),
        in_specs=[row(GROUP_W)] * 4 + [row(D), mod, mod, mod, mod, vec, vec,
                  pl.BlockSpec((D, D), lambda b, j, f: (0, 0)),
                  pl.BlockSpec((D, FF_TILE), lambda b, j, f: (0, f)),
                  pl.BlockSpec((FF_TILE, D), lambda b, j, f: (f, 0))],
        out_specs=row(D),
        out_shape=jax.ShapeDtypeStruct((B, T, D), F32),
        scratch_shapes=[pltpu.VMEM((bb, tt, D), F32),
                        pltpu.VMEM((bb * tt, D), BF16),
                        pltpu.VMEM((bb * tt, D), F32)],
        compiler_params=_params(("arbitrary", "arbitrary", "arbitrary")),
        name="out_mlp",
    )(*ys, x, g1, sc2, sh2, g2, n2g, fng, wout, wup, wdn)


_RW_PERM = np.concatenate([np.arange(0, 256), np.arange(320, 576), np.arange(576, 832),
                           np.arange(256, 320), np.arange(832, 896), np.arange(896, 1024)])
_RW_INV = np.argsort(_RW_PERM)


def _layer(x, mods, st, pos0, attend, lw, final):
    sh1, sc1, g1, sh2, sc2, g2 = mods
    shift0, wkv0, conv0, c0, n0, m0, pool0 = st
    B, T, D = x.shape
    u_rw, u_ml, gates, u_pool, q, k, v = _in_call(x, sc1, sh1, lw['norm1_g'], lw['wpack'])

    y_rw, wkv1 = _rwkv_call(u_rw, shift0[:, _RW_PERM][:, None, :], wkv0, lw['rw_mu'], lw['rw_vec'],
                            lw['rwkv_w_up'], lw['rwkv_a_up'], lw['rwkv_g_up'])
    shift1 = u_rw[:, -1, :][:, _RW_INV]

    m0p = jnp.pad(m0, ((0, 0), (0, GATE_PAD - N_HEADS)))[:, None, :]
    y_ml, c1, n1, m1p = _mlstm_call(u_ml, gates, conv0, c0, n0, m0p, lw['mlstm_conv_w'], lw['ml_cbias'],
                                    lw['ml_gbias'], lw['ml_ng'])
    m1 = m1p[:, 0, :N_HEADS]
    zc = u_ml[:, :, :2 * GROUP_W] if T >= 3 else jnp.concatenate([conv0, u_ml[:, :, :2 * GROUP_W]], 1)
    conv1 = zc[:, -3:]

    y_pool = _pool_call(u_pool, pool0, lw['pool_wbd'], lw['pool_scale'], pos0)
    zp = u_pool if T >= POOL_HIST else jnp.concatenate([pool0, u_pool], 1)
    pool1 = zp[:, -POOL_HIST:]

    y_at = attend(q, k, v)

    x_new = _out_call((y_rw, y_ml, y_pool, y_at), x, g1, sc2, sh2, g2, lw['norm2_g'], lw['final_g'],
                      lw['w_out'], lw['mlp_up'], lw['mlp_down'], final)
    kv_shape = (B, T, N_HEADS, HEAD_DIM)
    return x_new, (shift1, wkv1, conv1, c1, n1, m1, pool1, k.reshape(kv_shape), v.reshape(kv_shape))


def kernel(x_prompt, x_sample, c_prompt, c_sample, state_rwkv_shift, state_rwkv_wkv, state_mlstm_conv, state_mlstm_c, state_mlstm_n, state_mlstm_m, state_pool, cache_k, cache_v, page_table, ada_w, ada_b, norm1_g, norm2_g, w_in, w_out, rwkv_mu, rwkv_w0, rwkv_w_up, rwkv_a0, rwkv_a_up, rwkv_g_up, rwkv_k_k, rwkv_k_a, rwkv_r_k, rwkv_ln_g, rwkv_ln_b, mlstm_conv_w, mlstm_conv_b, mlstm_i_b, mlstm_f_b, mlstm_norm_g, pool_w, pool_scale, mlp_up, mlp_down, final_norm_g):
    B, T, D = x_prompt.shape
    DB = x_sample.shape[0]
    depth = ada_w.shape[0]
    G = GROUP_W
    assert D == D_MODEL and w_in.shape[-1] == N_IN

    mod = _ada_call(jnp.concatenate([c_prompt, c_sample], 0), ada_w, ada_b)

    st_p0 = (jnp.zeros((B, RW_COLS), F32), jnp.zeros((B, N_HEADS, HEAD_DIM, HEAD_DIM), F32),
             jnp.zeros((B, 3, 2 * G), F32), jnp.zeros((B, N_HEADS, HEAD_DIM, HEAD_DIM), F32),
             jnp.zeros((B, N_HEADS, HEAD_DIM), F32), jnp.zeros((B, N_HEADS), F32),
             jnp.zeros((B, POOL_HIST, G), F32))
    past_len = page_table.shape[1] * cache_k.shape[2]

    xp, xs = x_prompt, x_sample
    new_p, new_s = [], []
    for l in range(depth):
        wl = w_in[l]
        wpack = jnp.concatenate([
            wl[:, :RW_COLS][:, _RW_PERM],
            wl[:, RW_COLS:RW_COLS + ML_MAIN],
            jnp.pad(wl[:, RW_COLS + ML_MAIN:RW_COLS + ML_MAIN + ML_GATES], ((0, 0), (0, GATE_PAD - ML_GATES))),
            wl[:, RW_COLS + ML_MAIN + ML_GATES:],
        ], axis=1).astype(BF16)
        pool_wbd = jnp.zeros((G, G), F32)
        gcw = G // len(POOL_WINDOWS)
        for gi in range(len(POOL_WINDOWS)):
            pool_wbd = pool_wbd.at[gi * gcw:(gi + 1) * gcw, gi * gcw:(gi + 1) * gcw].set(pool_w[l, gi])
        row = lambda a: a.reshape(1, -1)
        lw = {
            'norm1_g': row(norm1_g[l]), 'norm2_g': row(norm2_g[l]), 'final_g': row(final_norm_g),
            'wpack': wpack, 'w_out': w_out[l].astype(BF16),
            'mlp_up': mlp_up[l].astype(BF16), 'mlp_down': mlp_down[l].astype(BF16),
            'rw_mu': row(rwkv_mu[l][_RW_PERM]),
            'rw_vec': jnp.stack([rwkv_w0[l], rwkv_a0[l], rwkv_k_k[l], rwkv_k_a[l], rwkv_ln_g[l], rwkv_ln_b[l],
                                 rwkv_r_k[l].reshape(-1), jnp.zeros((G,), F32)]),
            'rwkv_w_up': rwkv_w_up[l], 'rwkv_a_up': rwkv_a_up[l], 'rwkv_g_up': rwkv_g_up[l],
            'mlstm_conv_w': mlstm_conv_w[l], 'ml_cbias': row(mlstm_conv_b[l]),
            'ml_gbias': row(jnp.pad(jnp.concatenate([mlstm_i_b[l], mlstm_f_b[l]]), (0, GATE_PAD - ML_GATES))),
            'ml_ng': row(mlstm_norm_g[l]),
            'pool_wbd': pool_wbd, 'pool_scale': row(pool_scale[l]),
        }
        mods = [mod[l, :, i * D:(i + 1) * D][:, None, :] for i in range(6)]
        mods_p = [m[:B] for m in mods]
        mods_s = [m[B:] for m in mods]
        final = l == depth - 1

        xp, st_p = _layer(xp, mods_p, st_p0, 0, _moba_prompt_call, lw, final)
        st_s_in = (state_rwkv_shift[l], state_rwkv_wkv[l], state_mlstm_conv[l], state_mlstm_c[l],
                   state_mlstm_n[l], state_mlstm_m[l], state_pool[l])
        attend_s = functools.partial(_moba_sample_call, cache_k=cache_k, cache_v=cache_v,
                                     page_table=page_table, layer=l)
        xs, st_s = _layer(xs, mods_s, st_s_in, past_len, attend_s, lw, final)
        new_p.append(st_p)
        new_s.append(st_s)

    stack = lambda lst: tuple(jnp.stack([st[i] for st in lst]) for i in range(9))
    return (xp, xs) + stack(new_p) + stack(new_s)
```

```python
import functools
import math

import jax
import jax.numpy as jnp
import numpy as np
from jax import lax
from jax.experimental import pallas as pl
from jax.experimental.pallas import tpu as pltpu

F32 = jnp.float32
BF16 = jnp.bfloat16

D_MODEL = 1024
GROUP_W = 256
HEAD_DIM = 64
N_HEADS = 4
RW_COLS = 1024
ML_MAIN = 1024
ML_GATES = 8
GATE_PAD = 128
N_IN = 3080
MOBA_BLOCK = 256
MOBA_TOPK = 3
ML_CHUNK = 64
RW_CHUNK = 64
RW_CHUNKS_PER_STEP = 4
ML_CHUNKS_PER_STEP = 4
MOBA_SAMPLE_ROWS_PER_STEP = 4
POOL_WINDOWS = (2, 4, 8, 16)
POOL_HIST = 15
NORM_EPS = 1e-6
RW_GN_EPS = 64e-5
ML_NORM_EPS = 1e-6
ROW_TILE = 512
FF_TILE = 1024
VMEM_LIMIT = 48 * 1024 * 1024

NN = (((1,), (0,)), ((), ()))
NT = (((1,), (1,)), ((), ()))
TN = (((0,), (0,)), ((), ()))


def _dot(a, b, dn=NN):
    return lax.dot_general(a, b, dn, preferred_element_type=F32)


def _bdot(a, b, dn=NN):
    return _dot(a.astype(BF16), b.astype(BF16), dn)


def _split2(x):
    hi = x.astype(BF16)
    lo = (x - hi.astype(F32)).astype(BF16)
    return hi, lo


def _split3(x):
    hi = x.astype(BF16)
    r = x - hi.astype(F32)
    mid = r.astype(BF16)
    lo = (r - mid.astype(F32)).astype(BF16)
    return hi, mid, lo


def _dot3(a, b, dn=NN):
    ah, al = _split2(a)
    bh, bl = _split2(b)
    return _dot(ah, bh, dn) + (_dot(ah, bl, dn) + _dot(al, bh, dn))


def _dot3s(a_s, b_s, dn=NN):
    return _dot(a_s[0], b_s[0], dn) + (_dot(a_s[0], b_s[1], dn) + _dot(a_s[1], b_s[0], dn))


def _dot_exact_rhs(a, b01, dn=NN):
    h, m, l = _split3(a)
    b = b01.astype(BF16)
    return _dot(h, b, dn) + (_dot(m, b, dn) + _dot(l, b, dn))


def _dot_exact_lhs(a01, b, dn=NN):
    h, m, l = _split3(b)
    a = a01.astype(BF16)
    return _dot(a, h, dn) + (_dot(a, m, dn) + _dot(a, l, dn))


def _iota(shape, dim):
    return lax.broadcasted_iota(jnp.int32, shape, dim)


def _sigmoid(x):
    return jax.nn.sigmoid(x)


def _softplus(x):
    return jnp.maximum(x, 0.0) + jnp.log(1.0 + jnp.exp(-jnp.abs(x)))


def _block_ones():
    return (_iota((GROUP_W, GROUP_W), 0) // HEAD_DIM == _iota((GROUP_W, GROUP_W), 1) // HEAD_DIM).astype(F32)


def _head_sum(x, bo):
    return _dot_exact_rhs(x, bo)


def _head_sum1(x, bo):
    return _bdot(x, bo)


def _params(sem):
    return pltpu.CompilerParams(dimension_semantics=sem, vmem_limit_bytes=VMEM_LIMIT)


def _row_blocks(B, T):
    if T >= ROW_TILE:
        assert T % ROW_TILE == 0
        return 1, ROW_TILE
    bb = max(1, min(B, ROW_TILE // T))
    while B % bb:
        bb -= 1
    return bb, T


def _ada_kernel(c_ref, w_ref, b_ref, o_ref):
    c = c_ref[...]
    o_ref[...] = _bdot(c * _sigmoid(c), w_ref[...]) + b_ref[...]


def _ada_call(c_all, ada_w, ada_b):
    Ld, D, N6 = ada_w.shape
    NB = c_all.shape[0]
    tn = 1024
    return pl.pallas_call(
        _ada_kernel,
        grid=(Ld, N6 // tn),
        in_specs=[pl.BlockSpec((NB, D), lambda l, j: (0, 0)),
                  pl.BlockSpec((None, D, tn), lambda l, j: (l, 0, j)),
                  pl.BlockSpec((None, 1, tn), lambda l, j: (l, 0, j))],
        out_specs=pl.BlockSpec((None, NB, tn), lambda l, j: (l, 0, j)),
        out_shape=jax.ShapeDtypeStruct((Ld, NB, N6), F32),
        compiler_params=_params(("arbitrary", "arbitrary")),
        name="ada_mod",
    )(c_all, ada_w, ada_b.reshape(Ld, 1, N6))


IN_WIDTHS = (RW_COLS, ML_MAIN, GATE_PAD, GROUP_W, GROUP_W, GROUP_W, GROUP_W)


def _in_kernel(x_ref, sc_ref, sh_ref, g_ref, w_ref, wt_ref, *out_refs, n_rows_out):
    x = x_ref[...]
    bb, tt, D = x.shape
    y = x * lax.rsqrt(jnp.mean(x * x, -1, keepdims=True) + NORM_EPS) * g_ref[...]
    h = y * (1.0 + sc_ref[...]) + sh_ref[...]
    hb = h.reshape(bb * tt, D).astype(BF16)
    off = 0
    for ref in out_refs[:n_rows_out]:
        n = ref.shape[-1]
        ref[...] = _dot(hb, w_ref[:, off:off + n]).reshape(bb, tt, n)
        off += n
    for i, ref in enumerate(out_refs[n_rows_out:]):
        ref[0] = _dot(wt_ref[i * GROUP_W:(i + 1) * GROUP_W, :], hb, NT)


def _in_call(x, sc, sh, g, wpack, wkv_t, transposed_kv):
    B, T, D = x.shape
    bb, tt = _row_blocks(B, T)
    row = lambda n: pl.BlockSpec((bb, tt, n), lambda b, j: (b, j, 0))
    mod = pl.BlockSpec((bb, 1, D), lambda b, j: (b, 0, 0))
    widths = IN_WIDTHS[:-1] if transposed_kv else IN_WIDTHS
    out_specs = [row(n) for n in widths]
    out_shape = [jax.ShapeDtypeStruct((B, T, n), F32) for n in widths]
    if transposed_kv:
        assert bb == 1
        out_specs += [pl.BlockSpec((1, GROUP_W, tt), lambda b, j: (b, 0, j))] * 2
        out_shape += [jax.ShapeDtypeStruct((B, GROUP_W, T), F32)] * 2
    return pl.pallas_call(
        functools.partial(_in_kernel, n_rows_out=len(widths)),
        grid=(B // bb, T // tt),
        in_specs=[row(D), mod, mod,
                  pl.BlockSpec((1, D), lambda b, j: (0, 0)),
                  pl.BlockSpec(wpack.shape, lambda b, j: (0, 0)),
                  pl.BlockSpec(wkv_t.shape, lambda b, j: (0, 0))],
        out_specs=out_specs,
        out_shape=out_shape,
        compiler_params=_params(("arbitrary", "arbitrary")),
        name="in_proj",
    )(x, sc, sh, g, wpack, wkv_t)


def _rwkv_kernel(u_ref, s0_ref, wkv0_ref, mu_ref, vec_ref, wup_ref, aup_ref, gup_ref,
                 y_ref, wkv_ref, zbuf, s_sc, *, L, bb, nc):
    j = pl.program_id(1)
    G, HD = GROUP_W, HEAD_DIM
    TT = nc * L
    R = bb * TT

    @pl.when(j == 0)
    def _():
        zbuf[:, 7:8, :] = s0_ref[...]
        s_sc[...] = wkv0_ref[...]

    u3 = u_ref[...]
    zbuf[:, 8:8 + TT, :] = u3
    prev = zbuf[:, 7:7 + TT, :]
    xs = (u3 + (prev - u3) * mu_ref[...]).reshape(R, RW_COLS)
    zbuf[:, 7:8, :] = u3[:, TT - 1:TT, :]

    r = xs[:, 0:G]
    k = xs[:, G:2 * G]
    v = xs[:, 2 * G:3 * G]
    wc = xs[:, 3 * G:3 * G + 64]
    ac = xs[:, 3 * G + 64:3 * G + 128]
    gc = xs[:, 3 * G + 128:4 * G]
    w0, a0, k_k, k_a = vec_ref[0:1, :], vec_ref[1:2, :], vec_ref[2:3, :], vec_ref[3:4, :]
    ln_g, ln_b, r_k = vec_ref[4:5, :], vec_ref[5:6, :], vec_ref[6:7, :]

    w_log = -_softplus(-(w0 + _bdot(jnp.tanh(wc), wup_ref[...]))) - 0.5
    lw = -jnp.exp(w_log)
    a = _sigmoid(a0 + _bdot(ac, aup_ref[...]))
    g = _bdot(_sigmoid(gc), gup_ref[...])
    bo = _block_ones()
    kk = k * k_k
    kk = kk / jnp.maximum(jnp.sqrt(_head_sum(kk * kk, bo)), 1e-12)
    k2 = k * (1.0 + (a - 1.0) * k_a)
    beta = kk * a

    rr = _iota((R, R), 0)
    rc = _iota((R, R), 1)
    same = (rr // L) == (rc // L)
    c = _dot_exact_lhs((same & (rr >= rc)).astype(F32), lw)
    cl = _dot_exact_lhs(same.astype(F32), lw)
    e_inv = jnp.exp(-c)
    e_tail = jnp.exp(cl - c)
    a_t = -kk * jnp.exp(c - lw)
    b_t = beta * e_inv
    k_t = k2 * e_inv
    r_t = r * jnp.exp(c)
    b_l = beta * e_tail
    k_l = k2 * e_tail
    e_cl = jnp.exp(cl)

    row = _iota((L, L), 0)
    col = _iota((L, L), 1)
    incl = row >= col
    strict = row > col
    eye = (row == col).astype(F32)
    n_double = max(1, int(math.ceil(math.log2(L)))) - 1

    chains = [(ci, h) for ci in range(bb * nc) for h in range(N_HEADS)]

    def cut(x, ch):
        ci, h = ch
        return x[ci * L:(ci + 1) * L, h * HD:(h + 1) * HD]

    a_s = {ch: _split2(cut(a_t, ch)) for ch in chains}
    b_s = {ch: _split2(cut(b_t, ch)) for ch in chains}
    r_b = {ch: cut(r_t, ch).astype(BF16) for ch in chains}
    k_b = {ch: cut(k_t, ch).astype(BF16) for ch in chains}
    v_b = {ch: cut(v, ch).astype(BF16) for ch in chains}
    n_ab = {ch: jnp.where(strict, _dot3s(a_s[ch], b_s[ch], NT), 0.0) for ch in chains}
    n_ak = {ch: jnp.where(strict, _dot(a_s[ch][0], k_b[ch], NT), 0.0).astype(BF16) for ch in chains}
    n_rb = {ch: jnp.where(incl, _dot(r_b[ch], b_s[ch][0], NT), 0.0).astype(BF16) for ch in chains}
    n_rk = {ch: jnp.where(incl, _dot(r_b[ch], k_b[ch], NT), 0.0).astype(BF16) for ch in chains}
    p = {ch: eye + n_ab[ch] for ch in chains}
    m_s = {ch: _split2(n_ab[ch]) for ch in chains}
    for _ in range(n_double):
        m_s = {ch: _split2(_dot3s(m_s[ch], m_s[ch])) for ch in chains}
        p = {ch: p[ch] + _dot3s(m_s[ch], _split2(p[ch])) for ch in chains}
    p_b = {ch: p[ch].astype(BF16) for ch in chains}
    akv = {ch: _dot(n_ak[ch], v_b[ch]).astype(BF16) for ch in chains}
    y_v = {ch: _dot(n_rk[ch], v_b[ch]) for ch in chains}
    w_b = {ch: _dot(p_b[ch], a_s[ch][0]).astype(BF16) for ch in chains}
    u_t = {ch: _dot(p_b[ch], akv[ch]) for ch in chains}
    bl_b = {ch: cut(b_l, ch).astype(BF16) for ch in chains}
    kl_b = {ch: cut(k_l, ch).astype(BF16) for ch in chains}

    heads = range(N_HEADS)
    bh = [(b, h) for b in range(bb) for h in heads]
    s = {k_: s_sc[k_[0], k_[1]] for k_ in bh}
    y_chunk = {}
    for cc in range(nc):
        ch_of = {(b, h): (b * nc + cc, h) for b, h in bh}
        s_b = {k_: s[k_].astype(BF16) for k_ in bh}
        e_b = {k_: (_dot(w_b[ch_of[k_]], s_b[k_], NT) + u_t[ch_of[k_]]).astype(BF16) for k_ in bh}
        s = {k_: s[k_] * cut(e_cl, ch_of[k_])[0:1, :]
             + _dot(e_b[k_], bl_b[ch_of[k_]], TN) + _dot(v_b[ch_of[k_]], kl_b[ch_of[k_]], TN) for k_ in bh}
        yh = {k_: _dot(r_b[ch_of[k_]], s_b[k_], NT) + _dot(n_rb[ch_of[k_]], e_b[k_]) + y_v[ch_of[k_]] for k_ in bh}
        for b in range(bb):
            y_chunk[b * nc + cc] = jnp.concatenate([yh[b, h] for h in heads], axis=-1)
    for k_ in bh:
        s_sc[k_[0], k_[1]] = s[k_]

    ys = [y_chunk[ci] for ci in range(bb * nc)]
    y = jnp.concatenate(ys, axis=0) if len(ys) > 1 else ys[0]
    mu_y = _head_sum1(y, bo) * (1.0 / HD)
    yc = y - mu_y
    var = _head_sum1(yc * yc, bo) * (1.0 / HD)
    yn = yc * lax.rsqrt(var + RW_GN_EPS) * ln_g + ln_b
    bonus = _head_sum1(r * k2 * r_k, bo) * v
    y_ref[...] = ((yn + bonus) * g).reshape(bb, TT, G)

    @pl.when(j == pl.num_programs(1) - 1)
    def _():
        wkv_ref[...] = s_sc[...]


def _rwkv_call(u_rw, shift0, wkv0, mu, vec, wup, aup, gup):
    B, T, _ = u_rw.shape
    L = math.gcd(T, RW_CHUNK)
    bb = math.gcd(B, RW_CHUNKS_PER_STEP)
    nc = math.gcd(T // L, RW_CHUNKS_PER_STEP // bb)
    TT = nc * L
    full = lambda a: pl.BlockSpec(a.shape, lambda b, j: (0,) * a.ndim)
    return pl.pallas_call(
        functools.partial(_rwkv_kernel, L=L, bb=bb, nc=nc),
        grid=(B // bb, T // TT),
        in_specs=[pl.BlockSpec((bb, TT, RW_COLS), lambda b, j: (b, j, 0)),
                  pl.BlockSpec((bb, 1, RW_COLS), lambda b, j: (b, 0, 0)),
                  pl.BlockSpec((bb, N_HEADS, HEAD_DIM, HEAD_DIM), lambda b, j: (b, 0, 0, 0)),
                  full(mu), full(vec), full(wup), full(aup), full(gup)],
        out_specs=[pl.BlockSpec((bb, TT, GROUP_W), lambda b, j: (b, j, 0)),
                   pl.BlockSpec((bb, N_HEADS, HEAD_DIM, HEAD_DIM), lambda b, j: (b, 0, 0, 0))],
        out_shape=[jax.ShapeDtypeStruct((B, T, GROUP_W), F32),
                   jax.ShapeDtypeStruct((B, N_HEADS, HEAD_DIM, HEAD_DIM), F32)],
        scratch_shapes=[pltpu.VMEM((bb, 8 + TT, RW_COLS), F32),
                        pltpu.VMEM((bb, N_HEADS, HEAD_DIM, HEAD_DIM), F32)],
        compiler_params=_params(("arbitrary", "arbitrary")),
        name="rwkv7",
    )(u_rw, shift0, wkv0, mu, vec, wup, aup, gup)


def _mlstm_kernel(u_ref, gt_ref, cb_ref, c0_ref, n0_ref, m0_ref, cw_ref, cbias_ref, gbias_ref, ng_ref,
                  y_ref, c_ref, n_ref, m_ref, zbuf, c_sc, n_sc, m_sc, *, L, bb, nc):
    j = pl.program_id(1)
    G, HD = GROUP_W, HEAD_DIM
    TT = nc * L
    R = bb * TT

    @pl.when(j == 0)
    def _():
        zbuf[:, 5:8, :] = cb_ref[...]
        c_sc[...] = c0_ref[...]
        n_sc[...] = n0_ref[...]
        m_sc[...] = m0_ref[...]

    u3 = u_ref[...]
    zbuf[:, 8:8 + TT, :] = u3[:, :, 0:2 * G]
    conv = cbias_ref[...]
    for t in range(4):
        conv = conv + zbuf[:, 5 + t:5 + t + TT, :] * cw_ref[t:t + 1, :]
    tail = zbuf[:, 5 + TT:8 + TT, :]
    zbuf[:, 5:8, :] = tail
    conv = conv.reshape(R, 2 * G)
    u = u3.reshape(R, ML_MAIN)
    sq = conv * _sigmoid(conv)
    q = sq[:, 0:G]
    k = sq[:, G:2 * G] * (HD ** -0.5)
    v = u[:, 2 * G:3 * G]
    o = u[:, 3 * G:4 * G]

    gates = gt_ref[...].reshape(R, GATE_PAD) + gbias_ref[...]
    lane = _iota((1, GATE_PAD), 1)
    gl = jnp.where(lane < N_HEADS, gates, -_softplus(-gates))
    rr = _iota((R, R), 0)
    rc = _iota((R, R), 1)
    same = (rr // L) == (rc // L)
    bcol = _dot_exact_lhs((same & (rr >= rc)).astype(F32), gl)
    gtot = _dot_exact_lhs(same.astype(F32), gl)
    sel = jnp.concatenate([(same & (rr <= rc)).astype(F32), (rr == rc).astype(F32)], axis=1)
    brow = _dot_exact_rhs(gl, sel, TN)
    causal = _iota((L, L), 0) >= _iota((L, L), 1)

    heads = range(N_HEADS)
    chains = [(ci, h) for ci in range(bb * nc) for h in heads]

    def cut(x, ch):
        ci, h = ch
        return x[ci * L:(ci + 1) * L, h * HD:(h + 1) * HD]

    def rows_of(x, ch, lane0):
        ci, h = ch
        return x[ci * L:(ci + 1) * L, lane0 + h:lane0 + h + 1]

    bc = {ch: rows_of(bcol, ch, N_HEADS) for ch in chains}
    ic = {ch: rows_of(gl, ch, 0) for ch in chains}
    g_tot = {(ci, h): gtot[ci * L:ci * L + 1, N_HEADS + h:N_HEADS + h + 1] for ci, h in chains}
    br = {(ci, h): brow[N_HEADS + h:N_HEADS + h + 1, ci * L:(ci + 1) * L] for ci, h in chains}
    ir = {(ci, h): brow[h:h + 1, R + ci * L:R + (ci + 1) * L] for ci, h in chains}
    q_f = {ch: cut(q, ch) for ch in chains}
    k_f = {ch: cut(k, ch) for ch in chains}
    q_b = {ch: q_f[ch].astype(BF16) for ch in chains}
    k_b = {ch: k_f[ch].astype(BF16) for ch in chains}
    v_f = {ch: cut(v, ch) for ch in chains}
    v_b = {ch: v_f[ch].astype(BF16) for ch in chains}
    log_d = {ch: jnp.where(causal, bc[ch] - br[ch] + ir[ch], -jnp.inf) for ch in chains}
    m_loc = {ch: jnp.max(log_d[ch], -1, keepdims=True) for ch in chains}
    s0 = {ch: _dot(q_b[ch], k_b[ch], NT) * jnp.exp(log_d[ch] - m_loc[ch]) for ch in chains}
    s_sum = {ch: jnp.sum(s0[ch], -1, keepdims=True) for ch in chains}
    sv = {ch: _dot(s0[ch].astype(BF16), v_b[ch]) for ch in chains}
    logw = {ch: g_tot[ch] - bc[ch] + ic[ch] for ch in chains}
    m_w = {ch: jnp.max(logw[ch], 0, keepdims=True) for ch in chains}
    w_loc = {ch: jnp.exp(logw[ch] - m_w[ch]) for ch in chains}
    kv = {ch: _dot((v_f[ch] * w_loc[ch]).astype(BF16), k_b[ch], TN) for ch in chains}
    n_loc = {ch: jnp.sum(k_f[ch] * w_loc[ch], 0, keepdims=True) for ch in chains}

    bh = [(b, h) for b in range(bb) for h in heads]
    m_vec = {b: m_sc[b] for b in range(bb)}
    m = {(b, h): m_vec[b][:, h:h + 1] for b, h in bh}
    c = {k_: c_sc[k_[0], k_[1]] for k_ in bh}
    n = {(b, h): n_sc[b, h:h + 1, :] for b, h in bh}
    h_chunk = {}
    for cc in range(nc):
        ch_of = {(b, h): (b * nc + cc, h) for b, h in bh}
        qc = {k_: _dot(q_b[ch_of[k_]], c[k_].astype(BF16), NT) for k_ in bh}
        qn = {k_: jnp.sum(q_f[ch_of[k_]] * n[k_], -1, keepdims=True) for k_ in bh}
        inter = {k_: bc[ch_of[k_]] + m[k_] for k_ in bh}
        m_row = {k_: jnp.maximum(m_loc[ch_of[k_]], inter[k_]) for k_ in bh}
        m_new = {k_: jnp.maximum(g_tot[ch_of[k_]] + m[k_], m_w[ch_of[k_]]) for k_ in bh}
        dec = {k_: jnp.exp(g_tot[ch_of[k_]] + m[k_] - m_new[k_]) for k_ in bh}
        f2 = {k_: jnp.exp(m_w[ch_of[k_]] - m_new[k_]) for k_ in bh}
        c = {k_: dec[k_] * c[k_] + f2[k_] * kv[ch_of[k_]] for k_ in bh}
        n = {k_: dec[k_] * n[k_] + f2[k_] * n_loc[ch_of[k_]] for k_ in bh}
        m = m_new
        f1 = {k_: jnp.exp(m_loc[ch_of[k_]] - m_row[k_]) for k_ in bh}
        w_int = {k_: jnp.exp(inter[k_] - m_row[k_]) for k_ in bh}
        num = {k_: f1[k_] * sv[ch_of[k_]] + w_int[k_] * qc[k_] for k_ in bh}
        den = {k_: f1[k_] * s_sum[ch_of[k_]] + w_int[k_] * qn[k_] for k_ in bh}
        hh = {k_: num[k_] / jnp.maximum(jnp.abs(den[k_]), jnp.exp(-m_row[k_])) for k_ in bh}
        for b in range(bb):
            h_chunk[b * nc + cc] = jnp.concatenate([hh[b, h] for h in heads], axis=-1)
    for b in range(bb):
        m_out = m_vec[b]
        for h in heads:
            c_sc[b, h] = c[b, h]
            n_sc[b, h:h + 1, :] = n[b, h]
            m_out = jnp.where(lane == h, m[b, h], m_out)
        m_sc[b] = m_out

    hs = [h_chunk[ci] for ci in range(bb * nc)]
    hcat = jnp.concatenate(hs, axis=0) if len(hs) > 1 else hs[0]
    bo = _block_ones()
    mu_h = _head_sum1(hcat, bo) * (1.0 / HD)
    hc = hcat - mu_h
    var = _head_sum1(hc * hc, bo) * (1.0 / HD)
    y_ref[...] = (hc * lax.rsqrt(var + ML_NORM_EPS) * ng_ref[...] * _sigmoid(o)).reshape(bb, TT, G)

    @pl.when(j == pl.num_programs(1) - 1)
    def _():
        c_ref[...] = c_sc[...]
        n_ref[...] = n_sc[...]
        m_ref[...] = m_sc[...]


def _mlstm_call(u_ml, gates, conv0, c0, n0, m0, cw, cbias, gbias, ng):
    B, T, _ = u_ml.shape
    L = math.gcd(T, ML_CHUNK)
    bb = math.gcd(B, ML_CHUNKS_PER_STEP)
    nc = math.gcd(T // L, ML_CHUNKS_PER_STEP // bb)
    TT = nc * L
    full = lambda a: pl.BlockSpec(a.shape, lambda b, j: (0,) * a.ndim)
    st4 = pl.BlockSpec((bb, N_HEADS, HEAD_DIM, HEAD_DIM), lambda b, j: (b, 0, 0, 0))
    st3 = pl.BlockSpec((bb, N_HEADS, HEAD_DIM), lambda b, j: (b, 0, 0))
    stm = pl.BlockSpec((bb, 1, GATE_PAD), lambda b, j: (b, 0, 0))
    return pl.pallas_call(
        functools.partial(_mlstm_kernel, L=L, bb=bb, nc=nc),
        grid=(B // bb, T // TT),
        in_specs=[pl.BlockSpec((bb, TT, ML_MAIN), lambda b, j: (b, j, 0)),
                  pl.BlockSpec((bb, TT, GATE_PAD), lambda b, j: (b, j, 0)),
                  pl.BlockSpec((bb, 3, 2 * GROUP_W), lambda b, j: (b, 0, 0)),
                  st4, st3, stm, full(cw), full(cbias), full(gbias), full(ng)],
        out_specs=[pl.BlockSpec((bb, TT, GROUP_W), lambda b, j: (b, j, 0)), st4, st3, stm],
        out_shape=[jax.ShapeDtypeStruct((B, T, GROUP_W), F32),
                   jax.ShapeDtypeStruct((B, N_HEADS, HEAD_DIM, HEAD_DIM), F32),
                   jax.ShapeDtypeStruct((B, N_HEADS, HEAD_DIM), F32),
                   jax.ShapeDtypeStruct((B, 1, GATE_PAD), F32)],
        scratch_shapes=[pltpu.VMEM((bb, 8 + TT, 2 * GROUP_W), F32),
                        pltpu.VMEM((bb, N_HEADS, HEAD_DIM, HEAD_DIM), F32),
                        pltpu.VMEM((bb, N_HEADS, HEAD_DIM), F32),
                        pltpu.VMEM((bb, 1, GATE_PAD), F32)],
        compiler_params=_params(("arbitrary", "arbitrary")),
        name="mlstm",
    )(u_ml, gates, conv0, c0, n0, m0, cw, cbias, gbias, ng)


def _pool_kernel(u_ref, hist_ref, w_ref, scale_ref, y_ref, zbuf, *, pos0):
    j = pl.program_id(1)
    bb, tt, G = u_ref.shape
    gc = G // len(POOL_WINDOWS)

    @pl.when(j == 0)
    def _():
        zbuf[:, 0:1, :] = jnp.zeros((bb, 1, G), F32)
        zbuf[:, 1:16, :] = hist_ref[...]

    u = u_ref[...]
    zbuf[:, 16:16 + tt, :] = u
    lane = _iota((1, 1, G), 2)
    pos = pos0 + j * tt + _iota((1, tt, 1), 1)
    acc = jnp.zeros((bb, tt, G), F32)
    pooled = jnp.zeros((bb, tt, G), F32)
    for t in range(max(POOL_WINDOWS)):
        acc = acc + zbuf[:, 16 - t:16 - t + tt, :]
        if (t + 1) in POOL_WINDOWS:
            gi = POOL_WINDOWS.index(t + 1)
            cnt = jnp.minimum(pos + 1, t + 1).astype(F32)
            pooled = jnp.where(lane // gc == gi, acc / cnt, pooled)
    tail = zbuf[:, tt:tt + 16, :]
    zbuf[:, 0:16, :] = tail
    pooled = (pooled - u).reshape(bb * tt, G)
    y_ref[...] = (_bdot(pooled, w_ref[...]) * scale_ref[...]).reshape(bb, tt, G)


def _pool_call(u_pool, hist, wbd, scale, pos0):
    B, T, G = u_pool.shape
    bb, tt = _row_blocks(B, T)
    return pl.pallas_call(
        functools.partial(_pool_kernel, pos0=pos0),
        grid=(B // bb, T // tt),
        in_specs=[pl.BlockSpec((bb, tt, G), lambda b, j: (b, j, 0)),
                  pl.BlockSpec((bb, POOL_HIST, G), lambda b, j: (b, 0, 0)),
                  pl.BlockSpec((G, G), lambda b, j: (0, 0)),
                  pl.BlockSpec((1, G), lambda b, j: (0, 0))],
        out_specs=pl.BlockSpec((bb, tt, G), lambda b, j: (b, j, 0)),
        out_shape=jax.ShapeDtypeStruct((B, T, G), F32),
        scratch_shapes=[pltpu.VMEM((bb, 16 + tt, G), F32)],
        compiler_params=_params(("arbitrary", "arbitrary")),
        name="pool",
    )(u_pool, hist, wbd, scale)


def _slope_of(head):
    return jnp.where(head == 0, 2.0 ** -2, jnp.where(head == 1, 2.0 ** -4, jnp.where(head == 2, 2.0 ** -6, 2.0 ** -8)))


def _topk_mask(gate, valid, nblk):
    blk = _iota((1, nblk), 1)
    cnt = jnp.zeros(gate.shape, F32)
    for n in range(nblk):
        gn = gate[:, n:n + 1]
        ahead = (gn > gate) | ((gn == gate) & (n < blk))
        cnt = cnt + jnp.where(ahead, 1.0, 0.0)
    return valid & (cnt < MOBA_TOPK)


def _moba_p_kernel(q_ref, k_ref, vt_ref, o_ref, mean_sc, kb_sc, vt_sc, sel_sc, *, nblk):
    i = pl.program_id(1)
    BLK, G, HD, H = MOBA_BLOCK, GROUP_W, HEAD_DIM, N_HEADS
    W = H * BLK

    @pl.when(i == 0)
    def _():
        for n in range(nblk):
            kblk = k_ref[0, n * BLK:(n + 1) * BLK, :]
            mean_sc[n:n + 1, :] = jnp.mean(kblk, axis=0, keepdims=True)
            kb_sc[n] = kblk.astype(BF16)
            vt_sc[n] = vt_ref[0, :, n * BLK:(n + 1) * BLK].astype(BF16)

    q = q_ref[0]
    lane_head = _iota((1, G), 1) // HD
    qbd = jnp.concatenate([jnp.where(lane_head == h, q, 0.0) for h in range(H)], axis=0)
    qb = (qbd * (HD ** -0.5)).astype(BF16)
    col_head = _iota((1, W), 1) // BLK
    col_t = _iota((1, W), 1) % BLK
    slope = _slope_of(col_head).astype(F32)

    blk_row = _iota((nblk, 1), 0)
    valid = blk_row < i
    gate = jnp.where(valid, _dot3(mean_sc[...], qbd, NT), -jnp.inf)
    cnt = jnp.zeros((nblk, W), F32)
    for n in range(nblk):
        gn = gate[n:n + 1, :]
        ahead = (gn > gate) | ((gn == gate) & (n < blk_row))
        cnt = cnt + jnp.where(ahead, 1.0, 0.0)
    sel = jnp.where(valid & (cnt < MOBA_TOPK), 1.0, 0.0)
    for n in range(nblk):
        sel_sc[n] = sel[n:n + 1, :]

    s_idx = _iota((BLK, W), 0)
    base = slope * (col_t - s_idx).astype(F32)

    def pv(jb, pb):
        vt = vt_sc[jb]
        return [_dot(vt[h * HD:(h + 1) * HD, :], pb[:, h * BLK:(h + 1) * BLK]) for h in range(H)]

    s = _dot(kb_sc[i], qb, NT) - base
    s = jnp.where(s_idx <= col_t, s, -jnp.inf)
    m0 = jnp.max(s, 0, keepdims=True)
    p = jnp.exp(s - m0)
    l0 = jnp.sum(p, 0, keepdims=True)
    acc0 = pv(i, p.astype(BF16))

    def body(jb, carry):
        m, l, acc = carry
        off = ((i - jb) * BLK).astype(F32)
        row_off = jnp.where(sel_sc[jb] > 0.0, slope * off, jnp.inf)
        s = _dot(kb_sc[jb], qb, NT) - base - row_off
        m2 = jnp.maximum(m, jnp.max(s, 0, keepdims=True))
        alpha = jnp.exp(m - m2)
        p = jnp.exp(s - m2)
        new = pv(jb, p.astype(BF16))
        acc = [alpha[:, h * BLK:(h + 1) * BLK] * acc[h] + new[h] for h in range(H)]
        return m2, alpha * l + jnp.sum(p, 0, keepdims=True), acc

    m, l, acc = lax.fori_loop(0, i, body, (m0, l0, acc0))
    out_t = jnp.concatenate([acc[h] / l[:, h * BLK:(h + 1) * BLK] for h in range(H)], axis=0)
    o_ref[0] = out_t.T


def _moba_prompt_call(q, k, v_t):
    B, T, G = q.shape
    assert T % MOBA_BLOCK == 0
    nblk = T // MOBA_BLOCK
    return pl.pallas_call(
        functools.partial(_moba_p_kernel, nblk=nblk),
        grid=(B, nblk),
        in_specs=[pl.BlockSpec((1, MOBA_BLOCK, G), lambda b, i: (b, i, 0)),
                  pl.BlockSpec((1, T, G), lambda b, i: (b, 0, 0)),
                  pl.BlockSpec((1, G, T), lambda b, i: (b, 0, 0))],
        out_specs=pl.BlockSpec((1, MOBA_BLOCK, G), lambda b, i: (b, i, 0)),
        out_shape=jax.ShapeDtypeStruct((B, T, G), F32),
        scratch_shapes=[pltpu.VMEM((nblk, G), F32),
                        pltpu.VMEM((nblk, MOBA_BLOCK, G), BF16),
                        pltpu.VMEM((nblk, G, MOBA_BLOCK), BF16),
                        pltpu.VMEM((nblk, 1, N_HEADS * MOBA_BLOCK), F32)],
        compiler_params=_params(("arbitrary", "arbitrary")),
        name="moba_prompt",
    )(q, k, v_t)


def _moba_s_kernel(pt_ref, q_ref, kn_ref, vn_ref, *refs, n_pages, page, past_len, rb):
    kp = refs[:rb * n_pages]
    vp = refs[rb * n_pages:2 * rb * n_pages]
    o_ref = refs[2 * rb * n_pages]
    G, HD = GROUP_W, HEAD_DIM
    ts = q_ref.shape[1]
    R = N_HEADS * ts
    ppb = MOBA_BLOCK // page
    nb = n_pages // ppb
    rows = range(rb)
    pages = range(n_pages)
    lane_head = _iota((1, G), 1) // HD
    row_head = _iota((R, 1), 0) // ts
    row_t = _iota((R, 1), 0) % ts
    slope = _slope_of(row_head)
    scale = HD ** -0.5
    qbd = [jnp.concatenate([jnp.where(lane_head == h, q_ref[r], 0.0) for h in range(N_HEADS)], axis=0)
           for r in rows]
    qb = [x.astype(BF16) for x in qbd]

    kpg = {(r, p): kp[r * n_pages + p][...] for r in rows for p in pages}
    raw = {(r, p): _dot(qb[r], kpg[r, p].astype(BF16)) for r in rows for p in pages}
    sums = {(r, p): jnp.sum(kpg[r, p], axis=1, keepdims=True) for r in rows for p in pages}
    means_t = [jnp.concatenate(
        [sum(sums[r, p] for p in range(n * ppb, (n + 1) * ppb)) * (1.0 / MOBA_BLOCK) for n in range(nb)], axis=1)
        for r in rows]
    gate = [_dot3(qbd[r], means_t[r]) for r in rows]
    all_valid = _iota((1, nb), 1) >= 0
    sel = [jnp.where(_topk_mask(gate[r], all_valid, nb), 1.0, 0.0) for r in rows]

    q_pos = (past_len + row_t).astype(F32)
    off = _iota((1, page), 1).astype(F32)
    scores = {(r, p): jnp.where(sel[r][:, p // ppb:p // ppb + 1] > 0.0,
                                raw[r, p] * scale - slope * (q_pos - (p * page + off)), -jnp.inf)
              for r in rows for p in pages}
    t_new = _iota((1, ts), 1)
    s_own = [jnp.where(t_new <= row_t,
                       _dot(qb[r], kn_ref[r].astype(BF16), NT) * scale - slope * (row_t - t_new).astype(F32),
                       -jnp.inf) for r in rows]

    m = [jnp.max(s_own[r], -1, keepdims=True) for r in rows]
    for p in pages:
        m = [jnp.maximum(m[r], jnp.max(scores[r, p], -1, keepdims=True)) for r in rows]
    p_own = [jnp.exp(s_own[r] - m[r]) for r in rows]
    l = [jnp.sum(p_own[r], -1, keepdims=True) for r in rows]
    acc = [_dot(p_own[r].astype(BF16), vn_ref[r].astype(BF16)) for r in rows]
    for p in pages:
        pr = [jnp.exp(scores[r, p] - m[r]) for r in rows]
        l = [l[r] + jnp.sum(pr[r], -1, keepdims=True) for r in rows]
        acc = [acc[r] + _dot(pr[r].astype(BF16), vp[r * n_pages + p][...].astype(BF16), NT) for r in rows]
    for r in rows:
        out = acc[r] / l[r]
        y = jnp.zeros((ts, G), F32)
        for h in range(N_HEADS):
            y = jnp.where(lane_head == h, out[h * ts:(h + 1) * ts, :], y)
        o_ref[r] = y


def _moba_sample_call(q, k, v, cache_k, cache_v, page_table, layer):
    DB, TS, G = q.shape
    Ld, n_phys, page, H, d = cache_k.shape
    n_pages = page_table.shape[1]
    past_len = n_pages * page
    assert past_len % MOBA_BLOCK == 0 and MOBA_BLOCK % page == 0 and past_len // MOBA_BLOCK >= MOBA_TOPK
    ck = jnp.transpose(cache_k, (0, 1, 3, 4, 2)).reshape(Ld, n_phys, H * d, page)
    cv = jnp.transpose(cache_v, (0, 1, 3, 4, 2)).reshape(Ld, n_phys, H * d, page)
    rb = math.gcd(DB, MOBA_SAMPLE_ROWS_PER_STEP)
    new = pl.BlockSpec((rb, TS, G), lambda b, pt: (b, 0, 0))
    pg = [pl.BlockSpec((None, None, G, page),
                       functools.partial(lambda b, pt, r, p: (layer, pt[b * rb + r, p], 0, 0), r=r, p=p))
          for r in range(rb) for p in range(n_pages)]
    return pl.pallas_call(
        functools.partial(_moba_s_kernel, n_pages=n_pages, page=page, past_len=past_len, rb=rb),
        grid_spec=pltpu.PrefetchScalarGridSpec(
            num_scalar_prefetch=1, grid=(DB // rb,),
            in_specs=[new, new, new] + pg + pg,
            out_specs=new),
        out_shape=jax.ShapeDtypeStruct((DB, TS, G), F32),
        compiler_params=_params(("arbitrary",)),
        name="moba_sample",
    )(page_table, q, k, v, *([ck] * (rb * n_pages)), *([cv] * (rb * n_pages)))


def _out_kernel(yrw_ref, yml_ref, ypl_ref, yat_ref, x_ref, g1_ref, sc2_ref, sh2_ref, g2_ref, n2g_ref, fng_ref,
                wout_ref, wup_ref, wdn_ref, o_ref, x1_sc, h2_sc, acc_sc, *, final):
    f = pl.program_id(2)
    bb, tt, D = x_ref.shape
    M = bb * tt

    @pl.when(f == 0)
    def _():
        ycat = jnp.concatenate([r[...].reshape(M, GROUP_W) for r in (yrw_ref, yml_ref, ypl_ref, yat_ref)], axis=-1)
        y = _dot(ycat.astype(BF16), wout_ref[...]).reshape(bb, tt, D)
        x1 = x_ref[...] + g1_ref[...] * y
        x1_sc[...] = x1
        hn = x1 * lax.rsqrt(jnp.mean(x1 * x1, -1, keepdims=True) + NORM_EPS) * n2g_ref[...]
        h2_sc[...] = (hn * (1.0 + sc2_ref[...]) + sh2_ref[...]).reshape(M, D).astype(BF16)
        acc_sc[...] = jnp.zeros((M, D), F32)

    a = _dot(h2_sc[...], wup_ref[...])
    a = jnp.square(jnp.maximum(a, 0.0)).astype(BF16)
    acc_sc[...] += _dot(a, wdn_ref[...])

    @pl.when(f == pl.num_programs(2) - 1)
    def _():
        x2 = x1_sc[...] + g2_ref[...] * acc_sc[...].reshape(bb, tt, D)
        if final:
            x2 = x2 * lax.rsqrt(jnp.mean(x2 * x2, -1, keepdims=True) + NORM_EPS) * fng_ref[...]
        o_ref[...] = x2


def _out_call(ys, x, g1, sc2, sh2, g2, n2g, fng, wout, wup, wdn, final):
    B, T, D = x.shape
    bb, tt = _row_blocks(B, T)
    FF = wup.shape[1]
    row = lambda n: pl.BlockSpec((bb, tt, n), lambda b, j, f: (b, j, 0))
    mod = pl.BlockSpec((bb, 1, D), lambda b, j, f: (b, 0, 0))
    vec = pl.BlockSpec((1, D), lambda b, j, f: (0, 0))
    return pl.pallas_call(
        functools.partial(_out_kernel, final=final),
        grid=(B // bb, T // tt, FF // FF_TILE),
        in_specs=[row(GROUP_W)] * 4 + [row(D), mod, mod, mod, mod, vec, vec,
                  pl.BlockSpec((D, D), lambda b, j, f: (0, 0)),
                  pl.BlockSpec((D, FF_TILE), lambda b, j, f: (0, f)),
                  pl.BlockSpec((FF_TILE, D), lambda b, j, f: (f, 0))],
        out_specs=row(D),
        out_shape=jax.ShapeDtypeStruct((B, T, D), F32),
        scratch_shapes=[pltpu.VMEM((bb, tt, D), F32),
                        pltpu.VMEM((bb * tt, D), BF16),
                        pltpu.VMEM((bb * tt, D), F32)],
        compiler_params=_params(("arbitrary", "arbitrary", "arbitrary")),
        name="out_mlp",
    )(*ys, x, g1, sc2, sh2, g2, n2g, fng, wout, wup, wdn)


_RW_PERM = np.concatenate([np.arange(0, 256), np.arange(320, 576), np.arange(576, 832),
                           np.arange(256, 320), np.arange(832, 896), np.arange(896, 1024)])
_RW_INV = np.argsort(_RW_PERM)


def _layer(x, mods, st, pos0, attend, lw, final, transposed_kv):
    sh1, sc1, g1, sh2, sc2, g2 = mods
    shift0, wkv0, conv0, c0, n0, m0, pool0 = st
    B, T, D = x.shape
    outs = _in_call(x, sc1, sh1, lw['norm1_g'], lw['wpack'], lw['wkv_t'], transposed_kv)
    u_rw, u_ml, gates, u_pool, q, k = outs[:6]

    y_rw, wkv1 = _rwkv_call(u_rw, shift0[:, _RW_PERM][:, None, :], wkv0, lw['rw_mu'], lw['rw_vec'],
                            lw['rwkv_w_up'], lw['rwkv_a_up'], lw['rwkv_g_up'])
    shift1 = u_rw[:, -1, :][:, _RW_INV]

    m0p = jnp.pad(m0, ((0, 0), (0, GATE_PAD - N_HEADS)))[:, None, :]
    y_ml, c1, n1, m1p = _mlstm_call(u_ml, gates, conv0, c0, n0, m0p, lw['mlstm_conv_w'], lw['ml_cbias'],
                                    lw['ml_gbias'], lw['ml_ng'])
    m1 = m1p[:, 0, :N_HEADS]
    zc = u_ml[:, :, :2 * GROUP_W] if T >= 3 else jnp.concatenate([conv0, u_ml[:, :, :2 * GROUP_W]], 1)
    conv1 = zc[:, -3:]

    y_pool = _pool_call(u_pool, pool0, lw['pool_wbd'], lw['pool_scale'], pos0)
    zp = u_pool if T >= POOL_HIST else jnp.concatenate([pool0, u_pool], 1)
    pool1 = zp[:, -POOL_HIST:]

    if transposed_kv:
        k_out, v_out = outs[6], outs[7]
        y_at = attend(q, k, v_out)
    else:
        kv_shape = (B, T, N_HEADS, HEAD_DIM)
        y_at = attend(q, k, outs[6])
        k_out, v_out = k.reshape(kv_shape), outs[6].reshape(kv_shape)

    x_new = _out_call((y_rw, y_ml, y_pool, y_at), x, g1, sc2, sh2, g2, lw['norm2_g'], lw['final_g'],
                      lw['w_out'], lw['mlp_up'], lw['mlp_down'], final)
    return x_new, (shift1, wkv1, conv1, c1, n1, m1, pool1, k_out, v_out)


def kernel(x_prompt, x_sample, c_prompt, c_sample, state_rwkv_shift, state_rwkv_wkv, state_mlstm_conv, state_mlstm_c, state_mlstm_n, state_mlstm_m, state_pool, cache_k, cache_v, page_table, ada_w, ada_b, norm1_g, norm2_g, w_in, w_out, rwkv_mu, rwkv_w0, rwkv_w_up, rwkv_a0, rwkv_a_up, rwkv_g_up, rwkv_k_k, rwkv_k_a, rwkv_r_k, rwkv_ln_g, rwkv_ln_b, mlstm_conv_w, mlstm_conv_b, mlstm_i_b, mlstm_f_b, mlstm_norm_g, pool_w, pool_scale, mlp_up, mlp_down, final_norm_g):
    B, T, D = x_prompt.shape
    DB = x_sample.shape[0]
    depth = ada_w.shape[0]
    G = GROUP_W
    assert D == D_MODEL and w_in.shape[-1] == N_IN

    mod = _ada_call(jnp.concatenate([c_prompt, c_sample], 0), ada_w, ada_b)

    st_p0 = (jnp.zeros((B, RW_COLS), F32), jnp.zeros((B, N_HEADS, HEAD_DIM, HEAD_DIM), F32),
             jnp.zeros((B, 3, 2 * G), F32), jnp.zeros((B, N_HEADS, HEAD_DIM, HEAD_DIM), F32),
             jnp.zeros((B, N_HEADS, HEAD_DIM), F32), jnp.zeros((B, N_HEADS), F32),
             jnp.zeros((B, POOL_HIST, G), F32))
    past_len = page_table.shape[1] * cache_k.shape[2]

    xp, xs = x_prompt, x_sample
    new_p, new_s = [], []
    for l in range(depth):
        wl = w_in[l]
        wpack = jnp.concatenate([
            wl[:, :RW_COLS][:, _RW_PERM],
            wl[:, RW_COLS:RW_COLS + ML_MAIN],
            jnp.pad(wl[:, RW_COLS + ML_MAIN:RW_COLS + ML_MAIN + ML_GATES], ((0, 0), (0, GATE_PAD - ML_GATES))),
            wl[:, RW_COLS + ML_MAIN + ML_GATES:],
        ], axis=1).astype(BF16)
        pool_wbd = jnp.zeros((G, G), F32)
        gcw = G // len(POOL_WINDOWS)
        for gi in range(len(POOL_WINDOWS)):
            pool_wbd = pool_wbd.at[gi * gcw:(gi + 1) * gcw, gi * gcw:(gi + 1) * gcw].set(pool_w[l, gi])
        row = lambda a: a.reshape(1, -1)
        lw = {
            'norm1_g': row(norm1_g[l]), 'norm2_g': row(norm2_g[l]), 'final_g': row(final_norm_g),
            'wpack': wpack, 'wkv_t': wl[:, N_IN - 2 * G:].T.astype(BF16), 'w_out': w_out[l].astype(BF16),
            'mlp_up': mlp_up[l].astype(BF16), 'mlp_down': mlp_down[l].astype(BF16),
            'rw_mu': row(rwkv_mu[l][_RW_PERM]),
            'rw_vec': jnp.stack([rwkv_w0[l], rwkv_a0[l], rwkv_k_k[l], rwkv_k_a[l], rwkv_ln_g[l], rwkv_ln_b[l],
                                 rwkv_r_k[l].reshape(-1), jnp.zeros((G,), F32)]),
            'rwkv_w_up': rwkv_w_up[l], 'rwkv_a_up': rwkv_a_up[l], 'rwkv_g_up': rwkv_g_up[l],
            'mlstm_conv_w': mlstm_conv_w[l], 'ml_cbias': row(mlstm_conv_b[l]),
            'ml_gbias': row(jnp.pad(jnp.concatenate([mlstm_i_b[l], mlstm_f_b[l]]), (0, GATE_PAD - ML_GATES))),
            'ml_ng': row(mlstm_norm_g[l]),
            'pool_wbd': pool_wbd, 'pool_scale': row(pool_scale[l]),
        }
        mods = [mod[l, :, i * D:(i + 1) * D][:, None, :] for i in range(6)]
        mods_p = [m[:B] for m in mods]
        mods_s = [m[B:] for m in mods]
        final = l == depth - 1

        xp, st_p = _layer(xp, mods_p, st_p0, 0, _moba_prompt_call, lw, final, True)
        st_s_in = (state_rwkv_shift[l], state_rwkv_wkv[l], state_mlstm_conv[l], state_mlstm_c[l],
                   state_mlstm_n[l], state_mlstm_m[l], state_pool[l])
        attend_s = functools.partial(_moba_sample_call, cache_k=cache_k, cache_v=cache_v,
                                     page_table=page_table, layer=l)
        xs, st_s = _layer(xs, mods_s, st_s_in, past_len, attend_s, lw, final, False)
        new_p.append(st_p)
        new_s.append(st_s)

    stack = lambda lst: tuple(jnp.stack([st[i] for st in lst]) for i in range(9))
    out_p = list(stack(new_p))
    for i in (7, 8):
        out_p[i] = jnp.transpose(out_p[i].reshape(depth, B, N_HEADS, HEAD_DIM, T), (0, 1, 4, 2, 3))
    return (xp, xs) + tuple(out_p) + stack(new_s)
```

```python
import functools
import math

import jax
import jax.numpy as jnp
from jax import lax
from jax.experimental import pallas as pl
from jax.experimental.pallas import tpu as pltpu

F32 = jnp.float32
BF16 = jnp.bfloat16

D_MODEL = 1024
GROUP_W = 256
HEAD_DIM = 64
N_HEADS = 4
RW_COLS = 1024
ML_MAIN = 1024
ML_GATES = 8
GATE_PAD = 128
N_IN = 3080
MOBA_BLOCK = 256
MOBA_TOPK = 3
ML_CHUNK = 64
RW_CHUNK = 64
RW_CHUNKS_PER_STEP = 4
ML_CHUNKS_PER_STEP = 4
MOBA_SAMPLE_ROWS_PER_STEP = 4
POOL_WINDOWS = (2, 4, 8, 16)
POOL_HIST = 15
NORM_EPS = 1e-6
RW_GN_EPS = 64e-5
ML_NORM_EPS = 1e-6
ROW_TILE = 512
FF_TILE = 1024
VMEM_LIMIT = 48 * 1024 * 1024
OUT_ROW_TILE = 1024
OUT_VMEM_LIMIT = 60 * 1024 * 1024

NN = (((1,), (0,)), ((), ()))
NT = (((1,), (1,)), ((), ()))
TN = (((0,), (0,)), ((), ()))


def _dot(a, b, dn=NN):
    return lax.dot_general(a, b, dn, preferred_element_type=F32)


def _bdot(a, b, dn=NN):
    return _dot(a.astype(BF16), b.astype(BF16), dn)


def _split2(x):
    hi = x.astype(BF16)
    lo = (x - hi.astype(F32)).astype(BF16)
    return hi, lo


def _split3(x):
    hi = x.astype(BF16)
    r = x - hi.astype(F32)
    mid = r.astype(BF16)
    lo = (r - mid.astype(F32)).astype(BF16)
    return hi, mid, lo


def _dot3(a, b, dn=NN):
    ah, al = _split2(a)
    bh, bl = _split2(b)
    return _dot(ah, bh, dn) + (_dot(ah, bl, dn) + _dot(al, bh, dn))


def _dot3s(a_s, b_s, dn=NN):
    return _dot(a_s[0], b_s[0], dn) + (_dot(a_s[0], b_s[1], dn) + _dot(a_s[1], b_s[0], dn))


def _dot_exact_rhs(a, b01, dn=NN):
    h, m, l = _split3(a)
    b = b01.astype(BF16)
    return _dot(h, b, dn) + (_dot(m, b, dn) + _dot(l, b, dn))


def _dot_exact_lhs(a01, b, dn=NN):
    h, m, l = _split3(b)
    a = a01.astype(BF16)
    return _dot(a, h, dn) + (_dot(a, m, dn) + _dot(a, l, dn))


def _iota(shape, dim):
    return lax.broadcasted_iota(jnp.int32, shape, dim)


def _sigmoid(x):
    return jax.nn.sigmoid(x)


def _softplus(x):
    return jnp.maximum(x, 0.0) + jnp.log(1.0 + jnp.exp(-jnp.abs(x)))


def _block_ones():
    return (_iota((GROUP_W, GROUP_W), 0) // HEAD_DIM == _iota((GROUP_W, GROUP_W), 1) // HEAD_DIM).astype(F32)


def _head_sum(x, bo):
    return _dot_exact_rhs(x, bo)


def _head_sum1(x, bo):
    return _bdot(x, bo)


def _params(sem, vmem_limit=VMEM_LIMIT):
    return pltpu.CompilerParams(dimension_semantics=sem, vmem_limit_bytes=vmem_limit)


def _row_blocks(B, T, tile=ROW_TILE):
    if T >= tile:
        assert T % tile == 0
        return 1, tile
    bb = max(1, min(B, tile // T))
    while B % bb:
        bb -= 1
    return bb, T


def _ada_kernel(c_ref, w_ref, b_ref, o_ref):
    c = c_ref[...]
    o_ref[...] = _bdot(c * _sigmoid(c), w_ref[...]) + b_ref[...]


def _ada_call(c_all, ada_w, ada_b):
    Ld, D, N6 = ada_w.shape
    NB = c_all.shape[0]
    tn = 1024
    return pl.pallas_call(
        _ada_kernel,
        grid=(Ld, N6 // tn),
        in_specs=[pl.BlockSpec((NB, D), lambda l, j: (0, 0)),
                  pl.BlockSpec((None, D, tn), lambda l, j: (l, 0, j)),
                  pl.BlockSpec((None, 1, tn), lambda l, j: (l, 0, j))],
        out_specs=pl.BlockSpec((None, NB, tn), lambda l, j: (l, 0, j)),
        out_shape=jax.ShapeDtypeStruct((Ld, NB, N6), F32),
        compiler_params=_params(("arbitrary", "arbitrary")),
        name="ada_mod",
    )(c_all, ada_w, ada_b.reshape(Ld, 1, N6))


IN_WIDTHS = (RW_COLS, ML_MAIN, GATE_PAD, GROUP_W, GROUP_W, GROUP_W, GROUP_W)


def _in_kernel(x_ref, sc_ref, sh_ref, g_ref, w_ref, wt_ref, *out_refs, n_rows_out):
    x = x_ref[...]
    bb, tt, D = x.shape
    y = x * lax.rsqrt(jnp.mean(x * x, -1, keepdims=True) + NORM_EPS) * g_ref[...]
    h = y * (1.0 + sc_ref[...]) + sh_ref[...]
    hb = h.reshape(bb * tt, D).astype(BF16)
    off = 0
    for ref in out_refs[:n_rows_out]:
        n = ref.shape[-1]
        ref[...] = _dot(hb, w_ref[:, off:off + n]).reshape(bb, tt, n)
        off += n
    for i, ref in enumerate(out_refs[n_rows_out:]):
        ref[0] = _dot(wt_ref[i * GROUP_W:(i + 1) * GROUP_W, :], hb, NT)


def _in_call(x, sc, sh, g, wpack, wkv_t, transposed_kv):
    B, T, D = x.shape
    bb, tt = _row_blocks(B, T)
    row = lambda n: pl.BlockSpec((bb, tt, n), lambda b, j: (b, j, 0))
    mod = pl.BlockSpec((bb, 1, D), lambda b, j: (b, 0, 0))
    widths = IN_WIDTHS[:-1] if transposed_kv else IN_WIDTHS
    out_specs = [row(n) for n in widths]
    out_shape = [jax.ShapeDtypeStruct((B, T, n), F32) for n in widths]
    if transposed_kv:
        assert bb == 1
        out_specs += [pl.BlockSpec((1, GROUP_W, tt), lambda b, j: (b, 0, j))] * 2
        out_shape += [jax.ShapeDtypeStruct((B, GROUP_W, T), F32)] * 2
    return pl.pallas_call(
        functools.partial(_in_kernel, n_rows_out=len(widths)),
        grid=(B // bb, T // tt),
        in_specs=[row(D), mod, mod,
                  pl.BlockSpec((1, D), lambda b, j: (0, 0)),
                  pl.BlockSpec(wpack.shape, lambda b, j: (0, 0)),
                  pl.BlockSpec(wkv_t.shape, lambda b, j: (0, 0))],
        out_specs=out_specs,
        out_shape=out_shape,
        compiler_params=_params(("arbitrary", "arbitrary")),
        name="in_proj",
    )(x, sc, sh, g, wpack, wkv_t)


def _rwkv_kernel(u_ref, s0_ref, wkv0_ref, mu_ref, vec_ref, wup_ref, aup_ref, gup_ref,
                 y_ref, wkv_ref, zbuf, s_sc, *, L, bb, nc):
    j = pl.program_id(1)
    G, HD = GROUP_W, HEAD_DIM
    TT = nc * L
    R = bb * TT

    @pl.when(j == 0)
    def _():
        zbuf[:, 7:8, :] = s0_ref[...]
        s_sc[...] = wkv0_ref[...]

    u3 = u_ref[...]
    zbuf[:, 8:8 + TT, :] = u3
    prev = zbuf[:, 7:7 + TT, :]
    xs = (u3 + (prev - u3) * mu_ref[...]).reshape(R, RW_COLS)
    zbuf[:, 7:8, :] = u3[:, TT - 1:TT, :]

    r = xs[:, 0:G]
    k = xs[:, G:2 * G]
    v = xs[:, 2 * G:3 * G]
    wc = xs[:, 3 * G:3 * G + 64]
    ac = xs[:, 3 * G + 64:3 * G + 128]
    gc = xs[:, 3 * G + 128:4 * G]
    w0, a0, k_k, k_a = vec_ref[0:1, :], vec_ref[1:2, :], vec_ref[2:3, :], vec_ref[3:4, :]
    ln_g, ln_b, r_k = vec_ref[4:5, :], vec_ref[5:6, :], vec_ref[6:7, :]

    w_log = -_softplus(-(w0 + _bdot(jnp.tanh(wc), wup_ref[...]))) - 0.5
    lw = -jnp.exp(w_log)
    a = _sigmoid(a0 + _bdot(ac, aup_ref[...]))
    g = _bdot(_sigmoid(gc), gup_ref[...])
    bo = _block_ones()
    kk = k * k_k
    kk = kk / jnp.maximum(jnp.sqrt(_head_sum(kk * kk, bo)), 1e-12)
    k2 = k * (1.0 + (a - 1.0) * k_a)
    beta = kk * a

    rr = _iota((R, R), 0)
    rc = _iota((R, R), 1)
    same = (rr // L) == (rc // L)
    c = _dot_exact_lhs((same & (rr >= rc)).astype(F32), lw)
    cl = _dot_exact_lhs(same.astype(F32), lw)
    e_inv = jnp.exp(-c)
    e_tail = jnp.exp(cl - c)
    a_t = -kk * jnp.exp(c - lw)
    b_t = beta * e_inv
    k_t = k2 * e_inv
    r_t = r * jnp.exp(c)
    b_l = beta * e_tail
    k_l = k2 * e_tail
    e_cl = jnp.exp(cl)

    row = _iota((L, L), 0)
    col = _iota((L, L), 1)
    incl = row >= col
    strict = row > col
    eye = (row == col).astype(F32)
    n_double = max(1, int(math.ceil(math.log2(L)))) - 1

    chains = [(ci, h) for ci in range(bb * nc) for h in range(N_HEADS)]

    def cut(x, ch):
        ci, h = ch
        return x[ci * L:(ci + 1) * L, h * HD:(h + 1) * HD]

    a_s = {ch: _split2(cut(a_t, ch)) for ch in chains}
    b_s = {ch: _split2(cut(b_t, ch)) for ch in chains}
    r_b = {ch: cut(r_t, ch).astype(BF16) for ch in chains}
    k_b = {ch: cut(k_t, ch).astype(BF16) for ch in chains}
    v_b = {ch: cut(v, ch).astype(BF16) for ch in chains}
    n_ab = {ch: jnp.where(strict, _dot3s(a_s[ch], b_s[ch], NT), 0.0) for ch in chains}
    n_ak = {ch: jnp.where(strict, _dot(a_s[ch][0], k_b[ch], NT), 0.0).astype(BF16) for ch in chains}
    n_rb = {ch: jnp.where(incl, _dot(r_b[ch], b_s[ch][0], NT), 0.0).astype(BF16) for ch in chains}
    n_rk = {ch: jnp.where(incl, _dot(r_b[ch], k_b[ch], NT), 0.0).astype(BF16) for ch in chains}
    p = {ch: eye + n_ab[ch] for ch in chains}
    m_s = {ch: _split2(n_ab[ch]) for ch in chains}
    for _ in range(n_double):
        m_s = {ch: _split2(_dot3s(m_s[ch], m_s[ch])) for ch in chains}
        p = {ch: p[ch] + _dot3s(m_s[ch], _split2(p[ch])) for ch in chains}
    p_b = {ch: p[ch].astype(BF16) for ch in chains}
    akv = {ch: _dot(n_ak[ch], v_b[ch]).astype(BF16) for ch in chains}
    y_v = {ch: _dot(n_rk[ch], v_b[ch]) for ch in chains}
    w_b = {ch: _dot(p_b[ch], a_s[ch][0]).astype(BF16) for ch in chains}
    u_t = {ch: _dot(p_b[ch], akv[ch]) for ch in chains}
    bl_b = {ch: cut(b_l, ch).astype(BF16) for ch in chains}
    kl_b = {ch: cut(k_l, ch).astype(BF16) for ch in chains}

    heads = range(N_HEADS)
    bh = [(b, h) for b in range(bb) for h in heads]
    s = {k_: s_sc[k_[0], k_[1]] for k_ in bh}
    y_chunk = {}
    for cc in range(nc):
        ch_of = {(b, h): (b * nc + cc, h) for b, h in bh}
        s_b = {k_: s[k_].astype(BF16) for k_ in bh}
        e_b = {k_: (_dot(w_b[ch_of[k_]], s_b[k_], NT) + u_t[ch_of[k_]]).astype(BF16) for k_ in bh}
        s = {k_: s[k_] * cut(e_cl, ch_of[k_])[0:1, :]
             + _dot(e_b[k_], bl_b[ch_of[k_]], TN) + _dot(v_b[ch_of[k_]], kl_b[ch_of[k_]], TN) for k_ in bh}
        yh = {k_: _dot(r_b[ch_of[k_]], s_b[k_], NT) + _dot(n_rb[ch_of[k_]], e_b[k_]) + y_v[ch_of[k_]] for k_ in bh}
        for b in range(bb):
            y_chunk[b * nc + cc] = jnp.concatenate([yh[b, h] for h in heads], axis=-1)
    for k_ in bh:
        s_sc[k_[0], k_[1]] = s[k_]

    ys = [y_chunk[ci] for ci in range(bb * nc)]
    y = jnp.concatenate(ys, axis=0) if len(ys) > 1 else ys[0]
    mu_y = _head_sum1(y, bo) * (1.0 / HD)
    yc = y - mu_y
    var = _head_sum1(yc * yc, bo) * (1.0 / HD)
    yn = yc * lax.rsqrt(var + RW_GN_EPS) * ln_g + ln_b
    bonus = _head_sum1(r * k2 * r_k, bo) * v
    y_ref[...] = ((yn + bonus) * g).reshape(bb, TT, G)

    @pl.when(j == pl.num_programs(1) - 1)
    def _():
        wkv_ref[...] = s_sc[...]


def _rwkv_call(u_rw, shift0, wkv0, mu, vec, wup, aup, gup):
    B, T, _ = u_rw.shape
    L = math.gcd(T, RW_CHUNK)
    bb = math.gcd(B, RW_CHUNKS_PER_STEP)
    nc = math.gcd(T // L, RW_CHUNKS_PER_STEP // bb)
    TT = nc * L
    full = lambda a: pl.BlockSpec(a.shape, lambda b, j: (0,) * a.ndim)
    return pl.pallas_call(
        functools.partial(_rwkv_kernel, L=L, bb=bb, nc=nc),
        grid=(B // bb, T // TT),
        in_specs=[pl.BlockSpec((bb, TT, RW_COLS), lambda b, j: (b, j, 0)),
                  pl.BlockSpec((bb, 1, RW_COLS), lambda b, j: (b, 0, 0)),
                  pl.BlockSpec((bb, N_HEADS, HEAD_DIM, HEAD_DIM), lambda b, j: (b, 0, 0, 0)),
                  full(mu), full(vec), full(wup), full(aup), full(gup)],
        out_specs=[pl.BlockSpec((bb, TT, GROUP_W), lambda b, j: (b, j, 0)),
                   pl.BlockSpec((bb, N_HEADS, HEAD_DIM, HEAD_DIM), lambda b, j: (b, 0, 0, 0))],
        out_shape=[jax.ShapeDtypeStruct((B, T, GROUP_W), F32),
                   jax.ShapeDtypeStruct((B, N_HEADS, HEAD_DIM, HEAD_DIM), F32)],
        scratch_shapes=[pltpu.VMEM((bb, 8 + TT, RW_COLS), F32),
                        pltpu.VMEM((bb, N_HEADS, HEAD_DIM, HEAD_DIM), F32)],
        compiler_params=_params(("arbitrary", "arbitrary")),
        name="rwkv7",
    )(u_rw, shift0, wkv0, mu, vec, wup, aup, gup)


def _mlstm_kernel(u_ref, gt_ref, cb_ref, c0_ref, n0_ref, m0_ref, cw_ref, cbias_ref, gbias_ref, ng_ref,
                  y_ref, c_ref, n_ref, m_ref, zbuf, c_sc, n_sc, m_sc, *, L, bb, nc):
    j = pl.program_id(1)
    G, HD = GROUP_W, HEAD_DIM
    TT = nc * L
    R = bb * TT

    @pl.when(j == 0)
    def _():
        zbuf[:, 5:8, :] = cb_ref[...]
        c_sc[...] = c0_ref[...]
        n_sc[...] = n0_ref[...]
        m_sc[...] = m0_ref[...]

    u3 = u_ref[...]
    zbuf[:, 8:8 + TT, :] = u3[:, :, 0:2 * G]
    conv = cbias_ref[...]
    for t in range(4):
        conv = conv + zbuf[:, 5 + t:5 + t + TT, :] * cw_ref[t:t + 1, :]
    tail = zbuf[:, 5 + TT:8 + TT, :]
    zbuf[:, 5:8, :] = tail
    conv = conv.reshape(R, 2 * G)
    u = u3.reshape(R, ML_MAIN)
    sq = conv * _sigmoid(conv)
    q = sq[:, 0:G]
    k = sq[:, G:2 * G] * (HD ** -0.5)
    v = u[:, 2 * G:3 * G]
    o = u[:, 3 * G:4 * G]

    gates = gt_ref[...].reshape(R, GATE_PAD) + gbias_ref[...]
    lane = _iota((1, GATE_PAD), 1)
    gl = jnp.where(lane < N_HEADS, gates, -_softplus(-gates))
    rr = _iota((R, R), 0)
    rc = _iota((R, R), 1)
    same = (rr // L) == (rc // L)
    bcol = _dot_exact_lhs((same & (rr >= rc)).astype(F32), gl)
    gtot = _dot_exact_lhs(same.astype(F32), gl)
    sel = jnp.concatenate([(same & (rr <= rc)).astype(F32), (rr == rc).astype(F32)], axis=1)
    brow = _dot_exact_rhs(gl, sel, TN)
    causal = _iota((L, L), 0) >= _iota((L, L), 1)

    heads = range(N_HEADS)
    chains = [(ci, h) for ci in range(bb * nc) for h in heads]

    def cut(x, ch):
        ci, h = ch
        return x[ci * L:(ci + 1) * L, h * HD:(h + 1) * HD]

    def rows_of(x, ch, lane0):
        ci, h = ch
        return x[ci * L:(ci + 1) * L, lane0 + h:lane0 + h + 1]

    bc = {ch: rows_of(bcol, ch, N_HEADS) for ch in chains}
    ic = {ch: rows_of(gl, ch, 0) for ch in chains}
    g_tot = {(ci, h): gtot[ci * L:ci * L + 1, N_HEADS + h:N_HEADS + h + 1] for ci, h in chains}
    br = {(ci, h): brow[N_HEADS + h:N_HEADS + h + 1, ci * L:(ci + 1) * L] for ci, h in chains}
    ir = {(ci, h): brow[h:h + 1, R + ci * L:R + (ci + 1) * L] for ci, h in chains}
    q_f = {ch: cut(q, ch) for ch in chains}
    k_f = {ch: cut(k, ch) for ch in chains}
    q_b = {ch: q_f[ch].astype(BF16) for ch in chains}
    k_b = {ch: k_f[ch].astype(BF16) for ch in chains}
    v_f = {ch: cut(v, ch) for ch in chains}
    v_b = {ch: v_f[ch].astype(BF16) for ch in chains}
    log_d = {ch: jnp.where(causal, bc[ch] - br[ch] + ir[ch], -jnp.inf) for ch in chains}
    m_loc = {ch: jnp.max(log_d[ch], -1, keepdims=True) for ch in chains}
    s0 = {ch: _dot(q_b[ch], k_b[ch], NT) * jnp.exp(log_d[ch] - m_loc[ch]) for ch in chains}
    s_sum = {ch: jnp.sum(s0[ch], -1, keepdims=True) for ch in chains}
    sv = {ch: _dot(s0[ch].astype(BF16), v_b[ch]) for ch in chains}
    logw = {ch: g_tot[ch] - bc[ch] + ic[ch] for ch in chains}
    m_w = {ch: jnp.max(logw[ch], 0, keepdims=True) for ch in chains}
    w_loc = {ch: jnp.exp(logw[ch] - m_w[ch]) for ch in chains}
    kv = {ch: _dot((v_f[ch] * w_loc[ch]).astype(BF16), k_b[ch], TN) for ch in chains}
    n_loc = {ch: jnp.sum(k_f[ch] * w_loc[ch], 0, keepdims=True) for ch in chains}

    bh = [(b, h) for b in range(bb) for h in heads]
    m_vec = {b: m_sc[b] for b in range(bb)}
    m = {(b, h): m_vec[b][:, h:h + 1] for b, h in bh}
    c = {k_: c_sc[k_[0], k_[1]] for k_ in bh}
    n = {(b, h): n_sc[b, h:h + 1, :] for b, h in bh}
    h_chunk = {}
    for cc in range(nc):
        ch_of = {(b, h): (b * nc + cc, h) for b, h in bh}
        qc = {k_: _dot(q_b[ch_of[k_]], c[k_].astype(BF16), NT) for k_ in bh}
        qn = {k_: jnp.sum(q_f[ch_of[k_]] * n[k_], -1, keepdims=True) for k_ in bh}
        inter = {k_: bc[ch_of[k_]] + m[k_] for k_ in bh}
        m_row = {k_: jnp.maximum(m_loc[ch_of[k_]], inter[k_]) for k_ in bh}
        m_new = {k_: jnp.maximum(g_tot[ch_of[k_]] + m[k_], m_w[ch_of[k_]]) for k_ in bh}
        dec = {k_: jnp.exp(g_tot[ch_of[k_]] + m[k_] - m_new[k_]) for k_ in bh}
        f2 = {k_: jnp.exp(m_w[ch_of[k_]] - m_new[k_]) for k_ in bh}
        c = {k_: dec[k_] * c[k_] + f2[k_] * kv[ch_of[k_]] for k_ in bh}
        n = {k_: dec[k_] * n[k_] + f2[k_] * n_loc[ch_of[k_]] for k_ in bh}
        m = m_new
        f1 = {k_: jnp.exp(m_loc[ch_of[k_]] - m_row[k_]) for k_ in bh}
        w_int = {k_: jnp.exp(inter[k_] - m_row[k_]) for k_ in bh}
        num = {k_: f1[k_] * sv[ch_of[k_]] + w_int[k_] * qc[k_] for k_ in bh}
        den = {k_: f1[k_] * s_sum[ch_of[k_]] + w_int[k_] * qn[k_] for k_ in bh}
        hh = {k_: num[k_] / jnp.maximum(jnp.abs(den[k_]), jnp.exp(-m_row[k_])) for k_ in bh}
        for b in range(bb):
            h_chunk[b * nc + cc] = jnp.concatenate([hh[b, h] for h in heads], axis=-1)
    for b in range(bb):
        m_out = m_vec[b]
        for h in heads:
            c_sc[b, h] = c[b, h]
            n_sc[b, h:h + 1, :] = n[b, h]
            m_out = jnp.where(lane == h, m[b, h], m_out)
        m_sc[b] = m_out

    hs = [h_chunk[ci] for ci in range(bb * nc)]
    hcat = jnp.concatenate(hs, axis=0) if len(hs) > 1 else hs[0]
    bo = _block_ones()
    mu_h = _head_sum1(hcat, bo) * (1.0 / HD)
    hc = hcat - mu_h
    var = _head_sum1(hc * hc, bo) * (1.0 / HD)
    y_ref[...] = (hc * lax.rsqrt(var + ML_NORM_EPS) * ng_ref[...] * _sigmoid(o)).reshape(bb, TT, G)

    @pl.when(j == pl.num_programs(1) - 1)
    def _():
        c_ref[...] = c_sc[...]
        n_ref[...] = n_sc[...]
        m_ref[...] = m_sc[...]


def _mlstm_call(u_ml, gates, conv0, c0, n0, m0, cw, cbias, gbias, ng):
    B, T, _ = u_ml.shape
    L = math.gcd(T, ML_CHUNK)
    bb = math.gcd(B, ML_CHUNKS_PER_STEP)
    nc = math.gcd(T // L, ML_CHUNKS_PER_STEP // bb)
    TT = nc * L
    full = lambda a: pl.BlockSpec(a.shape, lambda b, j: (0,) * a.ndim)
    st4 = pl.BlockSpec((bb, N_HEADS, HEAD_DIM, HEAD_DIM), lambda b, j: (b, 0, 0, 0))
    st3 = pl.BlockSpec((bb, N_HEADS, HEAD_DIM), lambda b, j: (b, 0, 0))
    stm = pl.BlockSpec((bb, 1, GATE_PAD), lambda b, j: (b, 0, 0))
    return pl.pallas_call(
        functools.partial(_mlstm_kernel, L=L, bb=bb, nc=nc),
        grid=(B // bb, T // TT),
        in_specs=[pl.BlockSpec((bb, TT, ML_MAIN), lambda b, j: (b, j, 0)),
                  pl.BlockSpec((bb, TT, GATE_PAD), lambda b, j: (b, j, 0)),
                  pl.BlockSpec((bb, 3, 2 * GROUP_W), lambda b, j: (b, 0, 0)),
                  st4, st3, stm, full(cw), full(cbias), full(gbias), full(ng)],
        out_specs=[pl.BlockSpec((bb, TT, GROUP_W), lambda b, j: (b, j, 0)), st4, st3, stm],
        out_shape=[jax.ShapeDtypeStruct((B, T, GROUP_W), F32),
                   jax.ShapeDtypeStruct((B, N_HEADS, HEAD_DIM, HEAD_DIM), F32),
                   jax.ShapeDtypeStruct((B, N_HEADS, HEAD_DIM), F32),
                   jax.ShapeDtypeStruct((B, 1, GATE_PAD), F32)],
        scratch_shapes=[pltpu.VMEM((bb, 8 + TT, 2 * GROUP_W), F32),
                        pltpu.VMEM((bb, N_HEADS, HEAD_DIM, HEAD_DIM), F32),
                        pltpu.VMEM((bb, N_HEADS, HEAD_DIM), F32),
                        pltpu.VMEM((bb, 1, GATE_PAD), F32)],
        compiler_params=_params(("arbitrary", "arbitrary")),
        name="mlstm",
    )(u_ml, gates, conv0, c0, n0, m0, cw, cbias, gbias, ng)


def _pool_kernel(u_ref, hist_ref, w_ref, scale_ref, y_ref, zbuf, *, pos0):
    j = pl.program_id(1)
    bb, tt, G = u_ref.shape
    gc = G // len(POOL_WINDOWS)

    @pl.when(j == 0)
    def _():
        zbuf[:, 0:1, :] = jnp.zeros((bb, 1, G), F32)
        zbuf[:, 1:16, :] = hist_ref[...]

    u = u_ref[...]
    zbuf[:, 16:16 + tt, :] = u
    lane = _iota((1, 1, G), 2)
    pos = pos0 + j * tt + _iota((1, tt, 1), 1)
    acc = jnp.zeros((bb, tt, G), F32)
    pooled = jnp.zeros((bb, tt, G), F32)
    for t in range(max(POOL_WINDOWS)):
        acc = acc + zbuf[:, 16 - t:16 - t + tt, :]
        if (t + 1) in POOL_WINDOWS:
            gi = POOL_WINDOWS.index(t + 1)
            cnt = jnp.minimum(pos + 1, t + 1).astype(F32)
            pooled = jnp.where(lane // gc == gi, acc / cnt, pooled)
    tail = zbuf[:, tt:tt + 16, :]
    zbuf[:, 0:16, :] = tail
    pooled = (pooled - u).reshape(bb * tt, G)
    y_ref[...] = (_bdot(pooled, w_ref[...]) * scale_ref[...]).reshape(bb, tt, G)


def _pool_call(u_pool, hist, wbd, scale, pos0):
    B, T, G = u_pool.shape
    bb, tt = _row_blocks(B, T)
    return pl.pallas_call(
        functools.partial(_pool_kernel, pos0=pos0),
        grid=(B // bb, T // tt),
        in_specs=[pl.BlockSpec((bb, tt, G), lambda b, j: (b, j, 0)),
                  pl.BlockSpec((bb, POOL_HIST, G), lambda b, j: (b, 0, 0)),
                  pl.BlockSpec((G, G), lambda b, j: (0, 0)),
                  pl.BlockSpec((1, G), lambda b, j: (0, 0))],
        out_specs=pl.BlockSpec((bb, tt, G), lambda b, j: (b, j, 0)),
        out_shape=jax.ShapeDtypeStruct((B, T, G), F32),
        scratch_shapes=[pltpu.VMEM((bb, 16 + tt, G), F32)],
        compiler_params=_params(("arbitrary", "arbitrary")),
        name="pool",
    )(u_pool, hist, wbd, scale)


def _slope_of(head):
    return jnp.where(head == 0, 2.0 ** -2, jnp.where(head == 1, 2.0 ** -4, jnp.where(head == 2, 2.0 ** -6, 2.0 ** -8)))


def _topk_mask(gate, valid, nblk):
    blk = _iota((1, nblk), 1)
    cnt = jnp.zeros(gate.shape, F32)
    for n in range(nblk):
        gn = gate[:, n:n + 1]
        ahead = (gn > gate) | ((gn == gate) & (n < blk))
        cnt = cnt + jnp.where(ahead, 1.0, 0.0)
    return valid & (cnt < MOBA_TOPK)


def _moba_p_kernel(q_ref, k_ref, vt_ref, o_ref, mean_sc, kb_sc, vt_sc, sel_sc, *, nblk):
    i = pl.program_id(1)
    BLK, G, HD, H = MOBA_BLOCK, GROUP_W, HEAD_DIM, N_HEADS
    W = H * BLK

    SLAB = 2 * HD
    lane_s = _iota((1, SLAB), 1)

    def slab_of(x, h, extra):
        xs = x[:, (h // 2) * SLAB:(h // 2 + 1) * SLAB]
        own = (lane_s // HD) == (h % 2)
        spare = HD if h % 2 == 0 else 0
        return jnp.where(own, xs, jnp.where(lane_s == spare, extra, 0.0))

    @pl.when(i == 0)
    def _():
        pos = _iota((BLK, 1), 0).astype(F32)
        for n in range(nblk):
            kblk = k_ref[0, n * BLK:(n + 1) * BLK, :]
            mean_sc[n:n + 1, :] = jnp.mean(kblk, axis=0, keepdims=True)
            for h in range(H):
                kb_sc[n, h] = slab_of(kblk, h, pos).astype(BF16)
            vt_sc[n] = vt_ref[0, :, n * BLK:(n + 1) * BLK].astype(BF16)

    q = q_ref[0]
    lane_head = _iota((1, G), 1) // HD
    qbd = jnp.concatenate([jnp.where(lane_head == h, q, 0.0) for h in range(H)], axis=0)
    slopes = [2.0 ** (-2.0 * (h + 1)) for h in range(H)]
    scale = HD ** -0.5
    qs = [slab_of(q * scale, h, slopes[h]).astype(BF16) for h in range(H)]

    blk_row = _iota((nblk, 1), 0)
    valid = blk_row < i
    gate = jnp.where(valid, _dot3(mean_sc[...], qbd, NT), -jnp.inf)
    cnt = jnp.zeros((nblk, W), F32)
    for n in range(nblk):
        gn = gate[n:n + 1, :]
        ahead = (gn > gate) | ((gn == gate) & (n < blk_row))
        cnt = cnt + jnp.where(ahead, 1.0, 0.0)
    sel = jnp.where(valid & (cnt < MOBA_TOPK), 1.0, 0.0)
    for n in range(nblk):
        sel_sc[n] = sel[n:n + 1, :]

    heads = range(H)

    def scores(jb):
        return [_dot(kb_sc[jb, h], qs[h], NT) for h in heads]

    def pv(jb, p):
        vt = vt_sc[jb]
        return [_dot(vt[h * HD:(h + 1) * HD, :], p[h].astype(BF16)) for h in heads]

    causal = _iota((BLK, BLK), 0) <= _iota((BLK, BLK), 1)
    s = [jnp.where(causal, x, -jnp.inf) for x in scores(i)]
    m0 = [jnp.max(x, 0, keepdims=True) for x in s]
    p = [jnp.exp(s[h] - m0[h]) for h in heads]
    l0 = [jnp.sum(x, 0, keepdims=True) for x in p]
    acc0 = pv(i, p)

    def body(jb, carry):
        m, l, acc = carry
        s_raw = scores(jb)
        off = ((i - jb) * BLK).astype(F32)
        add = [jnp.where(sel_sc[jb, :, h * BLK:(h + 1) * BLK] > 0.0, -slopes[h] * off, -jnp.inf) for h in heads]
        s = [s_raw[h] + add[h] for h in heads]
        m2 = [jnp.maximum(m[h], jnp.max(s[h], 0, keepdims=True)) for h in heads]
        alpha = [jnp.exp(m[h] - m2[h]) for h in heads]
        p = [jnp.exp(s[h] - m2[h]) for h in heads]
        new = pv(jb, p)
        l = [alpha[h] * l[h] + jnp.sum(p[h], 0, keepdims=True) for h in heads]
        acc = [alpha[h] * acc[h] + new[h] for h in heads]
        return m2, l, acc

    m, l, acc = lax.fori_loop(0, i, body, (m0, l0, acc0))
    out_t = jnp.concatenate([acc[h] / l[h] for h in heads], axis=0)
    o_ref[0] = out_t.T


def _moba_prompt_call(q, k, v_t):
    B, T, G = q.shape
    assert T % MOBA_BLOCK == 0
    nblk = T // MOBA_BLOCK
    return pl.pallas_call(
        functools.partial(_moba_p_kernel, nblk=nblk),
        grid=(B, nblk),
        in_specs=[pl.BlockSpec((1, MOBA_BLOCK, G), lambda b, i: (b, i, 0)),
                  pl.BlockSpec((1, T, G), lambda b, i: (b, 0, 0)),
                  pl.BlockSpec((1, G, T), lambda b, i: (b, 0, 0))],
        out_specs=pl.BlockSpec((1, MOBA_BLOCK, G), lambda b, i: (b, i, 0)),
        out_shape=jax.ShapeDtypeStruct((B, T, G), F32),
        scratch_shapes=[pltpu.VMEM((nblk, G), F32),
                        pltpu.VMEM((nblk, N_HEADS, MOBA_BLOCK, 2 * HEAD_DIM), BF16),
                        pltpu.VMEM((nblk, G, MOBA_BLOCK), BF16),
                        pltpu.VMEM((nblk, 1, N_HEADS * MOBA_BLOCK), F32)],
        compiler_params=_params(("arbitrary", "arbitrary")),
        name="moba_prompt",
    )(q, k, v_t)


def _moba_s_kernel(pt_ref, q_ref, kn_ref, vn_ref, *refs, n_pages, page, past_len, rb):
    kp = refs[:rb * n_pages]
    vp = refs[rb * n_pages:2 * rb * n_pages]
    o_ref = refs[2 * rb * n_pages]
    G, HD = GROUP_W, HEAD_DIM
    ts = q_ref.shape[1]
    R = N_HEADS * ts
    ppb = MOBA_BLOCK // page
    nb = n_pages // ppb
    rows = range(rb)
    pages = range(n_pages)
    lane_head = _iota((1, G), 1) // HD
    row_head = _iota((R, 1), 0) // ts
    row_t = _iota((R, 1), 0) % ts
    slope = _slope_of(row_head)
    scale = HD ** -0.5
    qbd = [jnp.concatenate([jnp.where(lane_head == h, q_ref[r], 0.0) for h in range(N_HEADS)], axis=0)
           for r in rows]
    qb = [x.astype(BF16) for x in qbd]

    kpg = {(r, p): kp[r * n_pages + p][...] for r in rows for p in pages}
    raw = {(r, p): _dot(qb[r], kpg[r, p].astype(BF16)) for r in rows for p in pages}
    sums = {(r, p): jnp.sum(kpg[r, p], axis=1, keepdims=True) for r in rows for p in pages}
    means_t = [jnp.concatenate(
        [sum(sums[r, p] for p in range(n * ppb, (n + 1) * ppb)) * (1.0 / MOBA_BLOCK) for n in range(nb)], axis=1)
        for r in rows]
    gate = [_dot3(qbd[r], means_t[r]) for r in rows]
    all_valid = _iota((1, nb), 1) >= 0
    sel = [jnp.where(_topk_mask(gate[r], all_valid, nb), 1.0, 0.0) for r in rows]

    q_pos = (past_len + row_t).astype(F32)
    off = _iota((1, page), 1).astype(F32)
    scores = {(r, p): jnp.where(sel[r][:, p // ppb:p // ppb + 1] > 0.0,
                                raw[r, p] * scale - slope * (q_pos - (p * page + off)), -jnp.inf)
              for r in rows for p in pages}
    t_new = _iota((1, ts), 1)
    s_own = [jnp.where(t_new <= row_t,
                       _dot(qb[r], kn_ref[r].astype(BF16), NT) * scale - slope * (row_t - t_new).astype(F32),
                       -jnp.inf) for r in rows]

    m = [jnp.max(s_own[r], -1, keepdims=True) for r in rows]
    for p in pages:
        m = [jnp.maximum(m[r], jnp.max(scores[r, p], -1, keepdims=True)) for r in rows]
    p_own = [jnp.exp(s_own[r] - m[r]) for r in rows]
    l = [jnp.sum(p_own[r], -1, keepdims=True) for r in rows]
    acc = [_dot(p_own[r].astype(BF16), vn_ref[r].astype(BF16)) for r in rows]
    for p in pages:
        pr = [jnp.exp(scores[r, p] - m[r]) for r in rows]
        l = [l[r] + jnp.sum(pr[r], -1, keepdims=True) for r in rows]
        acc = [acc[r] + _dot(pr[r].astype(BF16), vp[r * n_pages + p][...].astype(BF16), NT) for r in rows]
    for r in rows:
        out = acc[r] / l[r]
        y = jnp.zeros((ts, G), F32)
        for h in range(N_HEADS):
            y = jnp.where(lane_head == h, out[h * ts:(h + 1) * ts, :], y)
        o_ref[r] = y


def _moba_sample_call(q, k, v, cache_k, cache_v, page_table, layer):
    DB, TS, G = q.shape
    Ld, n_phys, page, H, d = cache_k.shape
    n_pages = page_table.shape[1]
    past_len = n_pages * page
    assert past_len % MOBA_BLOCK == 0 and MOBA_BLOCK % page == 0 and past_len // MOBA_BLOCK >= MOBA_TOPK
    ck = jnp.transpose(cache_k, (0, 1, 3, 4, 2)).reshape(Ld, n_phys, H * d, page)
    cv = jnp.transpose(cache_v, (0, 1, 3, 4, 2)).reshape(Ld, n_phys, H * d, page)
    rb = math.gcd(DB, MOBA_SAMPLE_ROWS_PER_STEP)
    new = pl.BlockSpec((rb, TS, G), lambda b, pt: (b, 0, 0))
    pg = [pl.BlockSpec((None, None, G, page),
                       functools.partial(lambda b, pt, r, p: (layer, pt[b * rb + r, p], 0, 0), r=r, p=p))
          for r in range(rb) for p in range(n_pages)]
    return pl.pallas_call(
        functools.partial(_moba_s_kernel, n_pages=n_pages, page=page, past_len=past_len, rb=rb),
        grid_spec=pltpu.PrefetchScalarGridSpec(
            num_scalar_prefetch=1, grid=(DB // rb,),
            in_specs=[new, new, new] + pg + pg,
            out_specs=new),
        out_shape=jax.ShapeDtypeStruct((DB, TS, G), F32),
        compiler_params=_params(("arbitrary",)),
        name="moba_sample",
    )(page_table, q, k, v, *([ck] * (rb * n_pages)), *([cv] * (rb * n_pages)))


def _out_kernel(yrw_ref, yml_ref, ypl_ref, yat_ref, x_ref, g1_ref, sc2_ref, sh2_ref, g2_ref, n2g_ref, fng_ref,
                wout_ref, wup_ref, wdn_ref, o_ref, h2_sc, acc_sc, *, final):
    f = pl.program_id(2)
    bb, tt, D = x_ref.shape
    M = bb * tt

    @pl.when(f == 0)
    def _():
        ycat = jnp.concatenate([r[...].reshape(M, GROUP_W) for r in (yrw_ref, yml_ref, ypl_ref, yat_ref)], axis=-1)
        y = _dot(ycat.astype(BF16), wout_ref[...]).reshape(bb, tt, D)
        x1 = x_ref[...] + g1_ref[...] * y
        o_ref[...] = x1
        hn = x1 * lax.rsqrt(jnp.mean(x1 * x1, -1, keepdims=True) + NORM_EPS) * n2g_ref[...]
        h2_sc[...] = (hn * (1.0 + sc2_ref[...]) + sh2_ref[...]).reshape(M, D).astype(BF16)
        acc_sc[...] = jnp.zeros((M, D), F32)

    a = _dot(h2_sc[...], wup_ref[...])
    a = jnp.square(jnp.maximum(a, 0.0)).astype(BF16)
    acc_sc[...] += _dot(a, wdn_ref[...])

    @pl.when(f == pl.num_programs(2) - 1)
    def _():
        x2 = o_ref[...] + g2_ref[...] * acc_sc[...].reshape(bb, tt, D)
        if final:
            x2 = x2 * lax.rsqrt(jnp.mean(x2 * x2, -1, keepdims=True) + NORM_EPS) * fng_ref[...]
        o_ref[...] = x2


def _out_call(ys, x, g1, sc2, sh2, g2, n2g, fng, wout, wup, wdn, final):
    B, T, D = x.shape
    bb, tt = _row_blocks(B, T, OUT_ROW_TILE)
    FF = wup.shape[1]
    row = lambda n: pl.BlockSpec((bb, tt, n), lambda b, j, f: (b, j, 0))
    mod = pl.BlockSpec((bb, 1, D), lambda b, j, f: (b, 0, 0))
    vec = pl.BlockSpec((1, D), lambda b, j, f: (0, 0))
    return pl.pallas_call(
        functools.partial(_out_kernel, final=final),
        grid=(B // bb, T // tt, FF // FF_TILE),
        in_specs=[row(GROUP_W)] * 4 + [row(D), mod, mod, mod, mod, vec, vec,
                  pl.BlockSpec((D, D), lambda b, j, f: (0, 0)),
                  pl.BlockSpec((D, FF_TILE), lambda b, j, f: (0, f)),
                  pl.BlockSpec((FF_TILE, D), lambda b, j, f: (f, 0))],
        out_specs=row(D),
        out_shape=jax.ShapeDtypeStruct((B, T, D), F32),
        scratch_shapes=[pltpu.VMEM((bb * tt, D), BF16),
                        pltpu.VMEM((bb * tt, D), F32)],
        compiler_params=_params(("arbitrary", "arbitrary", "arbitrary"), OUT_VMEM_LIMIT),
        name="out_mlp",
    )(*ys, x, g1, sc2, sh2, g2, n2g, fng, wout, wup, wdn)


def _rw_perm(x):
    return jnp.concatenate([x[..., 0:256], x[..., 320:576], x[..., 576:832],
                            x[..., 256:320], x[..., 832:896], x[..., 896:1024]], axis=-1)


def _rw_unperm(x):
    return jnp.concatenate([x[..., 0:256], x[..., 768:832], x[..., 256:512],
                            x[..., 512:768], x[..., 832:896], x[..., 896:1024]], axis=-1)


def _layer(x, mods, st, pos0, attend, lw, final, transposed_kv):
    sh1, sc1, g1, sh2, sc2, g2 = mods
    shift0, wkv0, conv0, c0, n0, m0, pool0 = st
    B, T, D = x.shape
    outs = _in_call(x, sc1, sh1, lw['norm1_g'], lw['wpack'], lw['wkv_t'], transposed_kv)
    u_rw, u_ml, gates, u_pool, q, k = outs[:6]

    y_rw, wkv1 = _rwkv_call(u_rw, _rw_perm(shift0)[:, None, :], wkv0, lw['rw_mu'], lw['rw_vec'],
                            lw['rwkv_w_up'], lw['rwkv_a_up'], lw['rwkv_g_up'])
    shift1 = _rw_unperm(u_rw[:, -1, :])

    m0p = jnp.pad(m0, ((0, 0), (0, GATE_PAD - N_HEADS)))[:, None, :]
    y_ml, c1, n1, m1p = _mlstm_call(u_ml, gates, conv0, c0, n0, m0p, lw['mlstm_conv_w'], lw['ml_cbias'],
                                    lw['ml_gbias'], lw['ml_ng'])
    m1 = m1p[:, 0, :N_HEADS]
    zc = u_ml[:, :, :2 * GROUP_W] if T >= 3 else jnp.concatenate([conv0, u_ml[:, :, :2 * GROUP_W]], 1)
    conv1 = zc[:, -3:]

    y_pool = _pool_call(u_pool, pool0, lw['pool_wbd'], lw['pool_scale'], pos0)
    zp = u_pool if T >= POOL_HIST else jnp.concatenate([pool0, u_pool], 1)
    pool1 = zp[:, -POOL_HIST:]

    if transposed_kv:
        k_out, v_out = outs[6], outs[7]
        y_at = attend(q, k, v_out)
    else:
        kv_shape = (B, T, N_HEADS, HEAD_DIM)
        y_at = attend(q, k, outs[6])
        k_out, v_out = k.reshape(kv_shape), outs[6].reshape(kv_shape)

    x_new = _out_call((y_rw, y_ml, y_pool, y_at), x, g1, sc2, sh2, g2, lw['norm2_g'], lw['final_g'],
                      lw['w_out'], lw['mlp_up'], lw['mlp_down'], final)
    return x_new, (shift1, wkv1, conv1, c1, n1, m1, pool1, k_out, v_out)


def kernel(x_prompt, x_sample, c_prompt, c_sample, state_rwkv_shift, state_rwkv_wkv, state_mlstm_conv, state_mlstm_c, state_mlstm_n, state_mlstm_m, state_pool, cache_k, cache_v, page_table, ada_w, ada_b, norm1_g, norm2_g, w_in, w_out, rwkv_mu, rwkv_w0, rwkv_w_up, rwkv_a0, rwkv_a_up, rwkv_g_up, rwkv_k_k, rwkv_k_a, rwkv_r_k, rwkv_ln_g, rwkv_ln_b, mlstm_conv_w, mlstm_conv_b, mlstm_i_b, mlstm_f_b, mlstm_norm_g, pool_w, pool_scale, mlp_up, mlp_down, final_norm_g):
    B, T, D = x_prompt.shape
    DB = x_sample.shape[0]
    depth = ada_w.shape[0]
    G = GROUP_W
    assert D == D_MODEL and w_in.shape[-1] == N_IN

    mod = _ada_call(jnp.concatenate([c_prompt, c_sample], 0), ada_w, ada_b)

    st_p0 = (jnp.zeros((B, RW_COLS), F32), jnp.zeros((B, N_HEADS, HEAD_DIM, HEAD_DIM), F32),
             jnp.zeros((B, 3, 2 * G), F32), jnp.zeros((B, N_HEADS, HEAD_DIM, HEAD_DIM), F32),
             jnp.zeros((B, N_HEADS, HEAD_DIM), F32), jnp.zeros((B, N_HEADS), F32),
             jnp.zeros((B, POOL_HIST, G), F32))
    past_len = page_table.shape[1] * cache_k.shape[2]

    xp, xs = x_prompt, x_sample
    new_p, new_s = [], []
    for l in range(depth):
        wl = w_in[l]
        wpack = jnp.concatenate([
            _rw_perm(wl[:, :RW_COLS]),
            wl[:, RW_COLS:RW_COLS + ML_MAIN],
            jnp.pad(wl[:, RW_COLS + ML_MAIN:RW_COLS + ML_MAIN + ML_GATES], ((0, 0), (0, GATE_PAD - ML_GATES))),
            wl[:, RW_COLS + ML_MAIN + ML_GATES:],
        ], axis=1).astype(BF16)
        pool_wbd = jnp.zeros((G, G), F32)
        gcw = G // len(POOL_WINDOWS)
        for gi in range(len(POOL_WINDOWS)):
            pool_wbd = pool_wbd.at[gi * gcw:(gi + 1) * gcw, gi * gcw:(gi + 1) * gcw].set(pool_w[l, gi])
        row = lambda a: a.reshape(1, -1)
        lw = {
            'norm1_g': row(norm1_g[l]), 'norm2_g': row(norm2_g[l]), 'final_g': row(final_norm_g),
            'wpack': wpack, 'wkv_t': wl[:, N_IN - 2 * G:].T.astype(BF16), 'w_out': w_out[l].astype(BF16),
            'mlp_up': mlp_up[l].astype(BF16), 'mlp_down': mlp_down[l].astype(BF16),
            'rw_mu': row(_rw_perm(rwkv_mu[l])),
            'rw_vec': jnp.stack([rwkv_w0[l], rwkv_a0[l], rwkv_k_k[l], rwkv_k_a[l], rwkv_ln_g[l], rwkv_ln_b[l],
                                 rwkv_r_k[l].reshape(-1), jnp.zeros((G,), F32)]),
            'rwkv_w_up': rwkv_w_up[l], 'rwkv_a_up': rwkv_a_up[l], 'rwkv_g_up': rwkv_g_up[l],
            'mlstm_conv_w': mlstm_conv_w[l], 'ml_cbias': row(mlstm_conv_b[l]),
            'ml_gbias': row(jnp.pad(jnp.concatenate([mlstm_i_b[l], mlstm_f_b[l]]), (0, GATE_PAD - ML_GATES))),
            'ml_ng': row(mlstm_norm_g[l]),
            'pool_wbd': pool_wbd, 'pool_scale': row(pool_scale[l]),
        }
        mods = [mod[l, :, i * D:(i + 1) * D][:, None, :] for i in range(6)]
        mods_p = [m[:B] for m in mods]
        mods_s = [m[B:] for m in mods]
        final = l == depth - 1

        xp, st_p = _layer(xp, mods_p, st_p0, 0, _moba_prompt_call, lw, final, True)
        st_s_in = (state_rwkv_shift[l], state_rwkv_wkv[l], state_mlstm_conv[l], state_mlstm_c[l],
                   state_mlstm_n[l], state_mlstm_m[l], state_pool[l])
        attend_s = functools.partial(_moba_sample_call, cache_k=cache_k, cache_v=cache_v,
                                     page_table=page_table, layer=l)
        xs, st_s = _layer(xs, mods_s, st_s_in, past_len, attend_s, lw, final, False)
        new_p.append(st_p)
        new_s.append(st_s)

    stack = lambda lst: tuple(jnp.stack([st[i] for st in lst]) for i in range(9))
    out_p = list(stack(new_p))
    for i in (7, 8):
        out_p[i] = jnp.transpose(out_p[i].reshape(depth, B, N_HEADS, HEAD_DIM, T), (0, 1, 4, 2, 3))
    return (xp, xs) + tuple(out_p) + stack(new_s)
```

```python
import functools
import math

import jax
import jax.numpy as jnp
from jax import lax
from jax.experimental import pallas as pl
from jax.experimental.pallas import tpu as pltpu

F32 = jnp.float32
BF16 = jnp.bfloat16

D_MODEL = 1024
GROUP_W = 256
HEAD_DIM = 64
N_HEADS = 4
RW_COLS = 1024
ML_MAIN = 1024
ML_GATES = 8
GATE_PAD = 128
N_IN = 3080
MOBA_BLOCK = 256
MOBA_TOPK = 3
ML_CHUNK = 256
RW_CHUNK = 64
RW_CHUNKS_PER_STEP = 4
ML_CHUNKS_PER_STEP = 4
MOBA_SAMPLE_ROWS_PER_STEP = 4
POOL_WINDOWS = (2, 4, 8, 16)
POOL_HIST = 15
NORM_EPS = 1e-6
RW_GN_EPS = 64e-5
ML_NORM_EPS = 1e-6
ROW_TILE = 512
FF_TILE = 1024
VMEM_LIMIT = 48 * 1024 * 1024
OUT_ROW_TILE = 1024
OUT_VMEM_LIMIT = 60 * 1024 * 1024

NN = (((1,), (0,)), ((), ()))
NT = (((1,), (1,)), ((), ()))
TN = (((0,), (0,)), ((), ()))


def _dot(a, b, dn=NN):
    return lax.dot_general(a, b, dn, preferred_element_type=F32)


def _bdot(a, b, dn=NN):
    return _dot(a.astype(BF16), b.astype(BF16), dn)


def _split2(x):
    hi = x.astype(BF16)
    lo = (x - hi.astype(F32)).astype(BF16)
    return hi, lo


def _split3(x):
    hi = x.astype(BF16)
    r = x - hi.astype(F32)
    mid = r.astype(BF16)
    lo = (r - mid.astype(F32)).astype(BF16)
    return hi, mid, lo


def _dot3(a, b, dn=NN):
    ah, al = _split2(a)
    bh, bl = _split2(b)
    return _dot(ah, bh, dn) + (_dot(ah, bl, dn) + _dot(al, bh, dn))


def _dot3s(a_s, b_s, dn=NN):
    return _dot(a_s[0], b_s[0], dn) + (_dot(a_s[0], b_s[1], dn) + _dot(a_s[1], b_s[0], dn))


def _dot_exact_rhs(a, b01, dn=NN):
    h, m, l = _split3(a)
    b = b01.astype(BF16)
    return _dot(h, b, dn) + (_dot(m, b, dn) + _dot(l, b, dn))


def _dot_exact_lhs(a01, b, dn=NN):
    h, m, l = _split3(b)
    a = a01.astype(BF16)
    return _dot(a, h, dn) + (_dot(a, m, dn) + _dot(a, l, dn))


def _iota(shape, dim):
    return lax.broadcasted_iota(jnp.int32, shape, dim)


def _sigmoid(x):
    return jax.nn.sigmoid(x)


def _softplus(x):
    return jnp.maximum(x, 0.0) + jnp.log(1.0 + jnp.exp(-jnp.abs(x)))


def _block_ones():
    return (_iota((GROUP_W, GROUP_W), 0) // HEAD_DIM == _iota((GROUP_W, GROUP_W), 1) // HEAD_DIM).astype(F32)


def _head_sum(x, bo):
    return _dot_exact_rhs(x, bo)


def _head_sum1(x, bo):
    return _bdot(x, bo)


def _params(sem, vmem_limit=VMEM_LIMIT):
    return pltpu.CompilerParams(dimension_semantics=sem, vmem_limit_bytes=vmem_limit)


def _row_blocks(B, T, tile=ROW_TILE):
    if T >= tile:
        assert T % tile == 0
        return 1, tile
    bb = max(1, min(B, tile // T))
    while B % bb:
        bb -= 1
    return bb, T


def _ada_kernel(c_ref, w_ref, b_ref, o_ref):
    c = c_ref[...]
    o_ref[...] = _bdot(c * _sigmoid(c), w_ref[...]) + b_ref[...]


def _ada_call(c_all, ada_w, ada_b):
    Ld, D, N6 = ada_w.shape
    NB = c_all.shape[0]
    tn = 1024
    return pl.pallas_call(
        _ada_kernel,
        grid=(Ld, N6 // tn),
        in_specs=[pl.BlockSpec((NB, D), lambda l, j: (0, 0)),
                  pl.BlockSpec((None, D, tn), lambda l, j: (l, 0, j)),
                  pl.BlockSpec((None, 1, tn), lambda l, j: (l, 0, j))],
        out_specs=pl.BlockSpec((None, NB, tn), lambda l, j: (l, 0, j)),
        out_shape=jax.ShapeDtypeStruct((Ld, NB, N6), F32),
        compiler_params=_params(("arbitrary", "arbitrary")),
        name="ada_mod",
    )(c_all, ada_w, ada_b.reshape(Ld, 1, N6))


IN_WIDTHS = (RW_COLS, ML_MAIN, GATE_PAD, GROUP_W, GROUP_W, GROUP_W, GROUP_W)


def _in_kernel(x_ref, sc_ref, sh_ref, g_ref, w_ref, wt_ref, *out_refs, n_rows_out):
    x = x_ref[...]
    bb, tt, D = x.shape
    y = x * lax.rsqrt(jnp.mean(x * x, -1, keepdims=True) + NORM_EPS) * g_ref[...]
    h = y * (1.0 + sc_ref[...]) + sh_ref[...]
    hb = h.reshape(bb * tt, D).astype(BF16)
    off = 0
    for ref in out_refs[:n_rows_out]:
        n = ref.shape[-1]
        ref[...] = _dot(hb, w_ref[:, off:off + n]).reshape(bb, tt, n)
        off += n
    for i, ref in enumerate(out_refs[n_rows_out:]):
        ref[0] = _dot(wt_ref[i * GROUP_W:(i + 1) * GROUP_W, :], hb, NT)


def _in_call(x, sc, sh, g, wpack, wkv_t, transposed_kv):
    B, T, D = x.shape
    bb, tt = _row_blocks(B, T)
    row = lambda n: pl.BlockSpec((bb, tt, n), lambda b, j: (b, j, 0))
    mod = pl.BlockSpec((bb, 1, D), lambda b, j: (b, 0, 0))
    widths = IN_WIDTHS[:-1] if transposed_kv else IN_WIDTHS
    out_specs = [row(n) for n in widths]
    out_shape = [jax.ShapeDtypeStruct((B, T, n), F32) for n in widths]
    if transposed_kv:
        assert bb == 1
        out_specs += [pl.BlockSpec((1, GROUP_W, tt), lambda b, j: (b, 0, j))] * 2
        out_shape += [jax.ShapeDtypeStruct((B, GROUP_W, T), F32)] * 2
    return pl.pallas_call(
        functools.partial(_in_kernel, n_rows_out=len(widths)),
        grid=(B // bb, T // tt),
        in_specs=[row(D), mod, mod,
                  pl.BlockSpec((1, D), lambda b, j: (0, 0)),
                  pl.BlockSpec(wpack.shape, lambda b, j: (0, 0)),
                  pl.BlockSpec(wkv_t.shape, lambda b, j: (0, 0))],
        out_specs=out_specs,
        out_shape=out_shape,
        compiler_params=_params(("arbitrary", "arbitrary")),
        name="in_proj",
    )(x, sc, sh, g, wpack, wkv_t)


def _rwkv_kernel(u_ref, s0_ref, wkv0_ref, mu_ref, vec_ref, wup_ref, aup_ref, gup_ref,
                 y_ref, wkv_ref, zbuf, s_sc, *, L, bb, nc):
    j = pl.program_id(1)
    G, HD = GROUP_W, HEAD_DIM
    TT = nc * L
    R = bb * TT

    @pl.when(j == 0)
    def _():
        zbuf[:, 7:8, :] = s0_ref[...]
        s_sc[...] = wkv0_ref[...]

    u3 = u_ref[...]
    zbuf[:, 8:8 + TT, :] = u3
    prev = zbuf[:, 7:7 + TT, :]
    xs = (u3 + (prev - u3) * mu_ref[...]).reshape(R, RW_COLS)
    zbuf[:, 7:8, :] = u3[:, TT - 1:TT, :]

    r = xs[:, 0:G]
    k = xs[:, G:2 * G]
    v = xs[:, 2 * G:3 * G]
    wc = xs[:, 3 * G:3 * G + 64]
    ac = xs[:, 3 * G + 64:3 * G + 128]
    gc = xs[:, 3 * G + 128:4 * G]
    w0, a0, k_k, k_a = vec_ref[0:1, :], vec_ref[1:2, :], vec_ref[2:3, :], vec_ref[3:4, :]
    ln_g, ln_b, r_k = vec_ref[4:5, :], vec_ref[5:6, :], vec_ref[6:7, :]

    w_log = -_softplus(-(w0 + _bdot(jnp.tanh(wc), wup_ref[...]))) - 0.5
    lw = -jnp.exp(w_log)
    a = _sigmoid(a0 + _bdot(ac, aup_ref[...]))
    g = _bdot(_sigmoid(gc), gup_ref[...])
    bo = _block_ones()
    kk = k * k_k
    kk = kk / jnp.maximum(jnp.sqrt(_head_sum(kk * kk, bo)), 1e-12)
    k2 = k * (1.0 + (a - 1.0) * k_a)
    beta = kk * a

    rr = _iota((R, R), 0)
    rc = _iota((R, R), 1)
    same = (rr // L) == (rc // L)
    c = _dot_exact_lhs((same & (rr >= rc)).astype(F32), lw)
    cl = _dot_exact_lhs(same.astype(F32), lw)
    e_inv = jnp.exp(-c)
    e_tail = jnp.exp(cl - c)
    a_t = -kk * jnp.exp(c - lw)
    b_t = beta * e_inv
    k_t = k2 * e_inv
    r_t = r * jnp.exp(c)
    b_l = beta * e_tail
    k_l = k2 * e_tail
    e_cl = jnp.exp(cl)

    row = _iota((L, L), 0)
    col = _iota((L, L), 1)
    incl = row >= col
    strict = row > col
    eye = (row == col).astype(F32)
    n_double = max(1, int(math.ceil(math.log2(L)))) - 1

    chains = [(ci, h) for ci in range(bb * nc) for h in range(N_HEADS)]

    def cut(x, ch):
        ci, h = ch
        return x[ci * L:(ci + 1) * L, h * HD:(h + 1) * HD]

    a_s = {ch: _split2(cut(a_t, ch)) for ch in chains}
    b_s = {ch: _split2(cut(b_t, ch)) for ch in chains}
    r_b = {ch: cut(r_t, ch).astype(BF16) for ch in chains}
    k_b = {ch: cut(k_t, ch).astype(BF16) for ch in chains}
    v_b = {ch: cut(v, ch).astype(BF16) for ch in chains}
    n_ab = {ch: jnp.where(strict, _dot3s(a_s[ch], b_s[ch], NT), 0.0) for ch in chains}
    n_ak = {ch: jnp.where(strict, _dot(a_s[ch][0], k_b[ch], NT), 0.0).astype(BF16) for ch in chains}
    n_rb = {ch: jnp.where(incl, _dot(r_b[ch], b_s[ch][0], NT), 0.0).astype(BF16) for ch in chains}
    n_rk = {ch: jnp.where(incl, _dot(r_b[ch], k_b[ch], NT), 0.0).astype(BF16) for ch in chains}
    p = {ch: eye + n_ab[ch] for ch in chains}
    m_s = {ch: _split2(n_ab[ch]) for ch in chains}
    for _ in range(n_double):
        m_s = {ch: _split2(_dot3s(m_s[ch], m_s[ch])) for ch in chains}
        p = {ch: p[ch] + _dot3s(m_s[ch], _split2(p[ch])) for ch in chains}
    p_b = {ch: p[ch].astype(BF16) for ch in chains}
    akv = {ch: _dot(n_ak[ch], v_b[ch]).astype(BF16) for ch in chains}
    y_v = {ch: _dot(n_rk[ch], v_b[ch]) for ch in chains}
    w_b = {ch: _dot(p_b[ch], a_s[ch][0]).astype(BF16) for ch in chains}
    u_t = {ch: _dot(p_b[ch], akv[ch]) for ch in chains}
    bl_b = {ch: cut(b_l, ch).astype(BF16) for ch in chains}
    kl_b = {ch: cut(k_l, ch).astype(BF16) for ch in chains}

    heads = range(N_HEADS)
    bh = [(b, h) for b in range(bb) for h in heads]
    s = {k_: s_sc[k_[0], k_[1]] for k_ in bh}
    y_chunk = {}
    for cc in range(nc):
        ch_of = {(b, h): (b * nc + cc, h) for b, h in bh}
        s_b = {k_: s[k_].astype(BF16) for k_ in bh}
        e_b = {k_: (_dot(w_b[ch_of[k_]], s_b[k_], NT) + u_t[ch_of[k_]]).astype(BF16) for k_ in bh}
        s = {k_: s[k_] * cut(e_cl, ch_of[k_])[0:1, :]
             + _dot(e_b[k_], bl_b[ch_of[k_]], TN) + _dot(v_b[ch_of[k_]], kl_b[ch_of[k_]], TN) for k_ in bh}
        yh = {k_: _dot(r_b[ch_of[k_]], s_b[k_], NT) + _dot(n_rb[ch_of[k_]], e_b[k_]) + y_v[ch_of[k_]] for k_ in bh}
        for b in range(bb):
            y_chunk[b * nc + cc] = jnp.concatenate([yh[b, h] for h in heads], axis=-1)
    for k_ in bh:
        s_sc[k_[0], k_[1]] = s[k_]

    ys = [y_chunk[ci] for ci in range(bb * nc)]
    y = jnp.concatenate(ys, axis=0) if len(ys) > 1 else ys[0]
    mu_y = _head_sum1(y, bo) * (1.0 / HD)
    yc = y - mu_y
    var = _head_sum1(yc * yc, bo) * (1.0 / HD)
    yn = yc * lax.rsqrt(var + RW_GN_EPS) * ln_g + ln_b
    bonus = _head_sum1(r * k2 * r_k, bo) * v
    y_ref[...] = ((yn + bonus) * g).reshape(bb, TT, G)

    @pl.when(j == pl.num_programs(1) - 1)
    def _():
        wkv_ref[...] = s_sc[...]


def _rwkv_call(u_rw, shift0, wkv0, mu, vec, wup, aup, gup):
    B, T, _ = u_rw.shape
    L = math.gcd(T, RW_CHUNK)
    bb = math.gcd(B, RW_CHUNKS_PER_STEP)
    nc = math.gcd(T // L, RW_CHUNKS_PER_STEP // bb)
    TT = nc * L
    full = lambda a: pl.BlockSpec(a.shape, lambda b, j: (0,) * a.ndim)
    return pl.pallas_call(
        functools.partial(_rwkv_kernel, L=L, bb=bb, nc=nc),
        grid=(B // bb, T // TT),
        in_specs=[pl.BlockSpec((bb, TT, RW_COLS), lambda b, j: (b, j, 0)),
                  pl.BlockSpec((bb, 1, RW_COLS), lambda b, j: (b, 0, 0)),
                  pl.BlockSpec((bb, N_HEADS, HEAD_DIM, HEAD_DIM), lambda b, j: (b, 0, 0, 0)),
                  full(mu), full(vec), full(wup), full(aup), full(gup)],
        out_specs=[pl.BlockSpec((bb, TT, GROUP_W), lambda b, j: (b, j, 0)),
                   pl.BlockSpec((bb, N_HEADS, HEAD_DIM, HEAD_DIM), lambda b, j: (b, 0, 0, 0))],
        out_shape=[jax.ShapeDtypeStruct((B, T, GROUP_W), F32),
                   jax.ShapeDtypeStruct((B, N_HEADS, HEAD_DIM, HEAD_DIM), F32)],
        scratch_shapes=[pltpu.VMEM((bb, 8 + TT, RW_COLS), F32),
                        pltpu.VMEM((bb, N_HEADS, HEAD_DIM, HEAD_DIM), F32)],
        compiler_params=_params(("arbitrary", "arbitrary")),
        name="rwkv7",
    )(u_rw, shift0, wkv0, mu, vec, wup, aup, gup)


def _mlstm_kernel(u_ref, gt_ref, cb_ref, c0_ref, n0_ref, m0_ref, cw_ref, cbias_ref, gbias_ref, ng_ref,
                  y_ref, c_ref, n_ref, m_ref, zbuf, c_sc, n_sc, m_sc, *, L, bb, nc):
    j = pl.program_id(1)
    G, HD = GROUP_W, HEAD_DIM
    TT = nc * L
    R = bb * TT

    @pl.when(j == 0)
    def _():
        zbuf[:, 5:8, :] = cb_ref[...]
        c_sc[...] = c0_ref[...]
        n_sc[...] = n0_ref[...]
        m_sc[...] = m0_ref[...]

    u3 = u_ref[...]
    zbuf[:, 8:8 + TT, :] = u3[:, :, 0:2 * G]
    conv = cbias_ref[...]
    for t in range(4):
        conv = conv + zbuf[:, 5 + t:5 + t + TT, :] * cw_ref[t:t + 1, :]
    tail = zbuf[:, 5 + TT:8 + TT, :]
    zbuf[:, 5:8, :] = tail
    conv = conv.reshape(R, 2 * G)
    u = u3.reshape(R, ML_MAIN)
    sq = conv * _sigmoid(conv)
    q = sq[:, 0:G]
    k = sq[:, G:2 * G] * (HD ** -0.5)
    v = u[:, 2 * G:3 * G]
    o = u[:, 3 * G:4 * G]

    gates = gt_ref[...].reshape(R, GATE_PAD) + gbias_ref[...]
    lane = _iota((1, GATE_PAD), 1)
    gl = jnp.where(lane < N_HEADS, gates, -_softplus(-gates))
    rr = _iota((R, R), 0)
    rc = _iota((R, R), 1)
    same = (rr // L) == (rc // L)
    bcol = _dot_exact_lhs((same & (rr >= rc)).astype(F32), gl)
    gtot = _dot_exact_lhs(same.astype(F32), gl)
    sel = jnp.concatenate([(same & (rr <= rc)).astype(F32), (rr == rc).astype(F32)], axis=1)
    brow = _dot_exact_rhs(gl, sel, TN)
    causal = _iota((L, L), 0) >= _iota((L, L), 1)

    heads = range(N_HEADS)
    chains = [(ci, h) for ci in range(bb * nc) for h in heads]

    def cut(x, ch):
        ci, h = ch
        return x[ci * L:(ci + 1) * L, h * HD:(h + 1) * HD]

    def rows_of(x, ch, lane0):
        ci, h = ch
        return x[ci * L:(ci + 1) * L, lane0 + h:lane0 + h + 1]

    bc = {ch: rows_of(bcol, ch, N_HEADS) for ch in chains}
    ic = {ch: rows_of(gl, ch, 0) for ch in chains}
    g_tot = {(ci, h): gtot[ci * L:ci * L + 1, N_HEADS + h:N_HEADS + h + 1] for ci, h in chains}
    br = {(ci, h): brow[N_HEADS + h:N_HEADS + h + 1, ci * L:(ci + 1) * L] for ci, h in chains}
    ir = {(ci, h): brow[h:h + 1, R + ci * L:R + (ci + 1) * L] for ci, h in chains}
    q_f = {ch: cut(q, ch) for ch in chains}
    k_f = {ch: cut(k, ch) for ch in chains}
    q_b = {ch: q_f[ch].astype(BF16) for ch in chains}
    k_b = {ch: k_f[ch].astype(BF16) for ch in chains}
    v_f = {ch: cut(v, ch) for ch in chains}
    v_b = {ch: v_f[ch].astype(BF16) for ch in chains}
    log_d = {ch: jnp.where(causal, bc[ch] - br[ch] + ir[ch], -jnp.inf) for ch in chains}
    m_loc = {ch: jnp.max(log_d[ch], -1, keepdims=True) for ch in chains}
    s0 = {ch: _dot(q_b[ch], k_b[ch], NT) * jnp.exp(log_d[ch] - m_loc[ch]) for ch in chains}
    s_sum = {ch: jnp.sum(s0[ch], -1, keepdims=True) for ch in chains}
    sv = {ch: _dot(s0[ch].astype(BF16), v_b[ch]) for ch in chains}
    logw = {ch: g_tot[ch] - bc[ch] + ic[ch] for ch in chains}
    m_w = {ch: jnp.max(logw[ch], 0, keepdims=True) for ch in chains}
    w_loc = {ch: jnp.exp(logw[ch] - m_w[ch]) for ch in chains}
    kv = {ch: _dot((v_f[ch] * w_loc[ch]).astype(BF16), k_b[ch], TN) for ch in chains}
    n_loc = {ch: jnp.sum(k_f[ch] * w_loc[ch], 0, keepdims=True) for ch in chains}

    bh = [(b, h) for b in range(bb) for h in heads]
    m_vec = {b: m_sc[b] for b in range(bb)}
    m = {(b, h): m_vec[b][:, h:h + 1] for b, h in bh}
    c = {k_: c_sc[k_[0], k_[1]] for k_ in bh}
    n = {(b, h): n_sc[b, h:h + 1, :] for b, h in bh}
    h_chunk = {}
    for cc in range(nc):
        ch_of = {(b, h): (b * nc + cc, h) for b, h in bh}
        qc = {k_: _dot(q_b[ch_of[k_]], c[k_].astype(BF16), NT) for k_ in bh}
        qn = {k_: jnp.sum(q_f[ch_of[k_]] * n[k_], -1, keepdims=True) for k_ in bh}
        inter = {k_: bc[ch_of[k_]] + m[k_] for k_ in bh}
        m_row = {k_: jnp.maximum(m_loc[ch_of[k_]], inter[k_]) for k_ in bh}
        m_new = {k_: jnp.maximum(g_tot[ch_of[k_]] + m[k_], m_w[ch_of[k_]]) for k_ in bh}
        dec = {k_: jnp.exp(g_tot[ch_of[k_]] + m[k_] - m_new[k_]) for k_ in bh}
        f2 = {k_: jnp.exp(m_w[ch_of[k_]] - m_new[k_]) for k_ in bh}
        c = {k_: dec[k_] * c[k_] + f2[k_] * kv[ch_of[k_]] for k_ in bh}
        n = {k_: dec[k_] * n[k_] + f2[k_] * n_loc[ch_of[k_]] for k_ in bh}
        m = m_new
        f1 = {k_: jnp.exp(m_loc[ch_of[k_]] - m_row[k_]) for k_ in bh}
        w_int = {k_: jnp.exp(inter[k_] - m_row[k_]) for k_ in bh}
        num = {k_: f1[k_] * sv[ch_of[k_]] + w_int[k_] * qc[k_] for k_ in bh}
        den = {k_: f1[k_] * s_sum[ch_of[k_]] + w_int[k_] * qn[k_] for k_ in bh}
        hh = {k_: num[k_] / jnp.maximum(jnp.abs(den[k_]), jnp.exp(-m_row[k_])) for k_ in bh}
        for b in range(bb):
            h_chunk[b * nc + cc] = jnp.concatenate([hh[b, h] for h in heads], axis=-1)
    for b in range(bb):
        m_out = m_vec[b]
        for h in heads:
            c_sc[b, h] = c[b, h]
            n_sc[b, h:h + 1, :] = n[b, h]
            m_out = jnp.where(lane == h, m[b, h], m_out)
        m_sc[b] = m_out

    hs = [h_chunk[ci] for ci in range(bb * nc)]
    hcat = jnp.concatenate(hs, axis=0) if len(hs) > 1 else hs[0]
    bo = _block_ones()
    mu_h = _head_sum1(hcat, bo) * (1.0 / HD)
    hc = hcat - mu_h
    var = _head_sum1(hc * hc, bo) * (1.0 / HD)
    y_ref[...] = (hc * lax.rsqrt(var + ML_NORM_EPS) * ng_ref[...] * _sigmoid(o)).reshape(bb, TT, G)

    @pl.when(j == pl.num_programs(1) - 1)
    def _():
        c_ref[...] = c_sc[...]
        n_ref[...] = n_sc[...]
        m_ref[...] = m_sc[...]


def _mlstm_call(u_ml, gates, conv0, c0, n0, m0, cw, cbias, gbias, ng):
    B, T, _ = u_ml.shape
    L = math.gcd(T, ML_CHUNK)
    bb = math.gcd(B, ML_CHUNKS_PER_STEP)
    nc = math.gcd(T // L, ML_CHUNKS_PER_STEP // bb)
    TT = nc * L
    full = lambda a: pl.BlockSpec(a.shape, lambda b, j: (0,) * a.ndim)
    st4 = pl.BlockSpec((bb, N_HEADS, HEAD_DIM, HEAD_DIM), lambda b, j: (b, 0, 0, 0))
    st3 = pl.BlockSpec((bb, N_HEADS, HEAD_DIM), lambda b, j: (b, 0, 0))
    stm = pl.BlockSpec((bb, 1, GATE_PAD), lambda b, j: (b, 0, 0))
    return pl.pallas_call(
        functools.partial(_mlstm_kernel, L=L, bb=bb, nc=nc),
        grid=(B // bb, T // TT),
        in_specs=[pl.BlockSpec((bb, TT, ML_MAIN), lambda b, j: (b, j, 0)),
                  pl.BlockSpec((bb, TT, GATE_PAD), lambda b, j: (b, j, 0)),
                  pl.BlockSpec((bb, 3, 2 * GROUP_W), lambda b, j: (b, 0, 0)),
                  st4, st3, stm, full(cw), full(cbias), full(gbias), full(ng)],
        out_specs=[pl.BlockSpec((bb, TT, GROUP_W), lambda b, j: (b, j, 0)), st4, st3, stm],
        out_shape=[jax.ShapeDtypeStruct((B, T, GROUP_W), F32),
                   jax.ShapeDtypeStruct((B, N_HEADS, HEAD_DIM, HEAD_DIM), F32),
                   jax.ShapeDtypeStruct((B, N_HEADS, HEAD_DIM), F32),
                   jax.ShapeDtypeStruct((B, 1, GATE_PAD), F32)],
        scratch_shapes=[pltpu.VMEM((bb, 8 + TT, 2 * GROUP_W), F32),
                        pltpu.VMEM((bb, N_HEADS, HEAD_DIM, HEAD_DIM), F32),
                        pltpu.VMEM((bb, N_HEADS, HEAD_DIM), F32),
                        pltpu.VMEM((bb, 1, GATE_PAD), F32)],
        compiler_params=_params(("arbitrary", "arbitrary")),
        name="mlstm",
    )(u_ml, gates, conv0, c0, n0, m0, cw, cbias, gbias, ng)


def _pool_kernel(u_ref, hist_ref, w_ref, scale_ref, y_ref, zbuf, *, pos0):
    j = pl.program_id(1)
    bb, tt, G = u_ref.shape
    gc = G // len(POOL_WINDOWS)

    @pl.when(j == 0)
    def _():
        zbuf[:, 0:1, :] = jnp.zeros((bb, 1, G), F32)
        zbuf[:, 1:16, :] = hist_ref[...]

    u = u_ref[...]
    zbuf[:, 16:16 + tt, :] = u
    lane = _iota((1, 1, G), 2)
    pos = pos0 + j * tt + _iota((1, tt, 1), 1)
    acc = jnp.zeros((bb, tt, G), F32)
    pooled = jnp.zeros((bb, tt, G), F32)
    for t in range(max(POOL_WINDOWS)):
        acc = acc + zbuf[:, 16 - t:16 - t + tt, :]
        if (t + 1) in POOL_WINDOWS:
            gi = POOL_WINDOWS.index(t + 1)
            cnt = jnp.minimum(pos + 1, t + 1).astype(F32)
            pooled = jnp.where(lane // gc == gi, acc / cnt, pooled)
    tail = zbuf[:, tt:tt + 16, :]
    zbuf[:, 0:16, :] = tail
    pooled = (pooled - u).reshape(bb * tt, G)
    y_ref[...] = (_bdot(pooled, w_ref[...]) * scale_ref[...]).reshape(bb, tt, G)


def _pool_call(u_pool, hist, wbd, scale, pos0):
    B, T, G = u_pool.shape
    bb, tt = _row_blocks(B, T)
    return pl.pallas_call(
        functools.partial(_pool_kernel, pos0=pos0),
        grid=(B // bb, T // tt),
        in_specs=[pl.BlockSpec((bb, tt, G), lambda b, j: (b, j, 0)),
                  pl.BlockSpec((bb, POOL_HIST, G), lambda b, j: (b, 0, 0)),
                  pl.BlockSpec((G, G), lambda b, j: (0, 0)),
                  pl.BlockSpec((1, G), lambda b, j: (0, 0))],
        out_specs=pl.BlockSpec((bb, tt, G), lambda b, j: (b, j, 0)),
        out_shape=jax.ShapeDtypeStruct((B, T, G), F32),
        scratch_shapes=[pltpu.VMEM((bb, 16 + tt, G), F32)],
        compiler_params=_params(("arbitrary", "arbitrary")),
        name="pool",
    )(u_pool, hist, wbd, scale)


def _slope_of(head):
    return jnp.where(head == 0, 2.0 ** -2, jnp.where(head == 1, 2.0 ** -4, jnp.where(head == 2, 2.0 ** -6, 2.0 ** -8)))


def _topk_mask(gate, valid, nblk):
    blk = _iota((1, nblk), 1)
    cnt = jnp.zeros(gate.shape, F32)
    for n in range(nblk):
        gn = gate[:, n:n + 1]
        ahead = (gn > gate) | ((gn == gate) & (n < blk))
        cnt = cnt + jnp.where(ahead, 1.0, 0.0)
    return valid & (cnt < MOBA_TOPK)


def _moba_p_kernel(q_ref, k_ref, vt_ref, o_ref, mean_sc, kb_sc, vt_sc, sel_sc, *, nblk):
    i = pl.program_id(1)
    BLK, G, HD, H = MOBA_BLOCK, GROUP_W, HEAD_DIM, N_HEADS
    W = H * BLK

    SLAB = 2 * HD
    lane_s = _iota((1, SLAB), 1)

    def slab_of(x, h, extra):
        xs = x[:, (h // 2) * SLAB:(h // 2 + 1) * SLAB]
        own = (lane_s // HD) == (h % 2)
        spare = HD if h % 2 == 0 else 0
        return jnp.where(own, xs, jnp.where(lane_s == spare, extra, 0.0))

    @pl.when(i == 0)
    def _():
        pos = _iota((BLK, 1), 0).astype(F32)
        for n in range(nblk):
            kblk = k_ref[0, n * BLK:(n + 1) * BLK, :]
            mean_sc[n:n + 1, :] = jnp.mean(kblk, axis=0, keepdims=True)
            for h in range(H):
                kb_sc[n, h] = slab_of(kblk, h, pos).astype(BF16)
            vt_sc[n] = vt_ref[0, :, n * BLK:(n + 1) * BLK].astype(BF16)

    q = q_ref[0]
    lane_head = _iota((1, G), 1) // HD
    qbd = jnp.concatenate([jnp.where(lane_head == h, q, 0.0) for h in range(H)], axis=0)
    slopes = [2.0 ** (-2.0 * (h + 1)) for h in range(H)]
    scale = HD ** -0.5
    qs = [slab_of(q * scale, h, slopes[h]).astype(BF16) for h in range(H)]

    blk_row = _iota((nblk, 1), 0)
    valid = blk_row < i
    gate = jnp.where(valid, _dot3(mean_sc[...], qbd, NT), -jnp.inf)
    cnt = jnp.zeros((nblk, W), F32)
    for n in range(nblk):
        gn = gate[n:n + 1, :]
        ahead = (gn > gate) | ((gn == gate) & (n < blk_row))
        cnt = cnt + jnp.where(ahead, 1.0, 0.0)
    sel = jnp.where(valid & (cnt < MOBA_TOPK), 1.0, 0.0)
    for n in range(nblk):
        sel_sc[n] = sel[n:n + 1, :]

    heads = range(H)

    def scores(jb):
        return [_dot(kb_sc[jb, h], qs[h], NT) for h in heads]

    def pv(jb, p):
        vt = vt_sc[jb]
        return [_dot(vt[h * HD:(h + 1) * HD, :], p[h].astype(BF16)) for h in heads]

    causal = _iota((BLK, BLK), 0) <= _iota((BLK, BLK), 1)
    s = [jnp.where(causal, x, -jnp.inf) for x in scores(i)]
    m0 = [jnp.max(x, 0, keepdims=True) for x in s]
    p = [jnp.exp(s[h] - m0[h]) for h in heads]
    l0 = [jnp.sum(x, 0, keepdims=True) for x in p]
    acc0 = pv(i, p)

    def update(jbs, carry):
        m, l, acc = carry
        s = []
        for jb in jbs:
            s_raw = scores(jb)
            off = ((i - jb) * BLK).astype(F32)
            s.append([s_raw[h] + jnp.where(sel_sc[jb, :, h * BLK:(h + 1) * BLK] > 0.0, -slopes[h] * off, -jnp.inf)
                      for h in heads])
        m2 = m
        for x in s:
            m2 = [jnp.maximum(m2[h], jnp.max(x[h], 0, keepdims=True)) for h in heads]
        alpha = [jnp.exp(m[h] - m2[h]) for h in heads]
        p = [jnp.concatenate([jnp.exp(x[h] - m2[h]) for x in s], axis=0) for h in heads]
        vt = [jnp.concatenate([vt_sc[jb][h * HD:(h + 1) * HD, :] for jb in jbs], axis=1) for h in heads]
        new = [_dot(vt[h], p[h].astype(BF16)) for h in heads]
        l = [alpha[h] * l[h] + jnp.sum(p[h], 0, keepdims=True) for h in heads]
        acc = [alpha[h] * acc[h] + new[h] for h in heads]
        return m2, l, acc

    carry = lax.fori_loop(0, i // 2, lambda jp, c: update((2 * jp, 2 * jp + 1), c), (m0, l0, acc0))
    m, l, acc = lax.fori_loop(0, i % 2, lambda _, c: update((i - 1,), c), carry)
    out_t = jnp.concatenate([acc[h] / l[h] for h in heads], axis=0)
    o_ref[0] = out_t.T


def _moba_prompt_call(q, k, v_t):
    B, T, G = q.shape
    assert T % MOBA_BLOCK == 0
    nblk = T // MOBA_BLOCK
    return pl.pallas_call(
        functools.partial(_moba_p_kernel, nblk=nblk),
        grid=(B, nblk),
        in_specs=[pl.BlockSpec((1, MOBA_BLOCK, G), lambda b, i: (b, i, 0)),
                  pl.BlockSpec((1, T, G), lambda b, i: (b, 0, 0)),
                  pl.BlockSpec((1, G, T), lambda b, i: (b, 0, 0))],
        out_specs=pl.BlockSpec((1, MOBA_BLOCK, G), lambda b, i: (b, i, 0)),
        out_shape=jax.ShapeDtypeStruct((B, T, G), F32),
        scratch_shapes=[pltpu.VMEM((nblk, G), F32),
                        pltpu.VMEM((nblk, N_HEADS, MOBA_BLOCK, 2 * HEAD_DIM), BF16),
                        pltpu.VMEM((nblk, G, MOBA_BLOCK), BF16),
                        pltpu.VMEM((nblk, 1, N_HEADS * MOBA_BLOCK), F32)],
        compiler_params=_params(("arbitrary", "arbitrary")),
        name="moba_prompt",
    )(q, k, v_t)


def _moba_s_kernel(pt_ref, q_ref, kn_ref, vn_ref, *refs, n_pages, page, past_len, rb):
    kp = refs[:rb * n_pages]
    vp = refs[rb * n_pages:2 * rb * n_pages]
    o_ref = refs[2 * rb * n_pages]
    G, HD = GROUP_W, HEAD_DIM
    ts = q_ref.shape[1]
    R = N_HEADS * ts
    ppb = MOBA_BLOCK // page
    nb = n_pages // ppb
    rows = range(rb)
    pages = range(n_pages)
    lane_head = _iota((1, G), 1) // HD
    row_head = _iota((R, 1), 0) // ts
    row_t = _iota((R, 1), 0) % ts
    slope = _slope_of(row_head)
    scale = HD ** -0.5
    qbd = [jnp.concatenate([jnp.where(lane_head == h, q_ref[r], 0.0) for h in range(N_HEADS)], axis=0)
           for r in rows]
    qb = [x.astype(BF16) for x in qbd]

    kpg = {(r, p): kp[r * n_pages + p][...] for r in rows for p in pages}
    raw = {(r, p): _dot(qb[r], kpg[r, p].astype(BF16)) for r in rows for p in pages}
    sums = {(r, p): jnp.sum(kpg[r, p], axis=1, keepdims=True) for r in rows for p in pages}
    means_t = [jnp.concatenate(
        [sum(sums[r, p] for p in range(n * ppb, (n + 1) * ppb)) * (1.0 / MOBA_BLOCK) for n in range(nb)], axis=1)
        for r in rows]
    gate = [_dot3(qbd[r], means_t[r]) for r in rows]
    all_valid = _iota((1, nb), 1) >= 0
    sel = [jnp.where(_topk_mask(gate[r], all_valid, nb), 1.0, 0.0) for r in rows]

    q_pos = (past_len + row_t).astype(F32)
    off = _iota((1, page), 1).astype(F32)
    scores = {(r, p): jnp.where(sel[r][:, p // ppb:p // ppb + 1] > 0.0,
                                raw[r, p] * scale - slope * (q_pos - (p * page + off)), -jnp.inf)
              for r in rows for p in pages}
    t_new = _iota((1, ts), 1)
    s_own = [jnp.where(t_new <= row_t,
                       _dot(qb[r], kn_ref[r].astype(BF16), NT) * scale - slope * (row_t - t_new).astype(F32),
                       -jnp.inf) for r in rows]

    m = [jnp.max(s_own[r], -1, keepdims=True) for r in rows]
    for p in pages:
        m = [jnp.maximum(m[r], jnp.max(scores[r, p], -1, keepdims=True)) for r in rows]
    p_own = [jnp.exp(s_own[r] - m[r]) for r in rows]
    l = [jnp.sum(p_own[r], -1, keepdims=True) for r in rows]
    acc = [_dot(p_own[r].astype(BF16), vn_ref[r].astype(BF16)) for r in rows]
    for p in pages:
        pr = [jnp.exp(scores[r, p] - m[r]) for r in rows]
        l = [l[r] + jnp.sum(pr[r], -1, keepdims=True) for r in rows]
        acc = [acc[r] + _dot(pr[r].astype(BF16), vp[r * n_pages + p][...].astype(BF16), NT) for r in rows]
    for r in rows:
        out = acc[r] / l[r]
        y = jnp.zeros((ts, G), F32)
        for h in range(N_HEADS):
            y = jnp.where(lane_head == h, out[h * ts:(h + 1) * ts, :], y)
        o_ref[r] = y


def _moba_sample_call(q, k, v, cache_k, cache_v, page_table, layer):
    DB, TS, G = q.shape
    Ld, n_phys, page, H, d = cache_k.shape
    n_pages = page_table.shape[1]
    past_len = n_pages * page
    assert past_len % MOBA_BLOCK == 0 and MOBA_BLOCK % page == 0 and past_len // MOBA_BLOCK >= MOBA_TOPK
    ck = jnp.transpose(cache_k, (0, 1, 3, 4, 2)).reshape(Ld, n_phys, H * d, page)
    cv = jnp.transpose(cache_v, (0, 1, 3, 4, 2)).reshape(Ld, n_phys, H * d, page)
    rb = math.gcd(DB, MOBA_SAMPLE_ROWS_PER_STEP)
    new = pl.BlockSpec((rb, TS, G), lambda b, pt: (b, 0, 0))
    pg = [pl.BlockSpec((None, None, G, page),
                       functools.partial(lambda b, pt, r, p: (layer, pt[b * rb + r, p], 0, 0), r=r, p=p))
          for r in range(rb) for p in range(n_pages)]
    return pl.pallas_call(
        functools.partial(_moba_s_kernel, n_pages=n_pages, page=page, past_len=past_len, rb=rb),
        grid_spec=pltpu.PrefetchScalarGridSpec(
            num_scalar_prefetch=1, grid=(DB // rb,),
            in_specs=[new, new, new] + pg + pg,
            out_specs=new),
        out_shape=jax.ShapeDtypeStruct((DB, TS, G), F32),
        compiler_params=_params(("arbitrary",)),
        name="moba_sample",
    )(page_table, q, k, v, *([ck] * (rb * n_pages)), *([cv] * (rb * n_pages)))


def _out_kernel(yrw_ref, yml_ref, ypl_ref, yat_ref, x_ref, g1_ref, sc2_ref, sh2_ref, g2_ref, n2g_ref, fng_ref,
                wout_ref, wup_ref, wdn_ref, o_ref, h2_sc, acc_sc, *, final):
    f = pl.program_id(2)
    bb, tt, D = x_ref.shape
    M = bb * tt

    @pl.when(f == 0)
    def _():
        ycat = jnp.concatenate([r[...].reshape(M, GROUP_W) for r in (yrw_ref, yml_ref, ypl_ref, yat_ref)], axis=-1)
        y = _dot(ycat.astype(BF16), wout_ref[...]).reshape(bb, tt, D)
        x1 = x_ref[...] + g1_ref[...] * y
        o_ref[...] = x1
        hn = x1 * lax.rsqrt(jnp.mean(x1 * x1, -1, keepdims=True) + NORM_EPS) * n2g_ref[...]
        h2_sc[...] = (hn * (1.0 + sc2_ref[...]) + sh2_ref[...]).reshape(M, D).astype(BF16)
        acc_sc[...] = jnp.zeros((M, D), F32)

    a = _dot(h2_sc[...], wup_ref[...])
    a = jnp.square(jnp.maximum(a, 0.0)).astype(BF16)
    acc_sc[...] += _dot(a, wdn_ref[...])

    @pl.when(f == pl.num_programs(2) - 1)
    def _():
        x2 = o_ref[...] + g2_ref[...] * acc_sc[...].reshape(bb, tt, D)
        if final:
            x2 = x2 * lax.rsqrt(jnp.mean(x2 * x2, -1, keepdims=True) + NORM_EPS) * fng_ref[...]
        o_ref[...] = x2


def _out_call(ys, x, g1, sc2, sh2, g2, n2g, fng, wout, wup, wdn, final):
    B, T, D = x.shape
    bb, tt = _row_blocks(B, T, OUT_ROW_TILE)
    FF = wup.shape[1]
    row = lambda n: pl.BlockSpec((bb, tt, n), lambda b, j, f: (b, j, 0))
    mod = pl.BlockSpec((bb, 1, D), lambda b, j, f: (b, 0, 0))
    vec = pl.BlockSpec((1, D), lambda b, j, f: (0, 0))
    return pl.pallas_call(
        functools.partial(_out_kernel, final=final),
        grid=(B // bb, T // tt, FF // FF_TILE),
        in_specs=[row(GROUP_W)] * 4 + [row(D), mod, mod, mod, mod, vec, vec,
                  pl.BlockSpec((D, D), lambda b, j, f: (0, 0)),
                  pl.BlockSpec((D, FF_TILE), lambda b, j, f: (0, f)),
                  pl.BlockSpec((FF_TILE, D), lambda b, j, f: (f, 0))],
        out_specs=row(D),
        out_shape=jax.ShapeDtypeStruct((B, T, D), F32),
        scratch_shapes=[pltpu.VMEM((bb * tt, D), BF16),
                        pltpu.VMEM((bb * tt, D), F32)],
        compiler_params=_params(("arbitrary", "arbitrary", "arbitrary"), OUT_VMEM_LIMIT),
        name="out_mlp",
    )(*ys, x, g1, sc2, sh2, g2, n2g, fng, wout, wup, wdn)


def _rw_perm(x):
    return jnp.concatenate([x[..., 0:256], x[..., 320:576], x[..., 576:832],
                            x[..., 256:320], x[..., 832:896], x[..., 896:1024]], axis=-1)


def _rw_unperm(x):
    return jnp.concatenate([x[..., 0:256], x[..., 768:832], x[..., 256:512],
                            x[..., 512:768], x[..., 832:896], x[..., 896:1024]], axis=-1)


def _layer(x, mods, st, pos0, attend, lw, final, transposed_kv):
    sh1, sc1, g1, sh2, sc2, g2 = mods
    shift0, wkv0, conv0, c0, n0, m0, pool0 = st
    B, T, D = x.shape
    outs = _in_call(x, sc1, sh1, lw['norm1_g'], lw['wpack'], lw['wkv_t'], transposed_kv)
    u_rw, u_ml, gates, u_pool, q, k = outs[:6]

    y_rw, wkv1 = _rwkv_call(u_rw, _rw_perm(shift0)[:, None, :], wkv0, lw['rw_mu'], lw['rw_vec'],
                            lw['rwkv_w_up'], lw['rwkv_a_up'], lw['rwkv_g_up'])
    shift1 = _rw_unperm(u_rw[:, -1, :])

    m0p = jnp.pad(m0, ((0, 0), (0, GATE_PAD - N_HEADS)))[:, None, :]
    y_ml, c1, n1, m1p = _mlstm_call(u_ml, gates, conv0, c0, n0, m0p, lw['mlstm_conv_w'], lw['ml_cbias'],
                                    lw['ml_gbias'], lw['ml_ng'])
    m1 = m1p[:, 0, :N_HEADS]
    zc = u_ml[:, :, :2 * GROUP_W] if T >= 3 else jnp.concatenate([conv0, u_ml[:, :, :2 * GROUP_W]], 1)
    conv1 = zc[:, -3:]

    y_pool = _pool_call(u_pool, pool0, lw['pool_wbd'], lw['pool_scale'], pos0)
    zp = u_pool if T >= POOL_HIST else jnp.concatenate([pool0, u_pool], 1)
    pool1 = zp[:, -POOL_HIST:]

    if transposed_kv:
        k_out, v_out = outs[6], outs[7]
        y_at = attend(q, k, v_out)
    else:
        kv_shape = (B, T, N_HEADS, HEAD_DIM)
        y_at = attend(q, k, outs[6])
        k_out, v_out = k.reshape(kv_shape), outs[6].reshape(kv_shape)

    x_new = _out_call((y_rw, y_ml, y_pool, y_at), x, g1, sc2, sh2, g2, lw['norm2_g'], lw['final_g'],
                      lw['w_out'], lw['mlp_up'], lw['mlp_down'], final)
    return x_new, (shift1, wkv1, conv1, c1, n1, m1, pool1, k_out, v_out)


def kernel(x_prompt, x_sample, c_prompt, c_sample, state_rwkv_shift, state_rwkv_wkv, state_mlstm_conv, state_mlstm_c, state_mlstm_n, state_mlstm_m, state_pool, cache_k, cache_v, page_table, ada_w, ada_b, norm1_g, norm2_g, w_in, w_out, rwkv_mu, rwkv_w0, rwkv_w_up, rwkv_a0, rwkv_a_up, rwkv_g_up, rwkv_k_k, rwkv_k_a, rwkv_r_k, rwkv_ln_g, rwkv_ln_b, mlstm_conv_w, mlstm_conv_b, mlstm_i_b, mlstm_f_b, mlstm_norm_g, pool_w, pool_scale, mlp_up, mlp_down, final_norm_g):
    B, T, D = x_prompt.shape
    DB = x_sample.shape[0]
    depth = ada_w.shape[0]
    G = GROUP_W
    assert D == D_MODEL and w_in.shape[-1] == N_IN

    mod = _ada_call(jnp.concatenate([c_prompt, c_sample], 0), ada_w, ada_b)

    st_p0 = (jnp.zeros((B, RW_COLS), F32), jnp.zeros((B, N_HEADS, HEAD_DIM, HEAD_DIM), F32),
             jnp.zeros((B, 3, 2 * G), F32), jnp.zeros((B, N_HEADS, HEAD_DIM, HEAD_DIM), F32),
             jnp.zeros((B, N_HEADS, HEAD_DIM), F32), jnp.zeros((B, N_HEADS), F32),
             jnp.zeros((B, POOL_HIST, G), F32))
    past_len = page_table.shape[1] * cache_k.shape[2]

    xp, xs = x_prompt, x_sample
    new_p, new_s = [], []
    for l in range(depth):
        wl = w_in[l]
        wpack = jnp.concatenate([
            _rw_perm(wl[:, :RW_COLS]),
            wl[:, RW_COLS:RW_COLS + ML_MAIN],
            jnp.pad(wl[:, RW_COLS + ML_MAIN:RW_COLS + ML_MAIN + ML_GATES], ((0, 0), (0, GATE_PAD - ML_GATES))),
            wl[:, RW_COLS + ML_MAIN + ML_GATES:],
        ], axis=1).astype(BF16)
        pool_wbd = jnp.zeros((G, G), F32)
        gcw = G // len(POOL_WINDOWS)
        for gi in range(len(POOL_WINDOWS)):
            pool_wbd = pool_wbd.at[gi * gcw:(gi + 1) * gcw, gi * gcw:(gi + 1) * gcw].set(pool_w[l, gi])
        row = lambda a: a.reshape(1, -1)
        lw = {
            'norm1_g': row(norm1_g[l]), 'norm2_g': row(norm2_g[l]), 'final_g': row(final_norm_g),
            'wpack': wpack, 'wkv_t': wl[:, N_IN - 2 * G:].T.astype(BF16), 'w_out': w_out[l].astype(BF16),
            'mlp_up': mlp_up[l].astype(BF16), 'mlp_down': mlp_down[l].astype(BF16),
            'rw_mu': row(_rw_perm(rwkv_mu[l])),
            'rw_vec': jnp.stack([rwkv_w0[l], rwkv_a0[l], rwkv_k_k[l], rwkv_k_a[l], rwkv_ln_g[l], rwkv_ln_b[l],
                                 rwkv_r_k[l].reshape(-1), jnp.zeros((G,), F32)]),
            'rwkv_w_up': rwkv_w_up[l], 'rwkv_a_up': rwkv_a_up[l], 'rwkv_g_up': rwkv_g_up[l],
            'mlstm_conv_w': mlstm_conv_w[l], 'ml_cbias': row(mlstm_conv_b[l]),
            'ml_gbias': row(jnp.pad(jnp.concatenate([mlstm_i_b[l], mlstm_f_b[l]]), (0, GATE_PAD - ML_GATES))),
            'ml_ng': row(mlstm_norm_g[l]),
            'pool_wbd': pool_wbd, 'pool_scale': row(pool_scale[l]),
        }
        mods = [mod[l, :, i * D:(i + 1) * D][:, None, :] for i in range(6)]
        mods_p = [m[:B] for m in mods]
        mods_s = [m[B:] for m in mods]
        final = l == depth - 1

        xp, st_p = _layer(xp, mods_p, st_p0, 0, _moba_prompt_call, lw, final, True)
        st_s_in = (state_rwkv_shift[l], state_rwkv_wkv[l], state_mlstm_conv[l], state_mlstm_c[l],
                   state_mlstm_n[l], state_mlstm_m[l], state_pool[l])
        attend_s = functools.partial(_moba_sample_call, cache_k=cache_k, cache_v=cache_v,
                                     page_table=page_table, layer=l)
        xs, st_s = _layer(xs, mods_s, st_s_in, past_len, attend_s, lw, final, False)
        new_p.append(st_p)
        new_s.append(st_s)

    stack = lambda lst: tuple(jnp.stack([st[i] for st in lst]) for i in range(9))
    out_p = list(stack(new_p))
    for i in (7, 8):
        out_p[i] = jnp.transpose(out_p[i].reshape(depth, B, N_HEADS, HEAD_DIM, T), (0, 1, 4, 2, 3))
    return (xp, xs) + tuple(out_p) + stack(new_s)
```

```python
import functools
import math

import jax
import jax.numpy as jnp
from jax import lax
from jax.experimental import pallas as pl
from jax.experimental.pallas import tpu as pltpu

F32 = jnp.float32
BF16 = jnp.bfloat16

D_MODEL = 1024
GROUP_W = 256
HEAD_DIM = 64
N_HEADS = 4
RW_COLS = 1024
ML_MAIN = 1024
ML_GATES = 8
GATE_PAD = 128
N_IN = 3080
MOBA_BLOCK = 256
MOBA_TOPK = 3
ML_CHUNK = 256
RW_CHUNK = 64
RW_CHUNKS_PER_STEP = 8
ML_CHUNKS_PER_STEP = 4
MOBA_SAMPLE_ROWS_PER_STEP = 4
POOL_WINDOWS = (2, 4, 8, 16)
POOL_HIST = 15
NORM_EPS = 1e-6
RW_GN_EPS = 64e-5
ML_NORM_EPS = 1e-6
ROW_TILE = 512
FF_TILE = 1024
VMEM_LIMIT = 48 * 1024 * 1024
OUT_ROW_TILE = 1024
OUT_VMEM_LIMIT = 60 * 1024 * 1024

NN = (((1,), (0,)), ((), ()))
NT = (((1,), (1,)), ((), ()))
TN = (((0,), (0,)), ((), ()))


def _dot(a, b, dn=NN):
    return lax.dot_general(a, b, dn, preferred_element_type=F32)


def _bdot(a, b, dn=NN):
    return _dot(a.astype(BF16), b.astype(BF16), dn)


def _split2(x):
    hi = x.astype(BF16)
    lo = (x - hi.astype(F32)).astype(BF16)
    return hi, lo


def _split3(x):
    hi = x.astype(BF16)
    r = x - hi.astype(F32)
    mid = r.astype(BF16)
    lo = (r - mid.astype(F32)).astype(BF16)
    return hi, mid, lo


def _dot3(a, b, dn=NN):
    ah, al = _split2(a)
    bh, bl = _split2(b)
    return _dot(ah, bh, dn) + (_dot(ah, bl, dn) + _dot(al, bh, dn))


def _dot3s(a_s, b_s, dn=NN):
    return _dot(a_s[0], b_s[0], dn) + (_dot(a_s[0], b_s[1], dn) + _dot(a_s[1], b_s[0], dn))


def _dot_exact_rhs(a, b01, dn=NN):
    h, m, l = _split3(a)
    b = b01.astype(BF16)
    return _dot(h, b, dn) + (_dot(m, b, dn) + _dot(l, b, dn))


def _dot_exact_lhs(a01, b, dn=NN):
    h, m, l = _split3(b)
    a = a01.astype(BF16)
    return _dot(a, h, dn) + (_dot(a, m, dn) + _dot(a, l, dn))


def _iota(shape, dim):
    return lax.broadcasted_iota(jnp.int32, shape, dim)


def _sigmoid(x):
    return jax.nn.sigmoid(x)


def _softplus(x):
    return jnp.maximum(x, 0.0) + jnp.log(1.0 + jnp.exp(-jnp.abs(x)))


def _block_ones():
    return (_iota((GROUP_W, GROUP_W), 0) // HEAD_DIM == _iota((GROUP_W, GROUP_W), 1) // HEAD_DIM).astype(F32)


def _head_sum(x, bo):
    return _dot_exact_rhs(x, bo)


def _head_sum1(x, bo):
    return _bdot(x, bo)


def _params(sem, vmem_limit=VMEM_LIMIT):
    return pltpu.CompilerParams(dimension_semantics=sem, vmem_limit_bytes=vmem_limit)


def _row_blocks(B, T, tile=ROW_TILE):
    if T >= tile:
        assert T % tile == 0
        return 1, tile
    bb = max(1, min(B, tile // T))
    while B % bb:
        bb -= 1
    return bb, T


def _ada_kernel(c_ref, w_ref, b_ref, o_ref):
    c = c_ref[...]
    o_ref[...] = _bdot(c * _sigmoid(c), w_ref[...]) + b_ref[...]


def _ada_call(c_all, ada_w, ada_b):
    Ld, D, N6 = ada_w.shape
    NB = c_all.shape[0]
    tn = 1024
    return pl.pallas_call(
        _ada_kernel,
        grid=(Ld, N6 // tn),
        in_specs=[pl.BlockSpec((NB, D), lambda l, j: (0, 0)),
                  pl.BlockSpec((None, D, tn), lambda l, j: (l, 0, j)),
                  pl.BlockSpec((None, 1, tn), lambda l, j: (l, 0, j))],
        out_specs=pl.BlockSpec((None, NB, tn), lambda l, j: (l, 0, j)),
        out_shape=jax.ShapeDtypeStruct((Ld, NB, N6), F32),
        compiler_params=_params(("arbitrary", "arbitrary")),
        name="ada_mod",
    )(c_all, ada_w, ada_b.reshape(Ld, 1, N6))


IN_WIDTHS = (RW_COLS, ML_MAIN, GATE_PAD, GROUP_W, GROUP_W, GROUP_W, GROUP_W)


def _in_kernel(x_ref, sc_ref, sh_ref, g_ref, w_ref, wt_ref, *rest, n_rows_out, n_alias):
    out_refs = rest[n_alias:]
    x = x_ref[...]
    bb, tt, D = x.shape
    y = x * lax.rsqrt(jnp.mean(x * x, -1, keepdims=True) + NORM_EPS) * g_ref[...]
    h = y * (1.0 + sc_ref[...]) + sh_ref[...]
    hb = h.reshape(bb * tt, D).astype(BF16)
    off = 0
    for ref in out_refs[:n_rows_out]:
        n = ref.shape[-1]
        ref[...] = _dot(hb, w_ref[:, off:off + n]).reshape(bb, tt, n)
        off += n
    for i, ref in enumerate(out_refs[n_rows_out:]):
        ref[0] = _dot(wt_ref[i * GROUP_W:(i + 1) * GROUP_W, :], hb, NT)


def _in_call(x, sc, sh, g, wpack, wkv_t, transposed_kv, layer, depth, prev_kv):
    B, T, D = x.shape
    bb, tt = _row_blocks(B, T)
    row = lambda n: pl.BlockSpec((bb, tt, n), lambda b, j: (b, j, 0))
    mod = pl.BlockSpec((bb, 1, D), lambda b, j: (b, 0, 0))
    widths = IN_WIDTHS[:-1] if transposed_kv else IN_WIDTHS
    out_specs = [row(n) for n in widths]
    out_shape = [jax.ShapeDtypeStruct((B, T, n), F32) for n in widths]
    inputs = [x, sc, sh, g, wpack, wkv_t]
    in_specs = [row(D), mod, mod,
                pl.BlockSpec((1, D), lambda b, j: (0, 0)),
                pl.BlockSpec(wpack.shape, lambda b, j: (0, 0)),
                pl.BlockSpec(wkv_t.shape, lambda b, j: (0, 0))]
    aliases = {}
    if transposed_kv:
        assert bb == 1
        out_specs += [pl.BlockSpec((None, 1, GROUP_W, tt), lambda b, j: (layer, b, 0, j))] * 2
        out_shape += [jax.ShapeDtypeStruct((depth, B, GROUP_W, T), F32)] * 2
        if prev_kv is not None:
            aliases = {len(inputs) + i: len(widths) + i for i in range(2)}
            inputs += list(prev_kv)
            in_specs += [pl.BlockSpec(memory_space=pl.ANY)] * 2
    return pl.pallas_call(
        functools.partial(_in_kernel, n_rows_out=len(widths), n_alias=len(aliases)),
        grid=(B // bb, T // tt),
        in_specs=in_specs,
        out_specs=out_specs,
        out_shape=out_shape,
        input_output_aliases=aliases,
        compiler_params=_params(("arbitrary", "arbitrary")),
        name="in_proj",
    )(*inputs)


def _rwkv_kernel(u_ref, s0_ref, wkv0_ref, mu_ref, vec_ref, wup_ref, aup_ref, gup_ref,
                 y_ref, wkv_ref, zbuf, s_sc, *, L, bb, nc):
    j = pl.program_id(1)
    G, HD = GROUP_W, HEAD_DIM
    TT = nc * L
    R = bb * TT

    @pl.when(j == 0)
    def _():
        zbuf[:, 7:8, :] = s0_ref[...]
        s_sc[...] = wkv0_ref[...]

    u3 = u_ref[...]
    zbuf[:, 8:8 + TT, :] = u3
    prev = zbuf[:, 7:7 + TT, :]
    xs = (u3 + (prev - u3) * mu_ref[...]).reshape(R, RW_COLS)
    zbuf[:, 7:8, :] = u3[:, TT - 1:TT, :]

    r = xs[:, 0:G]
    k = xs[:, G:2 * G]
    v = xs[:, 2 * G:3 * G]
    wc = xs[:, 3 * G:3 * G + 64]
    ac = xs[:, 3 * G + 64:3 * G + 128]
    gc = xs[:, 3 * G + 128:4 * G]
    w0, a0, k_k, k_a = vec_ref[0:1, :], vec_ref[1:2, :], vec_ref[2:3, :], vec_ref[3:4, :]
    ln_g, ln_b, r_k = vec_ref[4:5, :], vec_ref[5:6, :], vec_ref[6:7, :]

    w_log = -_softplus(-(w0 + _bdot(jnp.tanh(wc), wup_ref[...]))) - 0.5
    lw = -jnp.exp(w_log)
    a = _sigmoid(a0 + _bdot(ac, aup_ref[...]))
    g = _bdot(_sigmoid(gc), gup_ref[...])
    bo = _block_ones()
    kk = k * k_k
    kk = kk / jnp.maximum(jnp.sqrt(_head_sum(kk * kk, bo)), 1e-12)
    k2 = k * (1.0 + (a - 1.0) * k_a)
    beta = kk * a

    rr = _iota((R, R), 0)
    rc = _iota((R, R), 1)
    same = (rr // L) == (rc // L)
    c = _dot_exact_lhs((same & (rr >= rc)).astype(F32), lw)
    cl = _dot_exact_lhs(same.astype(F32), lw)
    e_inv = jnp.exp(-c)
    e_tail = jnp.exp(cl - c)
    a_t = -kk * jnp.exp(c - lw)
    b_t = beta * e_inv
    k_t = k2 * e_inv
    r_t = r * jnp.exp(c)
    b_l = beta * e_tail
    k_l = k2 * e_tail
    e_cl = jnp.exp(cl)

    row = _iota((L, L), 0)
    col = _iota((L, L), 1)
    incl = row >= col
    strict = row > col
    eye = (row == col).astype(F32)
    n_double = max(1, int(math.ceil(math.log2(L)))) - 1

    chains = [(ci, h) for ci in range(bb * nc) for h in range(N_HEADS)]

    def cut(x, ch):
        ci, h = ch
        return x[ci * L:(ci + 1) * L, h * HD:(h + 1) * HD]

    a_s = {ch: _split2(cut(a_t, ch)) for ch in chains}
    b_s = {ch: _split2(cut(b_t, ch)) for ch in chains}
    r_b = {ch: cut(r_t, ch).astype(BF16) for ch in chains}
    k_b = {ch: cut(k_t, ch).astype(BF16) for ch in chains}
    v_b = {ch: cut(v, ch).astype(BF16) for ch in chains}
    n_ab = {ch: jnp.where(strict, _dot3s(a_s[ch], b_s[ch], NT), 0.0) for ch in chains}
    n_ak = {ch: jnp.where(strict, _dot(a_s[ch][0], k_b[ch], NT), 0.0).astype(BF16) for ch in chains}
    n_rb = {ch: jnp.where(incl, _dot(r_b[ch], b_s[ch][0], NT), 0.0).astype(BF16) for ch in chains}
    n_rk = {ch: jnp.where(incl, _dot(r_b[ch], k_b[ch], NT), 0.0).astype(BF16) for ch in chains}
    p = {ch: eye + n_ab[ch] for ch in chains}
    m_s = {ch: _split2(n_ab[ch]) for ch in chains}
    for _ in range(n_double):
        m_s = {ch: _split2(_dot3s(m_s[ch], m_s[ch])) for ch in chains}
        p = {ch: p[ch] + _dot3s(m_s[ch], _split2(p[ch])) for ch in chains}
    p_b = {ch: p[ch].astype(BF16) for ch in chains}
    akv = {ch: _dot(n_ak[ch], v_b[ch]).astype(BF16) for ch in chains}
    y_v = {ch: _dot(n_rk[ch], v_b[ch]) for ch in chains}
    w_b = {ch: _dot(p_b[ch], a_s[ch][0]).astype(BF16) for ch in chains}
    u_t = {ch: _dot(p_b[ch], akv[ch]) for ch in chains}
    bl_b = {ch: cut(b_l, ch).astype(BF16) for ch in chains}
    kl_b = {ch: cut(k_l, ch).astype(BF16) for ch in chains}

    heads = range(N_HEADS)
    bh = [(b, h) for b in range(bb) for h in heads]
    s = {k_: s_sc[k_[0], k_[1]] for k_ in bh}
    y_chunk = {}
    for cc in range(nc):
        ch_of = {(b, h): (b * nc + cc, h) for b, h in bh}
        s_b = {k_: s[k_].astype(BF16) for k_ in bh}
        e_b = {k_: (_dot(w_b[ch_of[k_]], s_b[k_], NT) + u_t[ch_of[k_]]).astype(BF16) for k_ in bh}
        s = {k_: s[k_] * cut(e_cl, ch_of[k_])[0:1, :]
             + _dot(e_b[k_], bl_b[ch_of[k_]], TN) + _dot(v_b[ch_of[k_]], kl_b[ch_of[k_]], TN) for k_ in bh}
        yh = {k_: _dot(r_b[ch_of[k_]], s_b[k_], NT) + _dot(n_rb[ch_of[k_]], e_b[k_]) + y_v[ch_of[k_]] for k_ in bh}
        for b in range(bb):
            y_chunk[b * nc + cc] = jnp.concatenate([yh[b, h] for h in heads], axis=-1)
    for k_ in bh:
        s_sc[k_[0], k_[1]] = s[k_]

    ys = [y_chunk[ci] for ci in range(bb * nc)]
    y = jnp.concatenate(ys, axis=0) if len(ys) > 1 else ys[0]
    mu_y = _head_sum1(y, bo) * (1.0 / HD)
    yc = y - mu_y
    var = _head_sum1(yc * yc, bo) * (1.0 / HD)
    yn = yc * lax.rsqrt(var + RW_GN_EPS) * ln_g + ln_b
    bonus = _head_sum1(r * k2 * r_k, bo) * v
    y_ref[...] = ((yn + bonus) * g).reshape(bb, TT, G)

    @pl.when(j == pl.num_programs(1) - 1)
    def _():
        wkv_ref[...] = s_sc[...]


def _rwkv_call(u_rw, shift0, wkv0, mu, vec, wup, aup, gup):
    B, T, _ = u_rw.shape
    L = math.gcd(T, RW_CHUNK)
    bb = math.gcd(B, RW_CHUNKS_PER_STEP)
    nc = math.gcd(T // L, RW_CHUNKS_PER_STEP // bb)
    TT = nc * L
    full = lambda a: pl.BlockSpec(a.shape, lambda b, j: (0,) * a.ndim)
    return pl.pallas_call(
        functools.partial(_rwkv_kernel, L=L, bb=bb, nc=nc),
        grid=(B // bb, T // TT),
        in_specs=[pl.BlockSpec((bb, TT, RW_COLS), lambda b, j: (b, j, 0)),
                  pl.BlockSpec((bb, 1, RW_COLS), lambda b, j: (b, 0, 0)),
                  pl.BlockSpec((bb, N_HEADS, HEAD_DIM, HEAD_DIM), lambda b, j: (b, 0, 0, 0)),
                  full(mu), full(vec), full(wup), full(aup), full(gup)],
        out_specs=[pl.BlockSpec((bb, TT, GROUP_W), lambda b, j: (b, j, 0)),
                   pl.BlockSpec((bb, N_HEADS, HEAD_DIM, HEAD_DIM), lambda b, j: (b, 0, 0, 0))],
        out_shape=[jax.ShapeDtypeStruct((B, T, GROUP_W), F32),
                   jax.ShapeDtypeStruct((B, N_HEADS, HEAD_DIM, HEAD_DIM), F32)],
        scratch_shapes=[pltpu.VMEM((bb, 8 + TT, RW_COLS), F32),
                        pltpu.VMEM((bb, N_HEADS, HEAD_DIM, HEAD_DIM), F32)],
        compiler_params=_params(("arbitrary", "arbitrary")),
        name="rwkv7",
    )(u_rw, shift0, wkv0, mu, vec, wup, aup, gup)


def _mlstm_kernel(u_ref, gt_ref, cb_ref, c0_ref, n0_ref, m0_ref, cw_ref, cbias_ref, gbias_ref, ng_ref,
                  y_ref, c_ref, n_ref, m_ref, zbuf, c_sc, n_sc, m_sc, *, L, bb, nc):
    j = pl.program_id(1)
    G, HD = GROUP_W, HEAD_DIM
    TT = nc * L
    R = bb * TT

    @pl.when(j == 0)
    def _():
        zbuf[:, 5:8, :] = cb_ref[...]
        c_sc[...] = c0_ref[...]
        n_sc[...] = n0_ref[...]
        m_sc[...] = m0_ref[...]

    u3 = u_ref[...]
    zbuf[:, 8:8 + TT, :] = u3[:, :, 0:2 * G]
    conv = cbias_ref[...]
    for t in range(4):
        conv = conv + zbuf[:, 5 + t:5 + t + TT, :] * cw_ref[t:t + 1, :]
    tail = zbuf[:, 5 + TT:8 + TT, :]
    zbuf[:, 5:8, :] = tail
    conv = conv.reshape(R, 2 * G)
    u = u3.reshape(R, ML_MAIN)
    sq = conv * _sigmoid(conv)
    q = sq[:, 0:G]
    k = sq[:, G:2 * G] * (HD ** -0.5)
    v = u[:, 2 * G:3 * G]
    o = u[:, 3 * G:4 * G]

    gates = gt_ref[...].reshape(R, GATE_PAD) + gbias_ref[...]
    lane = _iota((1, GATE_PAD), 1)
    gl = jnp.where(lane < N_HEADS, gates, -_softplus(-gates))
    rr = _iota((R, R), 0)
    rc = _iota((R, R), 1)
    same = (rr // L) == (rc // L)
    bcol = _dot_exact_lhs((same & (rr >= rc)).astype(F32), gl)
    gtot = _dot_exact_lhs(same.astype(F32), gl)
    sel = jnp.concatenate([(same & (rr <= rc)).astype(F32), (rr == rc).astype(F32)], axis=1)
    brow = _dot_exact_rhs(gl, sel, TN)
    causal = _iota((L, L), 0) >= _iota((L, L), 1)

    heads = range(N_HEADS)
    chains = [(ci, h) for ci in range(bb * nc) for h in heads]

    def cut(x, ch):
        ci, h = ch
        return x[ci * L:(ci + 1) * L, h * HD:(h + 1) * HD]

    def rows_of(x, ch, lane0):
        ci, h = ch
        return x[ci * L:(ci + 1) * L, lane0 + h:lane0 + h + 1]

    bc = {ch: rows_of(bcol, ch, N_HEADS) for ch in chains}
    ic = {ch: rows_of(gl, ch, 0) for ch in chains}
    g_tot = {(ci, h): gtot[ci * L:ci * L + 1, N_HEADS + h:N_HEADS + h + 1] for ci, h in chains}
    br = {(ci, h): brow[N_HEADS + h:N_HEADS + h + 1, ci * L:(ci + 1) * L] for ci, h in chains}
    ir = {(ci, h): brow[h:h + 1, R + ci * L:R + (ci + 1) * L] for ci, h in chains}
    q_f = {ch: cut(q, ch) for ch in chains}
    k_f = {ch: cut(k, ch) for ch in chains}
    q_b = {ch: q_f[ch].astype(BF16) for ch in chains}
    k_b = {ch: k_f[ch].astype(BF16) for ch in chains}
    v_f = {ch: cut(v, ch) for ch in chains}
    v_b = {ch: v_f[ch].astype(BF16) for ch in chains}
    log_d = {ch: jnp.where(causal, bc[ch] - br[ch] + ir[ch], -jnp.inf) for ch in chains}
    m_loc = {ch: jnp.max(log_d[ch], -1, keepdims=True) for ch in chains}
    s0 = {ch: _dot(q_b[ch], k_b[ch], NT) * jnp.exp(log_d[ch] - m_loc[ch]) for ch in chains}
    s_sum = {ch: jnp.sum(s0[ch], -1, keepdims=True) for ch in chains}
    sv = {ch: _dot(s0[ch].astype(BF16), v_b[ch]) for ch in chains}
    logw = {ch: g_tot[ch] - bc[ch] + ic[ch] for ch in chains}
    m_w = {ch: jnp.max(logw[ch], 0, keepdims=True) for ch in chains}
    w_loc = {ch: jnp.exp(logw[ch] - m_w[ch]) for ch in chains}
    kv = {ch: _dot((v_f[ch] * w_loc[ch]).astype(BF16), k_b[ch], TN) for ch in chains}
    n_loc = {ch: jnp.sum(k_f[ch] * w_loc[ch], 0, keepdims=True) for ch in chains}

    bh = [(b, h) for b in range(bb) for h in heads]
    m_vec = {b: m_sc[b] for b in range(bb)}
    m = {(b, h): m_vec[b][:, h:h + 1] for b, h in bh}
    c = {k_: c_sc[k_[0], k_[1]] for k_ in bh}
    n = {(b, h): n_sc[b, h:h + 1, :] for b, h in bh}
    h_chunk = {}
    for cc in range(nc):
        ch_of = {(b, h): (b * nc + cc, h) for b, h in bh}
        qc = {k_: _dot(q_b[ch_of[k_]], c[k_].astype(BF16), NT) for k_ in bh}
        qn = {k_: jnp.sum(q_f[ch_of[k_]] * n[k_], -1, keepdims=True) for k_ in bh}
        inter = {k_: bc[ch_of[k_]] + m[k_] for k_ in bh}
        m_row = {k_: jnp.maximum(m_loc[ch_of[k_]], inter[k_]) for k_ in bh}
        m_new = {k_: jnp.maximum(g_tot[ch_of[k_]] + m[k_], m_w[ch_of[k_]]) for k_ in bh}
        dec = {k_: jnp.exp(g_tot[ch_of[k_]] + m[k_] - m_new[k_]) for k_ in bh}
        f2 = {k_: jnp.exp(m_w[ch_of[k_]] - m_new[k_]) for k_ in bh}
        c = {k_: dec[k_] * c[k_] + f2[k_] * kv[ch_of[k_]] for k_ in bh}
        n = {k_: dec[k_] * n[k_] + f2[k_] * n_loc[ch_of[k_]] for k_ in bh}
        m = m_new
        f1 = {k_: jnp.exp(m_loc[ch_of[k_]] - m_row[k_]) for k_ in bh}
        w_int = {k_: jnp.exp(inter[k_] - m_row[k_]) for k_ in bh}
        num = {k_: f1[k_] * sv[ch_of[k_]] + w_int[k_] * qc[k_] for k_ in bh}
        den = {k_: f1[k_] * s_sum[ch_of[k_]] + w_int[k_] * qn[k_] for k_ in bh}
        hh = {k_: num[k_] / jnp.maximum(jnp.abs(den[k_]), jnp.exp(-m_row[k_])) for k_ in bh}
        for b in range(bb):
            h_chunk[b * nc + cc] = jnp.concatenate([hh[b, h] for h in heads], axis=-1)
    for b in range(bb):
        m_out = m_vec[b]
        for h in heads:
            c_sc[b, h] = c[b, h]
            n_sc[b, h:h + 1, :] = n[b, h]
            m_out = jnp.where(lane == h, m[b, h], m_out)
        m_sc[b] = m_out

    hs = [h_chunk[ci] for ci in range(bb * nc)]
    hcat = jnp.concatenate(hs, axis=0) if len(hs) > 1 else hs[0]
    bo = _block_ones()
    mu_h = _head_sum1(hcat, bo) * (1.0 / HD)
    hc = hcat - mu_h
    var = _head_sum1(hc * hc, bo) * (1.0 / HD)
    y_ref[...] = (hc * lax.rsqrt(var + ML_NORM_EPS) * ng_ref[...] * _sigmoid(o)).reshape(bb, TT, G)

    @pl.when(j == pl.num_programs(1) - 1)
    def _():
        c_ref[...] = c_sc[...]
        n_ref[...] = n_sc[...]
        m_ref[...] = m_sc[...]


def _mlstm_call(u_ml, gates, conv0, c0, n0, m0, cw, cbias, gbias, ng):
    B, T, _ = u_ml.shape
    L = math.gcd(T, ML_CHUNK)
    bb = math.gcd(B, ML_CHUNKS_PER_STEP)
    nc = math.gcd(T // L, ML_CHUNKS_PER_STEP // bb)
    TT = nc * L
    full = lambda a: pl.BlockSpec(a.shape, lambda b, j: (0,) * a.ndim)
    st4 = pl.BlockSpec((bb, N_HEADS, HEAD_DIM, HEAD_DIM), lambda b, j: (b, 0, 0, 0))
    st3 = pl.BlockSpec((bb, N_HEADS, HEAD_DIM), lambda b, j: (b, 0, 0))
    stm = pl.BlockSpec((bb, 1, GATE_PAD), lambda b, j: (b, 0, 0))
    return pl.pallas_call(
        functools.partial(_mlstm_kernel, L=L, bb=bb, nc=nc),
        grid=(B // bb, T // TT),
        in_specs=[pl.BlockSpec((bb, TT, ML_MAIN), lambda b, j: (b, j, 0)),
                  pl.BlockSpec((bb, TT, GATE_PAD), lambda b, j: (b, j, 0)),
                  pl.BlockSpec((bb, 3, 2 * GROUP_W), lambda b, j: (b, 0, 0)),
                  st4, st3, stm, full(cw), full(cbias), full(gbias), full(ng)],
        out_specs=[pl.BlockSpec((bb, TT, GROUP_W), lambda b, j: (b, j, 0)), st4, st3, stm],
        out_shape=[jax.ShapeDtypeStruct((B, T, GROUP_W), F32),
                   jax.ShapeDtypeStruct((B, N_HEADS, HEAD_DIM, HEAD_DIM), F32),
                   jax.ShapeDtypeStruct((B, N_HEADS, HEAD_DIM), F32),
                   jax.ShapeDtypeStruct((B, 1, GATE_PAD), F32)],
        scratch_shapes=[pltpu.VMEM((bb, 8 + TT, 2 * GROUP_W), F32),
                        pltpu.VMEM((bb, N_HEADS, HEAD_DIM, HEAD_DIM), F32),
                        pltpu.VMEM((bb, N_HEADS, HEAD_DIM), F32),
                        pltpu.VMEM((bb, 1, GATE_PAD), F32)],
        compiler_params=_params(("arbitrary", "arbitrary")),
        name="mlstm",
    )(u_ml, gates, conv0, c0, n0, m0, cw, cbias, gbias, ng)


def _pool_kernel(u_ref, hist_ref, w_ref, scale_ref, y_ref, zbuf, *, pos0):
    j = pl.program_id(1)
    bb, tt, G = u_ref.shape
    gc = G // len(POOL_WINDOWS)

    @pl.when(j == 0)
    def _():
        zbuf[:, 0:1, :] = jnp.zeros((bb, 1, G), F32)
        zbuf[:, 1:16, :] = hist_ref[...]

    u = u_ref[...]
    zbuf[:, 16:16 + tt, :] = u
    lane = _iota((1, 1, G), 2)
    pos = pos0 + j * tt + _iota((1, tt, 1), 1)
    acc = jnp.zeros((bb, tt, G), F32)
    pooled = jnp.zeros((bb, tt, G), F32)
    for t in range(max(POOL_WINDOWS)):
        acc = acc + zbuf[:, 16 - t:16 - t + tt, :]
        if (t + 1) in POOL_WINDOWS:
            gi = POOL_WINDOWS.index(t + 1)
            cnt = jnp.minimum(pos + 1, t + 1).astype(F32)
            pooled = jnp.where(lane // gc == gi, acc / cnt, pooled)
    tail = zbuf[:, tt:tt + 16, :]
    zbuf[:, 0:16, :] = tail
    pooled = (pooled - u).reshape(bb * tt, G)
    y_ref[...] = (_bdot(pooled, w_ref[...]) * scale_ref[...]).reshape(bb, tt, G)


def _pool_call(u_pool, hist, wbd, scale, pos0):
    B, T, G = u_pool.shape
    bb, tt = _row_blocks(B, T)
    return pl.pallas_call(
        functools.partial(_pool_kernel, pos0=pos0),
        grid=(B // bb, T // tt),
        in_specs=[pl.BlockSpec((bb, tt, G), lambda b, j: (b, j, 0)),
                  pl.BlockSpec((bb, POOL_HIST, G), lambda b, j: (b, 0, 0)),
                  pl.BlockSpec((G, G), lambda b, j: (0, 0)),
                  pl.BlockSpec((1, G), lambda b, j: (0, 0))],
        out_specs=pl.BlockSpec((bb, tt, G), lambda b, j: (b, j, 0)),
        out_shape=jax.ShapeDtypeStruct((B, T, G), F32),
        scratch_shapes=[pltpu.VMEM((bb, 16 + tt, G), F32)],
        compiler_params=_params(("arbitrary", "arbitrary")),
        name="pool",
    )(u_pool, hist, wbd, scale)


def _slope_of(head):
    return jnp.where(head == 0, 2.0 ** -2, jnp.where(head == 1, 2.0 ** -4, jnp.where(head == 2, 2.0 ** -6, 2.0 ** -8)))


def _topk_mask(gate, valid, nblk):
    blk = _iota((1, nblk), 1)
    cnt = jnp.zeros(gate.shape, F32)
    for n in range(nblk):
        gn = gate[:, n:n + 1]
        ahead = (gn > gate) | ((gn == gate) & (n < blk))
        cnt = cnt + jnp.where(ahead, 1.0, 0.0)
    return valid & (cnt < MOBA_TOPK)


def _moba_p_kernel(q_ref, k_ref, vt_ref, o_ref, mean_sc, kb_sc, vt_sc, sel_sc, *, nblk):
    i = pl.program_id(1)
    BLK, G, HD, H = MOBA_BLOCK, GROUP_W, HEAD_DIM, N_HEADS
    W = H * BLK

    SLAB = 2 * HD
    lane_s = _iota((1, SLAB), 1)

    def slab_of(x, h, extra):
        xs = x[:, (h // 2) * SLAB:(h // 2 + 1) * SLAB]
        own = (lane_s // HD) == (h % 2)
        spare = HD if h % 2 == 0 else 0
        return jnp.where(own, xs, jnp.where(lane_s == spare, extra, 0.0))

    @pl.when(i == 0)
    def _():
        pos = _iota((BLK, 1), 0).astype(F32)
        for n in range(nblk):
            kblk = k_ref[0, n * BLK:(n + 1) * BLK, :]
            mean_sc[n:n + 1, :] = jnp.mean(kblk, axis=0, keepdims=True)
            for h in range(H):
                kb_sc[n, h] = slab_of(kblk, h, pos).astype(BF16)
            vt_sc[n] = vt_ref[0, :, n * BLK:(n + 1) * BLK].astype(BF16)

    q = q_ref[0]
    lane_head = _iota((1, G), 1) // HD
    qbd = jnp.concatenate([jnp.where(lane_head == h, q, 0.0) for h in range(H)], axis=0)
    slopes = [2.0 ** (-2.0 * (h + 1)) for h in range(H)]
    scale = HD ** -0.5
    qs = [slab_of(q * scale, h, slopes[h]).astype(BF16) for h in range(H)]

    blk_row = _iota((nblk, 1), 0)
    valid = blk_row < i
    gate = jnp.where(valid, _dot3(mean_sc[...], qbd, NT), -jnp.inf)
    cnt = jnp.zeros((nblk, W), F32)
    for n in range(nblk):
        gn = gate[n:n + 1, :]
        ahead = (gn > gate) | ((gn == gate) & (n < blk_row))
        cnt = cnt + jnp.where(ahead, 1.0, 0.0)
    sel = jnp.where(valid & (cnt < MOBA_TOPK), 1.0, 0.0)
    for n in range(nblk):
        sel_sc[n] = sel[n:n + 1, :]

    heads = range(H)

    def scores(jb):
        return [_dot(kb_sc[jb, h], qs[h], NT) for h in heads]

    def pv(jb, p):
        vt = vt_sc[jb]
        return [_dot(vt[h * HD:(h + 1) * HD, :], p[h].astype(BF16)) for h in heads]

    causal = _iota((BLK, BLK), 0) <= _iota((BLK, BLK), 1)
    s = [jnp.where(causal, x, -jnp.inf) for x in scores(i)]
    m0 = [jnp.max(x, 0, keepdims=True) for x in s]
    p = [jnp.exp(s[h] - m0[h]) for h in heads]
    l0 = [jnp.sum(x, 0, keepdims=True) for x in p]
    acc0 = pv(i, p)

    def update(jbs, carry):
        m, l, acc = carry
        s = []
        for jb in jbs:
            s_raw = scores(jb)
            off = ((i - jb) * BLK).astype(F32)
            s.append([s_raw[h] + jnp.where(sel_sc[jb, :, h * BLK:(h + 1) * BLK] > 0.0, -slopes[h] * off, -jnp.inf)
                      for h in heads])
        m2 = m
        for x in s:
            m2 = [jnp.maximum(m2[h], jnp.max(x[h], 0, keepdims=True)) for h in heads]
        alpha = [jnp.exp(m[h] - m2[h]) for h in heads]
        p = [jnp.concatenate([jnp.exp(x[h] - m2[h]) for x in s], axis=0) for h in heads]
        vt = [jnp.concatenate([vt_sc[jb][h * HD:(h + 1) * HD, :] for jb in jbs], axis=1) for h in heads]
        new = [_dot(vt[h], p[h].astype(BF16)) for h in heads]
        l = [alpha[h] * l[h] + jnp.sum(p[h], 0, keepdims=True) for h in heads]
        acc = [alpha[h] * acc[h] + new[h] for h in heads]
        return m2, l, acc

    carry = lax.fori_loop(0, i // 2, lambda jp, c: update((2 * jp, 2 * jp + 1), c), (m0, l0, acc0))
    m, l, acc = lax.fori_loop(0, i % 2, lambda _, c: update((i - 1,), c), carry)
    out_t = jnp.concatenate([acc[h] / l[h] for h in heads], axis=0)
    o_ref[0] = out_t.T


def _moba_prompt_call(q, k, v_t, layer):
    B, T, G = q.shape
    assert T % MOBA_BLOCK == 0
    nblk = T // MOBA_BLOCK
    return pl.pallas_call(
        functools.partial(_moba_p_kernel, nblk=nblk),
        grid=(B, nblk),
        in_specs=[pl.BlockSpec((1, MOBA_BLOCK, G), lambda b, i: (b, i, 0)),
                  pl.BlockSpec((1, T, G), lambda b, i: (b, 0, 0)),
                  pl.BlockSpec((None, 1, G, T), lambda b, i: (layer, b, 0, 0))],
        out_specs=pl.BlockSpec((1, MOBA_BLOCK, G), lambda b, i: (b, i, 0)),
        out_shape=jax.ShapeDtypeStruct((B, T, G), F32),
        scratch_shapes=[pltpu.VMEM((nblk, G), F32),
                        pltpu.VMEM((nblk, N_HEADS, MOBA_BLOCK, 2 * HEAD_DIM), BF16),
                        pltpu.VMEM((nblk, G, MOBA_BLOCK), BF16),
                        pltpu.VMEM((nblk, 1, N_HEADS * MOBA_BLOCK), F32)],
        compiler_params=_params(("arbitrary", "arbitrary")),
        name="moba_prompt",
    )(q, k, v_t)


def _moba_s_kernel(pt_ref, q_ref, kn_ref, vn_ref, *refs, n_pages, page, past_len, rb):
    kp = refs[:rb * n_pages]
    vp = refs[rb * n_pages:2 * rb * n_pages]
    o_ref = refs[2 * rb * n_pages]
    G, HD = GROUP_W, HEAD_DIM
    ts = q_ref.shape[1]
    R = N_HEADS * ts
    ppb = MOBA_BLOCK // page
    nb = n_pages // ppb
    rows = range(rb)
    pages = range(n_pages)
    lane_head = _iota((1, G), 1) // HD
    row_head = _iota((R, 1), 0) // ts
    row_t = _iota((R, 1), 0) % ts
    slope = _slope_of(row_head)
    scale = HD ** -0.5
    qbd = [jnp.concatenate([jnp.where(lane_head == h, q_ref[r], 0.0) for h in range(N_HEADS)], axis=0)
           for r in rows]
    qb = [x.astype(BF16) for x in qbd]

    kpg = {(r, p): kp[r * n_pages + p][...] for r in rows for p in pages}
    raw = {(r, p): _dot(qb[r], kpg[r, p].astype(BF16)) for r in rows for p in pages}
    rsum = {(r, p): jnp.sum(raw[r, p], axis=1, keepdims=True) for r in rows for p in pages}
    gate = [jnp.concatenate(
        [sum(rsum[r, p] for p in range(n * ppb, (n + 1) * ppb)) * (1.0 / MOBA_BLOCK) for n in range(nb)], axis=1)
        for r in rows]
    all_valid = _iota((1, nb), 1) >= 0
    sel = [jnp.where(_topk_mask(gate[r], all_valid, nb), 1.0, 0.0) for r in rows]

    q_pos = (past_len + row_t).astype(F32)
    off = _iota((1, page), 1).astype(F32)
    scores = {(r, p): jnp.where(sel[r][:, p // ppb:p // ppb + 1] > 0.0,
                                raw[r, p] * scale - slope * (q_pos - (p * page + off)), -jnp.inf)
              for r in rows for p in pages}
    t_new = _iota((1, ts), 1)
    s_own = [jnp.where(t_new <= row_t,
                       _dot(qb[r], kn_ref[r].astype(BF16), NT) * scale - slope * (row_t - t_new).astype(F32),
                       -jnp.inf) for r in rows]

    m = [jnp.max(s_own[r], -1, keepdims=True) for r in rows]
    for p in pages:
        m = [jnp.maximum(m[r], jnp.max(scores[r, p], -1, keepdims=True)) for r in rows]
    p_own = [jnp.exp(s_own[r] - m[r]) for r in rows]
    l = [jnp.sum(p_own[r], -1, keepdims=True) for r in rows]
    acc = [_dot(p_own[r].astype(BF16), vn_ref[r].astype(BF16)) for r in rows]
    for p in pages:
        pr = [jnp.exp(scores[r, p] - m[r]) for r in rows]
        l = [l[r] + jnp.sum(pr[r], -1, keepdims=True) for r in rows]
        acc = [acc[r] + _dot(pr[r].astype(BF16), vp[r * n_pages + p][...].astype(BF16), NT) for r in rows]
    for r in rows:
        out = acc[r] / l[r]
        y = jnp.zeros((ts, G), F32)
        for h in range(N_HEADS):
            y = jnp.where(lane_head == h, out[h * ts:(h + 1) * ts, :], y)
        o_ref[r] = y


def _moba_sample_call(q, k, v, cache_k, cache_v, page_table, layer):
    DB, TS, G = q.shape
    Ld, n_phys, page, H, d = cache_k.shape
    n_pages = page_table.shape[1]
    past_len = n_pages * page
    assert past_len % MOBA_BLOCK == 0 and MOBA_BLOCK % page == 0 and past_len // MOBA_BLOCK >= MOBA_TOPK
    ck = jnp.transpose(cache_k, (0, 1, 3, 4, 2)).reshape(Ld, n_phys, H * d, page)
    cv = jnp.transpose(cache_v, (0, 1, 3, 4, 2)).reshape(Ld, n_phys, H * d, page)
    rb = math.gcd(DB, MOBA_SAMPLE_ROWS_PER_STEP)
    new = pl.BlockSpec((rb, TS, G), lambda b, pt: (b, 0, 0))
    pg = [pl.BlockSpec((None, None, G, page),
                       functools.partial(lambda b, pt, r, p: (layer, pt[b * rb + r, p], 0, 0), r=r, p=p))
          for r in range(rb) for p in range(n_pages)]
    return pl.pallas_call(
        functools.partial(_moba_s_kernel, n_pages=n_pages, page=page, past_len=past_len, rb=rb),
        grid_spec=pltpu.PrefetchScalarGridSpec(
            num_scalar_prefetch=1, grid=(DB // rb,),
            in_specs=[new, new, new] + pg + pg,
            out_specs=new),
        out_shape=jax.ShapeDtypeStruct((DB, TS, G), F32),
        compiler_params=_params(("arbitrary",)),
        name="moba_sample",
    )(page_table, q, k, v, *([ck] * (rb * n_pages)), *([cv] * (rb * n_pages)))


def _out_kernel(yrw_ref, yml_ref, ypl_ref, yat_ref, x_ref, g1_ref, sc2_ref, sh2_ref, g2_ref, n2g_ref, fng_ref,
                wout_ref, wup_ref, wdn_ref, o_ref, h2_sc, acc_sc, *, final):
    f = pl.program_id(2)
    bb, tt, D = x_ref.shape
    M = bb * tt

    @pl.when(f == 0)
    def _():
        ycat = jnp.concatenate([r[...].reshape(M, GROUP_W) for r in (yrw_ref, yml_ref, ypl_ref, yat_ref)], axis=-1)
        y = _dot(ycat.astype(BF16), wout_ref[...]).reshape(bb, tt, D)
        x1 = x_ref[...] + g1_ref[...] * y
        o_ref[...] = x1
        hn = x1 * lax.rsqrt(jnp.mean(x1 * x1, -1, keepdims=True) + NORM_EPS) * n2g_ref[...]
        h2_sc[...] = (hn * (1.0 + sc2_ref[...]) + sh2_ref[...]).reshape(M, D).astype(BF16)
        acc_sc[...] = jnp.zeros((M, D), F32)

    a = _dot(h2_sc[...], wup_ref[...])
    a = jnp.square(jnp.maximum(a, 0.0)).astype(BF16)
    acc_sc[...] += _dot(a, wdn_ref[...])

    @pl.when(f == pl.num_programs(2) - 1)
    def _():
        x2 = o_ref[...] + g2_ref[...] * acc_sc[...].reshape(bb, tt, D)
        if final:
            x2 = x2 * lax.rsqrt(jnp.mean(x2 * x2, -1, keepdims=True) + NORM_EPS) * fng_ref[...]
        o_ref[...] = x2


def _out_call(ys, x, g1, sc2, sh2, g2, n2g, fng, wout, wup, wdn, layer, final):
    B, T, D = x.shape
    bb, tt = _row_blocks(B, T, OUT_ROW_TILE)
    FF = wup.shape[2]
    row = lambda n: pl.BlockSpec((bb, tt, n), lambda b, j, f: (b, j, 0))
    mod = pl.BlockSpec((bb, 1, D), lambda b, j, f: (b, 0, 0))
    vec = pl.BlockSpec((1, D), lambda b, j, f: (0, 0))
    return pl.pallas_call(
        functools.partial(_out_kernel, final=final),
        grid=(B // bb, T // tt, FF // FF_TILE),
        in_specs=[row(GROUP_W)] * 4 + [row(D), mod, mod, mod, mod, vec, vec,
                  pl.BlockSpec((None, D, D), lambda b, j, f: (layer, 0, 0)),
                  pl.BlockSpec((None, D, FF_TILE), lambda b, j, f: (layer, 0, f)),
                  pl.BlockSpec((None, FF_TILE, D), lambda b, j, f: (layer, f, 0))],
        out_specs=row(D),
        out_shape=jax.ShapeDtypeStruct((B, T, D), F32),
        scratch_shapes=[pltpu.VMEM((bb * tt, D), BF16),
                        pltpu.VMEM((bb * tt, D), F32)],
        compiler_params=_params(("arbitrary", "arbitrary", "arbitrary"), OUT_VMEM_LIMIT),
        name="out_mlp",
    )(*ys, x, g1, sc2, sh2, g2, n2g, fng, wout, wup, wdn)


def _rw_perm(x):
    return jnp.concatenate([x[..., 0:256], x[..., 320:576], x[..., 576:832],
                            x[..., 256:320], x[..., 832:896], x[..., 896:1024]], axis=-1)


def _rw_unperm(x):
    return jnp.concatenate([x[..., 0:256], x[..., 768:832], x[..., 256:512],
                            x[..., 512:768], x[..., 832:896], x[..., 896:1024]], axis=-1)


def _layer(x, mods, st, pos0, attend, lw, final, transposed_kv, prev_kv):
    sh1, sc1, g1, sh2, sc2, g2 = mods
    shift0, wkv0, conv0, c0, n0, m0, pool0 = st
    B, T, D = x.shape
    outs = _in_call(x, sc1, sh1, lw['norm1_g'], lw['wpack'], lw['wkv_t'], transposed_kv,
                    lw['layer'], lw['depth'], prev_kv)
    u_rw, u_ml, gates, u_pool, q, k = outs[:6]

    y_rw, wkv1 = _rwkv_call(u_rw, _rw_perm(shift0)[:, None, :], wkv0, lw['rw_mu'], lw['rw_vec'],
                            lw['rwkv_w_up'], lw['rwkv_a_up'], lw['rwkv_g_up'])
    shift1 = _rw_unperm(u_rw[:, -1, :])

    m0p = jnp.pad(m0, ((0, 0), (0, GATE_PAD - N_HEADS)))[:, None, :]
    y_ml, c1, n1, m1p = _mlstm_call(u_ml, gates, conv0, c0, n0, m0p, lw['mlstm_conv_w'], lw['ml_cbias'],
                                    lw['ml_gbias'], lw['ml_ng'])
    m1 = m1p[:, 0, :N_HEADS]
    zc = u_ml[:, :, :2 * GROUP_W] if T >= 3 else jnp.concatenate([conv0, u_ml[:, :, :2 * GROUP_W]], 1)
    conv1 = zc[:, -3:]

    y_pool = _pool_call(u_pool, pool0, lw['pool_wbd'], lw['pool_scale'], pos0)
    zp = u_pool if T >= POOL_HIST else jnp.concatenate([pool0, u_pool], 1)
    pool1 = zp[:, -POOL_HIST:]

    if transposed_kv:
        k_out, v_out = outs[6], outs[7]
        y_at = attend(q, k, v_out)
    else:
        kv_shape = (B, T, N_HEADS, HEAD_DIM)
        y_at = attend(q, k, outs[6])
        k_out, v_out = k.reshape(kv_shape), outs[6].reshape(kv_shape)

    x_new = _out_call((y_rw, y_ml, y_pool, y_at), x, g1, sc2, sh2, g2, lw['norm2_g'], lw['final_g'],
                      lw['w_out'], lw['mlp_up'], lw['mlp_down'], lw['layer'], final)
    return x_new, (shift1, wkv1, conv1, c1, n1, m1, pool1, k_out, v_out)


def kernel(x_prompt, x_sample, c_prompt, c_sample, state_rwkv_shift, state_rwkv_wkv, state_mlstm_conv, state_mlstm_c, state_mlstm_n, state_mlstm_m, state_pool, cache_k, cache_v, page_table, ada_w, ada_b, norm1_g, norm2_g, w_in, w_out, rwkv_mu, rwkv_w0, rwkv_w_up, rwkv_a0, rwkv_a_up, rwkv_g_up, rwkv_k_k, rwkv_k_a, rwkv_r_k, rwkv_ln_g, rwkv_ln_b, mlstm_conv_w, mlstm_conv_b, mlstm_i_b, mlstm_f_b, mlstm_norm_g, pool_w, pool_scale, mlp_up, mlp_down, final_norm_g):
    B, T, D = x_prompt.shape
    DB = x_sample.shape[0]
    depth = ada_w.shape[0]
    G = GROUP_W
    assert D == D_MODEL and w_in.shape[-1] == N_IN

    mod = _ada_call(jnp.concatenate([c_prompt, c_sample], 0), ada_w, ada_b)

    st_p0 = (jnp.zeros((B, RW_COLS), F32), jnp.zeros((B, N_HEADS, HEAD_DIM, HEAD_DIM), F32),
             jnp.zeros((B, 3, 2 * G), F32), jnp.zeros((B, N_HEADS, HEAD_DIM, HEAD_DIM), F32),
             jnp.zeros((B, N_HEADS, HEAD_DIM), F32), jnp.zeros((B, N_HEADS), F32),
             jnp.zeros((B, POOL_HIST, G), F32))
    past_len = page_table.shape[1] * cache_k.shape[2]

    w_out_b, mlp_up_b, mlp_down_b = w_out.astype(BF16), mlp_up.astype(BF16), mlp_down.astype(BF16)
    xp, xs = x_prompt, x_sample
    new_p, new_s = [], []
    kv_p = None
    for l in range(depth):
        wl = w_in[l]
        wpack = jnp.concatenate([
            _rw_perm(wl[:, :RW_COLS]),
            wl[:, RW_COLS:RW_COLS + ML_MAIN],
            jnp.pad(wl[:, RW_COLS + ML_MAIN:RW_COLS + ML_MAIN + ML_GATES], ((0, 0), (0, GATE_PAD - ML_GATES))),
            wl[:, RW_COLS + ML_MAIN + ML_GATES:],
        ], axis=1).astype(BF16)
        pool_wbd = jnp.zeros((G, G), F32)
        gcw = G // len(POOL_WINDOWS)
        for gi in range(len(POOL_WINDOWS)):
            pool_wbd = pool_wbd.at[gi * gcw:(gi + 1) * gcw, gi * gcw:(gi + 1) * gcw].set(pool_w[l, gi])
        row = lambda a: a.reshape(1, -1)
        lw = {
            'norm1_g': row(norm1_g[l]), 'norm2_g': row(norm2_g[l]), 'final_g': row(final_norm_g),
            'wpack': wpack, 'wkv_t': wl[:, N_IN - 2 * G:].T.astype(BF16), 'layer': l, 'depth': depth,
            'w_out': w_out_b, 'mlp_up': mlp_up_b, 'mlp_down': mlp_down_b,
            'rw_mu': row(_rw_perm(rwkv_mu[l])),
            'rw_vec': jnp.stack([rwkv_w0[l], rwkv_a0[l], rwkv_k_k[l], rwkv_k_a[l], rwkv_ln_g[l], rwkv_ln_b[l],
                                 rwkv_r_k[l].reshape(-1), jnp.zeros((G,), F32)]),
            'rwkv_w_up': rwkv_w_up[l], 'rwkv_a_up': rwkv_a_up[l], 'rwkv_g_up': rwkv_g_up[l],
            'mlstm_conv_w': mlstm_conv_w[l], 'ml_cbias': row(mlstm_conv_b[l]),
            'ml_gbias': row(jnp.pad(jnp.concatenate([mlstm_i_b[l], mlstm_f_b[l]]), (0, GATE_PAD - ML_GATES))),
            'ml_ng': row(mlstm_norm_g[l]),
            'pool_wbd': pool_wbd, 'pool_scale': row(pool_scale[l]),
        }
        mods = [mod[l, :, i * D:(i + 1) * D][:, None, :] for i in range(6)]
        mods_p = [m[:B] for m in mods]
        mods_s = [m[B:] for m in mods]
        final = l == depth - 1

        xp, st_p = _layer(xp, mods_p, st_p0, 0, functools.partial(_moba_prompt_call, layer=l), lw, final, True,
                          kv_p)
        kv_p = st_p[7:9]
        st_s_in = (state_rwkv_shift[l], state_rwkv_wkv[l], state_mlstm_conv[l], state_mlstm_c[l],
                   state_mlstm_n[l], state_mlstm_m[l], state_pool[l])
        attend_s = functools.partial(_moba_sample_call, cache_k=cache_k, cache_v=cache_v,
                                     page_table=page_table, layer=l)
        xs, st_s = _layer(xs, mods_s, st_s_in, past_len, attend_s, lw, final, False, None)
        new_p.append(st_p)
        new_s.append(st_s)

    stack = lambda lst, n: tuple(jnp.stack([st[i] for st in lst]) for i in range(n))
    kv_out = tuple(jnp.transpose(a.reshape(depth, B, N_HEADS, HEAD_DIM, T), (0, 1, 4, 2, 3)) for a in kv_p)
    return (xp, xs) + stack(new_p, 7) + kv_out + stack(new_s, 9)
```

```python
import functools
import math

import jax
import jax.numpy as jnp
from jax import lax
from jax.experimental import pallas as pl
from jax.experimental.pallas import tpu as pltpu

F32 = jnp.float32
BF16 = jnp.bfloat16

D_MODEL = 1024
GROUP_W = 256
HEAD_DIM = 64
N_HEADS = 4
RW_COLS = 1024
ML_MAIN = 1024
ML_GATES = 8
GATE_PAD = 128
N_IN = 3080
MOBA_BLOCK = 256
MOBA_TOPK = 3
ML_CHUNK = 256
RW_CHUNK = 64
RW_CHUNKS_PER_STEP = 8
ML_CHUNKS_PER_STEP = 4
MOBA_SAMPLE_ROWS_PER_STEP = 4
POOL_WINDOWS = (2, 4, 8, 16)
POOL_HIST = 15
NORM_EPS = 1e-6
RW_GN_EPS = 64e-5
ML_NORM_EPS = 1e-6
ROW_TILE = 512
FF_TILE = 1024
VMEM_LIMIT = 48 * 1024 * 1024
OUT_ROW_TILE = 1024
OUT_VMEM_LIMIT = 60 * 1024 * 1024

NN = (((1,), (0,)), ((), ()))
NT = (((1,), (1,)), ((), ()))
TN = (((0,), (0,)), ((), ()))


def _dot(a, b, dn=NN):
    return lax.dot_general(a, b, dn, preferred_element_type=F32)


def _bdot(a, b, dn=NN):
    return _dot(a.astype(BF16), b.astype(BF16), dn)


def _split2(x):
    hi = x.astype(BF16)
    lo = (x - hi.astype(F32)).astype(BF16)
    return hi, lo


def _split3(x):
    hi = x.astype(BF16)
    r = x - hi.astype(F32)
    mid = r.astype(BF16)
    lo = (r - mid.astype(F32)).astype(BF16)
    return hi, mid, lo


def _dot3(a, b, dn=NN):
    ah, al = _split2(a)
    bh, bl = _split2(b)
    return _dot(ah, bh, dn) + (_dot(ah, bl, dn) + _dot(al, bh, dn))


def _dot3s(a_s, b_s, dn=NN):
    return _dot(a_s[0], b_s[0], dn) + (_dot(a_s[0], b_s[1], dn) + _dot(a_s[1], b_s[0], dn))


def _dot_exact_rhs(a, b01, dn=NN):
    h, m, l = _split3(a)
    b = b01.astype(BF16)
    return _dot(h, b, dn) + (_dot(m, b, dn) + _dot(l, b, dn))


def _dot_exact_lhs(a01, b, dn=NN):
    h, m, l = _split3(b)
    a = a01.astype(BF16)
    return _dot(a, h, dn) + (_dot(a, m, dn) + _dot(a, l, dn))


def _iota(shape, dim):
    return lax.broadcasted_iota(jnp.int32, shape, dim)


def _sigmoid(x):
    return jax.nn.sigmoid(x)


def _softplus(x):
    return jnp.maximum(x, 0.0) + jnp.log(1.0 + jnp.exp(-jnp.abs(x)))


def _block_ones():
    return (_iota((GROUP_W, GROUP_W), 0) // HEAD_DIM == _iota((GROUP_W, GROUP_W), 1) // HEAD_DIM).astype(F32)


def _head_sum(x, bo):
    return _dot_exact_rhs(x, bo)


def _head_sum1(x, bo):
    return _bdot(x, bo)


def _params(sem, vmem_limit=VMEM_LIMIT):
    return pltpu.CompilerParams(dimension_semantics=sem, vmem_limit_bytes=vmem_limit)


def _row_blocks(B, T, tile=ROW_TILE):
    if T >= tile:
        assert T % tile == 0
        return 1, tile
    bb = max(1, min(B, tile // T))
    while B % bb:
        bb -= 1
    return bb, T


def _ada_kernel(c_ref, w_ref, b_ref, o_ref):
    c = c_ref[...]
    o_ref[...] = _bdot(c * _sigmoid(c), w_ref[...]) + b_ref[...]


def _ada_call(c_all, ada_w, ada_b):
    Ld, D, N6 = ada_w.shape
    NB = c_all.shape[0]
    tn = 1024
    return pl.pallas_call(
        _ada_kernel,
        grid=(Ld, N6 // tn),
        in_specs=[pl.BlockSpec((NB, D), lambda l, j: (0, 0)),
                  pl.BlockSpec((None, D, tn), lambda l, j: (l, 0, j)),
                  pl.BlockSpec((None, 1, tn), lambda l, j: (l, 0, j))],
        out_specs=pl.BlockSpec((None, NB, tn), lambda l, j: (l, 0, j)),
        out_shape=jax.ShapeDtypeStruct((Ld, NB, N6), F32),
        compiler_params=_params(("arbitrary", "arbitrary")),
        name="ada_mod",
    )(c_all, ada_w, ada_b.reshape(Ld, 1, N6))


IN_WIDTHS = (RW_COLS, ML_MAIN, GATE_PAD, GROUP_W, GROUP_W, GROUP_W, GROUP_W)


def _in_kernel(x_ref, sc_ref, sh_ref, g_ref, w_ref, wt_ref, *rest, n_rows_out, n_alias):
    out_refs = rest[n_alias:]
    x = x_ref[...]
    bb, tt, D = x.shape
    y = x * lax.rsqrt(jnp.mean(x * x, -1, keepdims=True) + NORM_EPS) * g_ref[...]
    h = y * (1.0 + sc_ref[...]) + sh_ref[...]
    hb = h.reshape(bb * tt, D).astype(BF16)
    off = 0
    for ref in out_refs[:n_rows_out]:
        n = ref.shape[-1]
        ref[...] = _dot(hb, w_ref[:, off:off + n]).reshape(bb, tt, n)
        off += n
    for i, ref in enumerate(out_refs[n_rows_out:]):
        ref[0] = _dot(wt_ref[i * GROUP_W:(i + 1) * GROUP_W, :], hb, NT)


def _in_call(x, sc, sh, g, wpack, wkv_t, transposed_kv, layer, depth, prev_kv):
    B, T, D = x.shape
    bb, tt = _row_blocks(B, T)
    row = lambda n: pl.BlockSpec((bb, tt, n), lambda b, j: (b, j, 0))
    mod = pl.BlockSpec((bb, 1, D), lambda b, j: (b, 0, 0))
    widths = IN_WIDTHS[:-1] if transposed_kv else IN_WIDTHS
    out_specs = [row(n) for n in widths]
    out_shape = [jax.ShapeDtypeStruct((B, T, n), F32) for n in widths]
    inputs = [x, sc, sh, g, wpack, wkv_t]
    in_specs = [row(D), mod, mod,
                pl.BlockSpec((1, D), lambda b, j: (0, 0)),
                pl.BlockSpec(wpack.shape, lambda b, j: (0, 0)),
                pl.BlockSpec(wkv_t.shape, lambda b, j: (0, 0))]
    aliases = {}
    if transposed_kv:
        assert bb == 1
        out_specs += [pl.BlockSpec((None, 1, GROUP_W, tt), lambda b, j: (layer, b, 0, j))] * 2
        out_shape += [jax.ShapeDtypeStruct((depth, B, GROUP_W, T), F32)] * 2
        if prev_kv is not None:
            aliases = {len(inputs) + i: len(widths) + i for i in range(2)}
            inputs += list(prev_kv)
            in_specs += [pl.BlockSpec(memory_space=pl.ANY)] * 2
    return pl.pallas_call(
        functools.partial(_in_kernel, n_rows_out=len(widths), n_alias=len(aliases)),
        grid=(B // bb, T // tt),
        in_specs=in_specs,
        out_specs=out_specs,
        out_shape=out_shape,
        input_output_aliases=aliases,
        compiler_params=_params(("arbitrary", "arbitrary")),
        name="in_proj",
    )(*inputs)


def _rwkv_kernel(u_ref, s0_ref, wkv0_ref, mu_ref, vec_ref, wup_ref, aup_ref, gup_ref, *rest, L, bb, nc):
    y_ref, wkv_ref, zbuf, s_sc = rest[-4:]
    j = pl.program_id(1)
    G, HD = GROUP_W, HEAD_DIM
    TT = nc * L
    R = bb * TT

    @pl.when(j == 0)
    def _():
        zbuf[:, 7:8, :] = s0_ref[...]
        s_sc[...] = wkv0_ref[...]

    u3 = u_ref[...]
    zbuf[:, 8:8 + TT, :] = u3
    prev = zbuf[:, 7:7 + TT, :]
    xs = (u3 + (prev - u3) * mu_ref[...]).reshape(R, RW_COLS)
    zbuf[:, 7:8, :] = u3[:, TT - 1:TT, :]

    r = xs[:, 0:G]
    k = xs[:, G:2 * G]
    v = xs[:, 2 * G:3 * G]
    wc = xs[:, 3 * G:3 * G + 64]
    ac = xs[:, 3 * G + 64:3 * G + 128]
    gc = xs[:, 3 * G + 128:4 * G]
    w0, a0, k_k, k_a = vec_ref[0:1, :], vec_ref[1:2, :], vec_ref[2:3, :], vec_ref[3:4, :]
    ln_g, ln_b, r_k = vec_ref[4:5, :], vec_ref[5:6, :], vec_ref[6:7, :]

    w_log = -_softplus(-(w0 + _bdot(jnp.tanh(wc), wup_ref[...]))) - 0.5
    lw = -jnp.exp(w_log)
    a = _sigmoid(a0 + _bdot(ac, aup_ref[...]))
    g = _bdot(_sigmoid(gc), gup_ref[...])
    bo = _block_ones()
    kk = k * k_k
    kk = kk / jnp.maximum(jnp.sqrt(_head_sum(kk * kk, bo)), 1e-12)
    k2 = k * (1.0 + (a - 1.0) * k_a)
    beta = kk * a

    row = _iota((L, L), 0)
    col = _iota((L, L), 1)
    incl = row >= col
    strict = row > col
    eye = (row == col).astype(F32)
    groups = range(bb * nc)
    c_parts = [_dot_exact_lhs(incl.astype(F32), lw[ci * L:(ci + 1) * L]) for ci in groups]
    c = jnp.concatenate(c_parts, axis=0) if len(c_parts) > 1 else c_parts[0]
    cl_parts = [jnp.broadcast_to(x[L - 1:L, :], (L, G)) for x in c_parts]
    cl = jnp.concatenate(cl_parts, axis=0) if len(cl_parts) > 1 else cl_parts[0]
    e_inv = jnp.exp(-c)
    e_tail = jnp.exp(cl - c)
    a_t = -kk * jnp.exp(c - lw)
    b_t = beta * e_inv
    k_t = k2 * e_inv
    r_t = r * jnp.exp(c)
    b_l = beta * e_tail
    k_l = k2 * e_tail
    e_cl = jnp.exp(cl)
    n_double = max(1, int(math.ceil(math.log2(L)))) - 1

    chains = [(ci, h) for ci in range(bb * nc) for h in range(N_HEADS)]

    def cut(x, ch):
        ci, h = ch
        return x[ci * L:(ci + 1) * L, h * HD:(h + 1) * HD]

    a_s = {ch: _split2(cut(a_t, ch)) for ch in chains}
    b_s = {ch: _split2(cut(b_t, ch)) for ch in chains}
    r_b = {ch: cut(r_t, ch).astype(BF16) for ch in chains}
    k_b = {ch: cut(k_t, ch).astype(BF16) for ch in chains}
    v_b = {ch: cut(v, ch).astype(BF16) for ch in chains}
    n_ab = {ch: jnp.where(strict, _dot3s(a_s[ch], b_s[ch], NT), 0.0) for ch in chains}
    n_ak = {ch: jnp.where(strict, _dot(a_s[ch][0], k_b[ch], NT), 0.0).astype(BF16) for ch in chains}
    n_rb = {ch: jnp.where(incl, _dot(r_b[ch], b_s[ch][0], NT), 0.0).astype(BF16) for ch in chains}
    n_rk = {ch: jnp.where(incl, _dot(r_b[ch], k_b[ch], NT), 0.0).astype(BF16) for ch in chains}
    p = {ch: eye + n_ab[ch] for ch in chains}
    m_s = {ch: _split2(n_ab[ch]) for ch in chains}
    for _ in range(n_double):
        m_s = {ch: _split2(_dot3s(m_s[ch], m_s[ch])) for ch in chains}
        p = {ch: p[ch] + _dot3s(m_s[ch], _split2(p[ch])) for ch in chains}
    p_b = {ch: p[ch].astype(BF16) for ch in chains}
    akv = {ch: _dot(n_ak[ch], v_b[ch]).astype(BF16) for ch in chains}
    y_v = {ch: _dot(n_rk[ch], v_b[ch]) for ch in chains}
    w_b = {ch: _dot(p_b[ch], a_s[ch][0]).astype(BF16) for ch in chains}
    u_t = {ch: _dot(p_b[ch], akv[ch]) for ch in chains}
    bl_b = {ch: cut(b_l, ch).astype(BF16) for ch in chains}
    kl_b = {ch: cut(k_l, ch).astype(BF16) for ch in chains}

    heads = range(N_HEADS)
    bh = [(b, h) for b in range(bb) for h in heads]
    s = {k_: s_sc[k_[0], k_[1]] for k_ in bh}
    y_chunk = {}
    for cc in range(nc):
        ch_of = {(b, h): (b * nc + cc, h) for b, h in bh}
        s_b = {k_: s[k_].astype(BF16) for k_ in bh}
        e_b = {k_: (_dot(w_b[ch_of[k_]], s_b[k_], NT) + u_t[ch_of[k_]]).astype(BF16) for k_ in bh}
        s = {k_: s[k_] * cut(e_cl, ch_of[k_])[0:1, :]
             + _dot(e_b[k_], bl_b[ch_of[k_]], TN) + _dot(v_b[ch_of[k_]], kl_b[ch_of[k_]], TN) for k_ in bh}
        yh = {k_: _dot(r_b[ch_of[k_]], s_b[k_], NT) + _dot(n_rb[ch_of[k_]], e_b[k_]) + y_v[ch_of[k_]] for k_ in bh}
        for b in range(bb):
            y_chunk[b * nc + cc] = jnp.concatenate([yh[b, h] for h in heads], axis=-1)
    for k_ in bh:
        s_sc[k_[0], k_[1]] = s[k_]

    ys = [y_chunk[ci] for ci in range(bb * nc)]
    y = jnp.concatenate(ys, axis=0) if len(ys) > 1 else ys[0]
    mu_y = _head_sum1(y, bo) * (1.0 / HD)
    yc = y - mu_y
    var = _head_sum1(yc * yc, bo) * (1.0 / HD)
    yn = yc * lax.rsqrt(var + RW_GN_EPS) * ln_g + ln_b
    bonus = _head_sum1(r * k2 * r_k, bo) * v
    y_ref[...] = ((yn + bonus) * g).reshape(bb, TT, G)

    @pl.when(j == pl.num_programs(1) - 1)
    def _():
        wkv_ref[...] = s_sc[...]


def _state_slot(B, bb, layer, depth, prev, n_inputs, out_index):
    spec = pl.BlockSpec((None, bb, N_HEADS, HEAD_DIM, HEAD_DIM), lambda b, j: (layer, b, 0, 0, 0))
    shape = jax.ShapeDtypeStruct((depth, B, N_HEADS, HEAD_DIM, HEAD_DIM), F32)
    if prev is None:
        return spec, shape, [], [], {}
    return spec, shape, [prev], [pl.BlockSpec(memory_space=pl.ANY)], {n_inputs: out_index}


def _rwkv_call(u_rw, shift0, wkv0, mu, vec, wup, aup, gup, layer, depth, prev_wkv):
    B, T, _ = u_rw.shape
    L = math.gcd(T, RW_CHUNK)
    bb = math.gcd(B, RW_CHUNKS_PER_STEP)
    nc = math.gcd(T // L, RW_CHUNKS_PER_STEP // bb)
    TT = nc * L
    full = lambda a: pl.BlockSpec(a.shape, lambda b, j: (0,) * a.ndim)
    inputs = [u_rw, shift0, wkv0, mu, vec, wup, aup, gup]
    st_spec, st_shape, extra_in, extra_specs, aliases = _state_slot(B, bb, layer, depth, prev_wkv, len(inputs), 1)
    return pl.pallas_call(
        functools.partial(_rwkv_kernel, L=L, bb=bb, nc=nc),
        grid=(B // bb, T // TT),
        in_specs=[pl.BlockSpec((bb, TT, RW_COLS), lambda b, j: (b, j, 0)),
                  pl.BlockSpec((bb, 1, RW_COLS), lambda b, j: (b, 0, 0)),
                  pl.BlockSpec((bb, N_HEADS, HEAD_DIM, HEAD_DIM), lambda b, j: (b, 0, 0, 0)),
                  full(mu), full(vec), full(wup), full(aup), full(gup)] + extra_specs,
        out_specs=[pl.BlockSpec((bb, TT, GROUP_W), lambda b, j: (b, j, 0)), st_spec],
        out_shape=[jax.ShapeDtypeStruct((B, T, GROUP_W), F32), st_shape],
        input_output_aliases=aliases,
        scratch_shapes=[pltpu.VMEM((bb, 8 + TT, RW_COLS), F32),
                        pltpu.VMEM((bb, N_HEADS, HEAD_DIM, HEAD_DIM), F32)],
        compiler_params=_params(("arbitrary", "arbitrary")),
        name="rwkv7",
    )(*inputs, *extra_in)


def _mlstm_kernel(u_ref, gt_ref, cb_ref, c0_ref, n0_ref, m0_ref, cw_ref, cbias_ref, gbias_ref, ng_ref,
                  *rest, L, bb, nc):
    y_ref, c_ref, n_ref, m_ref, zbuf, c_sc, n_sc, m_sc = rest[-8:]
    j = pl.program_id(1)
    G, HD = GROUP_W, HEAD_DIM
    TT = nc * L
    R = bb * TT

    @pl.when(j == 0)
    def _():
        zbuf[:, 5:8, :] = cb_ref[...]
        c_sc[...] = c0_ref[...]
        n_sc[...] = n0_ref[...]
        m_sc[...] = m0_ref[...]

    u3 = u_ref[...]
    zbuf[:, 8:8 + TT, :] = u3[:, :, 0:2 * G]
    conv = cbias_ref[...]
    for t in range(4):
        conv = conv + zbuf[:, 5 + t:5 + t + TT, :] * cw_ref[t:t + 1, :]
    tail = zbuf[:, 5 + TT:8 + TT, :]
    zbuf[:, 5:8, :] = tail
    conv = conv.reshape(R, 2 * G)
    u = u3.reshape(R, ML_MAIN)
    sq = conv * _sigmoid(conv)
    q = sq[:, 0:G]
    k = sq[:, G:2 * G] * (HD ** -0.5)
    v = u[:, 2 * G:3 * G]
    o = u[:, 3 * G:4 * G]

    gates = gt_ref[...].reshape(R, GATE_PAD) + gbias_ref[...]
    lane = _iota((1, GATE_PAD), 1)
    gl = jnp.where(lane < N_HEADS, gates, -_softplus(-gates))
    row = _iota((L, L), 0)
    col = _iota((L, L), 1)
    causal = row >= col
    groups = range(bb * nc)
    sel = jnp.concatenate([(row <= col).astype(F32), (row == col).astype(F32)], axis=1)
    bcol_parts = [_dot_exact_lhs(causal.astype(F32), gl[ci * L:(ci + 1) * L]) for ci in groups]
    brow_parts = [_dot_exact_rhs(gl[ci * L:(ci + 1) * L], sel, TN) for ci in groups]
    bcol = jnp.concatenate(bcol_parts, axis=0) if len(bcol_parts) > 1 else bcol_parts[0]

    heads = range(N_HEADS)
    chains = [(ci, h) for ci in range(bb * nc) for h in heads]

    def cut(x, ch):
        ci, h = ch
        return x[ci * L:(ci + 1) * L, h * HD:(h + 1) * HD]

    def rows_of(x, ch, lane0):
        ci, h = ch
        return x[ci * L:(ci + 1) * L, lane0 + h:lane0 + h + 1]

    bc = {ch: rows_of(bcol, ch, N_HEADS) for ch in chains}
    ic = {ch: rows_of(gl, ch, 0) for ch in chains}
    g_tot = {(ci, h): bcol_parts[ci][L - 1:L, N_HEADS + h:N_HEADS + h + 1] for ci, h in chains}
    br = {(ci, h): brow_parts[ci][N_HEADS + h:N_HEADS + h + 1, 0:L] for ci, h in chains}
    ir = {(ci, h): brow_parts[ci][h:h + 1, L:2 * L] for ci, h in chains}
    q_f = {ch: cut(q, ch) for ch in chains}
    k_f = {ch: cut(k, ch) for ch in chains}
    q_b = {ch: q_f[ch].astype(BF16) for ch in chains}
    k_b = {ch: k_f[ch].astype(BF16) for ch in chains}
    v_f = {ch: cut(v, ch) for ch in chains}
    v_b = {ch: v_f[ch].astype(BF16) for ch in chains}
    log_d = {ch: jnp.where(causal, bc[ch] - br[ch] + ir[ch], -jnp.inf) for ch in chains}
    m_loc = {ch: jnp.max(log_d[ch], -1, keepdims=True) for ch in chains}
    s0 = {ch: _dot(q_b[ch], k_b[ch], NT) * jnp.exp(log_d[ch] - m_loc[ch]) for ch in chains}
    s_sum = {ch: jnp.sum(s0[ch], -1, keepdims=True) for ch in chains}
    sv = {ch: _dot(s0[ch].astype(BF16), v_b[ch]) for ch in chains}
    logw = {ch: g_tot[ch] - bc[ch] + ic[ch] for ch in chains}
    m_w = {ch: jnp.max(logw[ch], 0, keepdims=True) for ch in chains}
    w_loc = {ch: jnp.exp(logw[ch] - m_w[ch]) for ch in chains}
    kv = {ch: _dot((v_f[ch] * w_loc[ch]).astype(BF16), k_b[ch], TN) for ch in chains}
    n_loc = {ch: jnp.sum(k_f[ch] * w_loc[ch], 0, keepdims=True) for ch in chains}

    bh = [(b, h) for b in range(bb) for h in heads]
    m_vec = {b: m_sc[b] for b in range(bb)}
    m = {(b, h): m_vec[b][:, h:h + 1] for b, h in bh}
    c = {k_: c_sc[k_[0], k_[1]] for k_ in bh}
    n = {(b, h): n_sc[b, h:h + 1, :] for b, h in bh}
    h_chunk = {}
    for cc in range(nc):
        ch_of = {(b, h): (b * nc + cc, h) for b, h in bh}
        qc = {k_: _dot(q_b[ch_of[k_]], c[k_].astype(BF16), NT) for k_ in bh}
        qn = {k_: jnp.sum(q_f[ch_of[k_]] * n[k_], -1, keepdims=True) for k_ in bh}
        inter = {k_: bc[ch_of[k_]] + m[k_] for k_ in bh}
        m_row = {k_: jnp.maximum(m_loc[ch_of[k_]], inter[k_]) for k_ in bh}
        m_new = {k_: jnp.maximum(g_tot[ch_of[k_]] + m[k_], m_w[ch_of[k_]]) for k_ in bh}
        dec = {k_: jnp.exp(g_tot[ch_of[k_]] + m[k_] - m_new[k_]) for k_ in bh}
        f2 = {k_: jnp.exp(m_w[ch_of[k_]] - m_new[k_]) for k_ in bh}
        c = {k_: dec[k_] * c[k_] + f2[k_] * kv[ch_of[k_]] for k_ in bh}
        n = {k_: dec[k_] * n[k_] + f2[k_] * n_loc[ch_of[k_]] for k_ in bh}
        m = m_new
        f1 = {k_: jnp.exp(m_loc[ch_of[k_]] - m_row[k_]) for k_ in bh}
        w_int = {k_: jnp.exp(inter[k_] - m_row[k_]) for k_ in bh}
        num = {k_: f1[k_] * sv[ch_of[k_]] + w_int[k_] * qc[k_] for k_ in bh}
        den = {k_: f1[k_] * s_sum[ch_of[k_]] + w_int[k_] * qn[k_] for k_ in bh}
        hh = {k_: num[k_] / jnp.maximum(jnp.abs(den[k_]), jnp.exp(-m_row[k_])) for k_ in bh}
        for b in range(bb):
            h_chunk[b * nc + cc] = jnp.concatenate([hh[b, h] for h in heads], axis=-1)
    for b in range(bb):
        m_out = m_vec[b]
        for h in heads:
            c_sc[b, h] = c[b, h]
            n_sc[b, h:h + 1, :] = n[b, h]
            m_out = jnp.where(lane == h, m[b, h], m_out)
        m_sc[b] = m_out

    hs = [h_chunk[ci] for ci in range(bb * nc)]
    hcat = jnp.concatenate(hs, axis=0) if len(hs) > 1 else hs[0]
    bo = _block_ones()
    mu_h = _head_sum1(hcat, bo) * (1.0 / HD)
    hc = hcat - mu_h
    var = _head_sum1(hc * hc, bo) * (1.0 / HD)
    y_ref[...] = (hc * lax.rsqrt(var + ML_NORM_EPS) * ng_ref[...] * _sigmoid(o)).reshape(bb, TT, G)

    @pl.when(j == pl.num_programs(1) - 1)
    def _():
        c_ref[...] = c_sc[...]
        n_ref[...] = n_sc[...]
        m_ref[...] = m_sc[...]


def _mlstm_call(u_ml, gates, conv0, c0, n0, m0, cw, cbias, gbias, ng, layer, depth, prev_c):
    B, T, _ = u_ml.shape
    L = math.gcd(T, ML_CHUNK)
    bb = math.gcd(B, ML_CHUNKS_PER_STEP)
    nc = math.gcd(T // L, ML_CHUNKS_PER_STEP // bb)
    TT = nc * L
    full = lambda a: pl.BlockSpec(a.shape, lambda b, j: (0,) * a.ndim)
    st4 = pl.BlockSpec((bb, N_HEADS, HEAD_DIM, HEAD_DIM), lambda b, j: (b, 0, 0, 0))
    st3 = pl.BlockSpec((bb, N_HEADS, HEAD_DIM), lambda b, j: (b, 0, 0))
    stm = pl.BlockSpec((bb, 1, GATE_PAD), lambda b, j: (b, 0, 0))
    inputs = [u_ml, gates, conv0, c0, n0, m0, cw, cbias, gbias, ng]
    c_spec, c_shape, extra_in, extra_specs, aliases = _state_slot(B, bb, layer, depth, prev_c, len(inputs), 1)
    return pl.pallas_call(
        functools.partial(_mlstm_kernel, L=L, bb=bb, nc=nc),
        grid=(B // bb, T // TT),
        in_specs=[pl.BlockSpec((bb, TT, ML_MAIN), lambda b, j: (b, j, 0)),
                  pl.BlockSpec((bb, TT, GATE_PAD), lambda b, j: (b, j, 0)),
                  pl.BlockSpec((bb, 3, 2 * GROUP_W), lambda b, j: (b, 0, 0)),
                  st4, st3, stm, full(cw), full(cbias), full(gbias), full(ng)] + extra_specs,
        out_specs=[pl.BlockSpec((bb, TT, GROUP_W), lambda b, j: (b, j, 0)), c_spec, st3, stm],
        out_shape=[jax.ShapeDtypeStruct((B, T, GROUP_W), F32),
                   c_shape,
                   jax.ShapeDtypeStruct((B, N_HEADS, HEAD_DIM), F32),
                   jax.ShapeDtypeStruct((B, 1, GATE_PAD), F32)],
        input_output_aliases=aliases,
        scratch_shapes=[pltpu.VMEM((bb, 8 + TT, 2 * GROUP_W), F32),
                        pltpu.VMEM((bb, N_HEADS, HEAD_DIM, HEAD_DIM), F32),
                        pltpu.VMEM((bb, N_HEADS, HEAD_DIM), F32),
                        pltpu.VMEM((bb, 1, GATE_PAD), F32)],
        compiler_params=_params(("arbitrary", "arbitrary")),
        name="mlstm",
    )(*inputs, *extra_in)


def _pool_kernel(u_ref, hist_ref, w_ref, scale_ref, y_ref, zbuf, *, pos0):
    j = pl.program_id(1)
    bb, tt, G = u_ref.shape
    gc = G // len(POOL_WINDOWS)

    @pl.when(j == 0)
    def _():
        zbuf[:, 0:1, :] = jnp.zeros((bb, 1, G), F32)
        zbuf[:, 1:16, :] = hist_ref[...]

    u = u_ref[...]
    zbuf[:, 16:16 + tt, :] = u
    lane = _iota((1, 1, G), 2)
    pos = pos0 + j * tt + _iota((1, tt, 1), 1)
    acc = jnp.zeros((bb, tt, G), F32)
    pooled = jnp.zeros((bb, tt, G), F32)
    for t in range(max(POOL_WINDOWS)):
        acc = acc + zbuf[:, 16 - t:16 - t + tt, :]
        if (t + 1) in POOL_WINDOWS:
            gi = POOL_WINDOWS.index(t + 1)
            cnt = jnp.minimum(pos + 1, t + 1).astype(F32)
            pooled = jnp.where(lane // gc == gi, acc / cnt, pooled)
    tail = zbuf[:, tt:tt + 16, :]
    zbuf[:, 0:16, :] = tail
    pooled = (pooled - u).reshape(bb * tt, G)
    y_ref[...] = (_bdot(pooled, w_ref[...]) * scale_ref[...]).reshape(bb, tt, G)


def _pool_call(u_pool, hist, wbd, scale, pos0):
    B, T, G = u_pool.shape
    bb, tt = _row_blocks(B, T)
    return pl.pallas_call(
        functools.partial(_pool_kernel, pos0=pos0),
        grid=(B // bb, T // tt),
        in_specs=[pl.BlockSpec((bb, tt, G), lambda b, j: (b, j, 0)),
                  pl.BlockSpec((bb, POOL_HIST, G), lambda b, j: (b, 0, 0)),
                  pl.BlockSpec((G, G), lambda b, j: (0, 0)),
                  pl.BlockSpec((1, G), lambda b, j: (0, 0))],
        out_specs=pl.BlockSpec((bb, tt, G), lambda b, j: (b, j, 0)),
        out_shape=jax.ShapeDtypeStruct((B, T, G), F32),
        scratch_shapes=[pltpu.VMEM((bb, 16 + tt, G), F32)],
        compiler_params=_params(("arbitrary", "arbitrary")),
        name="pool",
    )(u_pool, hist, wbd, scale)


def _slope_of(head):
    return jnp.where(head == 0, 2.0 ** -2, jnp.where(head == 1, 2.0 ** -4, jnp.where(head == 2, 2.0 ** -6, 2.0 ** -8)))


def _topk_mask(gate, valid, nblk):
    blk = _iota((1, nblk), 1)
    cnt = jnp.zeros(gate.shape, F32)
    for n in range(nblk):
        gn = gate[:, n:n + 1]
        ahead = (gn > gate) | ((gn == gate) & (n < blk))
        cnt = cnt + jnp.where(ahead, 1.0, 0.0)
    return valid & (cnt < MOBA_TOPK)


def _moba_p_kernel(q_ref, k_ref, vt_ref, o_ref, mean_sc, kb_sc, vt_sc, sel_sc, *, nblk):
    i = pl.program_id(1)
    BLK, G, HD, H = MOBA_BLOCK, GROUP_W, HEAD_DIM, N_HEADS
    W = H * BLK

    SLAB = 2 * HD
    lane_s = _iota((1, SLAB), 1)

    def slab_of(x, h, extra):
        xs = x[:, (h // 2) * SLAB:(h // 2 + 1) * SLAB]
        own = (lane_s // HD) == (h % 2)
        spare = HD if h % 2 == 0 else 0
        return jnp.where(own, xs, jnp.where(lane_s == spare, extra, 0.0))

    @pl.when(i == 0)
    def _():
        pos = _iota((BLK, 1), 0).astype(F32)
        for n in range(nblk):
            kblk = k_ref[0, n * BLK:(n + 1) * BLK, :]
            mean_sc[n:n + 1, :] = jnp.mean(kblk, axis=0, keepdims=True)
            for h in range(H):
                kb_sc[n, h] = slab_of(kblk, h, pos).astype(BF16)
            vt_sc[n] = vt_ref[0, :, n * BLK:(n + 1) * BLK].astype(BF16)

    q = q_ref[0]
    lane_head = _iota((1, G), 1) // HD
    qbd = jnp.concatenate([jnp.where(lane_head == h, q, 0.0) for h in range(H)], axis=0)
    slopes = [2.0 ** (-2.0 * (h + 1)) for h in range(H)]
    scale = HD ** -0.5
    qs = [slab_of(q * scale, h, slopes[h]).astype(BF16) for h in range(H)]

    blk_row = _iota((nblk, 1), 0)
    valid = blk_row < i
    gate = jnp.where(valid, _dot3(mean_sc[...], qbd, NT), -jnp.inf)
    cnt = jnp.zeros((nblk, W), F32)
    for n in range(nblk):
        gn = gate[n:n + 1, :]
        ahead = (gn > gate) | ((gn == gate) & (n < blk_row))
        cnt = cnt + jnp.where(ahead, 1.0, 0.0)
    sel = jnp.where(valid & (cnt < MOBA_TOPK), 1.0, 0.0)
    for n in range(nblk):
        sel_sc[n] = sel[n:n + 1, :]

    heads = range(H)

    def scores(jb):
        return [_dot(kb_sc[jb, h], qs[h], NT) for h in heads]

    def pv(jb, p):
        vt = vt_sc[jb]
        return [_dot(vt[h * HD:(h + 1) * HD, :], p[h].astype(BF16)) for h in heads]

    causal = _iota((BLK, BLK), 0) <= _iota((BLK, BLK), 1)
    s = [jnp.where(causal, x, -jnp.inf) for x in scores(i)]
    m0 = [jnp.max(x, 0, keepdims=True) for x in s]
    p = [jnp.exp(s[h] - m0[h]) for h in heads]
    l0 = [jnp.sum(x, 0, keepdims=True) for x in p]
    acc0 = pv(i, p)

    def update(jbs, carry):
        m, l, acc = carry
        s = []
        for jb in jbs:
            s_raw = scores(jb)
            off = ((i - jb) * BLK).astype(F32)
            s.append([s_raw[h] + jnp.where(sel_sc[jb, :, h * BLK:(h + 1) * BLK] > 0.0, -slopes[h] * off, -jnp.inf)
                      for h in heads])
        m2 = m
        for x in s:
            m2 = [jnp.maximum(m2[h], jnp.max(x[h], 0, keepdims=True)) for h in heads]
        alpha = [jnp.exp(m[h] - m2[h]) for h in heads]
        p = [jnp.concatenate([jnp.exp(x[h] - m2[h]) for x in s], axis=0) for h in heads]
        vt = [jnp.concatenate([vt_sc[jb][h * HD:(h + 1) * HD, :] for jb in jbs], axis=1) for h in heads]
        new = [_dot(vt[h], p[h].astype(BF16)) for h in heads]
        l = [alpha[h] * l[h] + jnp.sum(p[h], 0, keepdims=True) for h in heads]
        acc = [alpha[h] * acc[h] + new[h] for h in heads]
        return m2, l, acc

    carry = lax.fori_loop(0, i // 2, lambda jp, c: update((2 * jp, 2 * jp + 1), c), (m0, l0, acc0))
    m, l, acc = lax.fori_loop(0, i % 2, lambda _, c: update((i - 1,), c), carry)
    out_t = jnp.concatenate([acc[h] / l[h] for h in heads], axis=0)
    o_ref[0] = out_t.T


def _moba_prompt_call(q, k, v_t, layer):
    B, T, G = q.shape
    assert T % MOBA_BLOCK == 0
    nblk = T // MOBA_BLOCK
    return pl.pallas_call(
        functools.partial(_moba_p_kernel, nblk=nblk),
        grid=(B, nblk),
        in_specs=[pl.BlockSpec((1, MOBA_BLOCK, G), lambda b, i: (b, i, 0)),
                  pl.BlockSpec((1, T, G), lambda b, i: (b, 0, 0)),
                  pl.BlockSpec((None, 1, G, T), lambda b, i: (layer, b, 0, 0))],
        out_specs=pl.BlockSpec((1, MOBA_BLOCK, G), lambda b, i: (b, i, 0)),
        out_shape=jax.ShapeDtypeStruct((B, T, G), F32),
        scratch_shapes=[pltpu.VMEM((nblk, G), F32),
                        pltpu.VMEM((nblk, N_HEADS, MOBA_BLOCK, 2 * HEAD_DIM), BF16),
                        pltpu.VMEM((nblk, G, MOBA_BLOCK), BF16),
                        pltpu.VMEM((nblk, 1, N_HEADS * MOBA_BLOCK), F32)],
        compiler_params=_params(("arbitrary", "arbitrary")),
        name="moba_prompt",
    )(q, k, v_t)


def _moba_s_kernel(pt_ref, q_ref, kn_ref, vn_ref, *refs, n_pages, page, past_len, rb):
    kp = refs[:rb * n_pages]
    vp = refs[rb * n_pages:2 * rb * n_pages]
    o_ref = refs[2 * rb * n_pages]
    G, HD = GROUP_W, HEAD_DIM
    ts = q_ref.shape[1]
    R = N_HEADS * ts
    ppb = MOBA_BLOCK // page
    nb = n_pages // ppb
    rows = range(rb)
    pages = range(n_pages)
    lane_head = _iota((1, G), 1) // HD
    row_head = _iota((R, 1), 0) // ts
    row_t = _iota((R, 1), 0) % ts
    slope = _slope_of(row_head)
    scale = HD ** -0.5
    qbd = [jnp.concatenate([jnp.where(lane_head == h, q_ref[r], 0.0) for h in range(N_HEADS)], axis=0)
           for r in rows]
    qb = [x.astype(BF16) for x in qbd]

    kpg = {(r, p): kp[r * n_pages + p][...] for r in rows for p in pages}
    raw = {(r, p): _dot(qb[r], kpg[r, p].astype(BF16)) for r in rows for p in pages}
    rsum = {(r, p): jnp.sum(raw[r, p], axis=1, keepdims=True) for r in rows for p in pages}
    gate = [jnp.concatenate(
        [sum(rsum[r, p] for p in range(n * ppb, (n + 1) * ppb)) * (1.0 / MOBA_BLOCK) for n in range(nb)], axis=1)
        for r in rows]
    all_valid = _iota((1, nb), 1) >= 0
    sel = [jnp.where(_topk_mask(gate[r], all_valid, nb), 1.0, 0.0) for r in rows]

    q_pos = (past_len + row_t).astype(F32)
    off = _iota((1, page), 1).astype(F32)
    scores = {(r, p): jnp.where(sel[r][:, p // ppb:p // ppb + 1] > 0.0,
                                raw[r, p] * scale - slope * (q_pos - (p * page + off)), -jnp.inf)
              for r in rows for p in pages}
    t_new = _iota((1, ts), 1)
    s_own = [jnp.where(t_new <= row_t,
                       _dot(qb[r], kn_ref[r].astype(BF16), NT) * scale - slope * (row_t - t_new).astype(F32),
                       -jnp.inf) for r in rows]

    m = [jnp.max(s_own[r], -1, keepdims=True) for r in rows]
    for p in pages:
        m = [jnp.maximum(m[r], jnp.max(scores[r, p], -1, keepdims=True)) for r in rows]
    p_own = [jnp.exp(s_own[r] - m[r]) for r in rows]
    l = [jnp.sum(p_own[r], -1, keepdims=True) for r in rows]
    acc = [_dot(p_own[r].astype(BF16), vn_ref[r].astype(BF16)) for r in rows]
    for p in pages:
        pr = [jnp.exp(scores[r, p] - m[r]) for r in rows]
        l = [l[r] + jnp.sum(pr[r], -1, keepdims=True) for r in rows]
        acc = [acc[r] + _dot(pr[r].astype(BF16), vp[r * n_pages + p][...].astype(BF16), NT) for r in rows]
    for r in rows:
        out = acc[r] / l[r]
        y = jnp.zeros((ts, G), F32)
        for h in range(N_HEADS):
            y = jnp.where(lane_head == h, out[h * ts:(h + 1) * ts, :], y)
        o_ref[r] = y


def _moba_sample_call(q, k, v, cache_k, cache_v, page_table, layer):
    DB, TS, G = q.shape
    Ld, n_phys, page, H, d = cache_k.shape
    n_pages = page_table.shape[1]
    past_len = n_pages * page
    assert past_len % MOBA_BLOCK == 0 and MOBA_BLOCK % page == 0 and past_len // MOBA_BLOCK >= MOBA_TOPK
    ck = jnp.transpose(cache_k, (0, 1, 3, 4, 2)).reshape(Ld, n_phys, H * d, page)
    cv = jnp.transpose(cache_v, (0, 1, 3, 4, 2)).reshape(Ld, n_phys, H * d, page)
    rb = math.gcd(DB, MOBA_SAMPLE_ROWS_PER_STEP)
    new = pl.BlockSpec((rb, TS, G), lambda b, pt: (b, 0, 0))
    pg = [pl.BlockSpec((None, None, G, page),
                       functools.partial(lambda b, pt, r, p: (layer, pt[b * rb + r, p], 0, 0), r=r, p=p))
          for r in range(rb) for p in range(n_pages)]
    return pl.pallas_call(
        functools.partial(_moba_s_kernel, n_pages=n_pages, page=page, past_len=past_len, rb=rb),
        grid_spec=pltpu.PrefetchScalarGridSpec(
            num_scalar_prefetch=1, grid=(DB // rb,),
            in_specs=[new, new, new] + pg + pg,
            out_specs=new),
        out_shape=jax.ShapeDtypeStruct((DB, TS, G), F32),
        compiler_params=_params(("arbitrary",)),
        name="moba_sample",
    )(page_table, q, k, v, *([ck] * (rb * n_pages)), *([cv] * (rb * n_pages)))


def _out_kernel(yrw_ref, yml_ref, ypl_ref, yat_ref, x_ref, g1_ref, sc2_ref, sh2_ref, g2_ref, n2g_ref, fng_ref,
                wout_ref, wup_ref, wdn_ref, o_ref, h2_sc, acc_sc, *, final):
    f = pl.program_id(2)
    bb, tt, D = x_ref.shape
    M = bb * tt

    @pl.when(f == 0)
    def _():
        ycat = jnp.concatenate([r[...].reshape(M, GROUP_W) for r in (yrw_ref, yml_ref, ypl_ref, yat_ref)], axis=-1)
        y = _dot(ycat.astype(BF16), wout_ref[...]).reshape(bb, tt, D)
        x1 = x_ref[...] + g1_ref[...] * y
        o_ref[...] = x1
        hn = x1 * lax.rsqrt(jnp.mean(x1 * x1, -1, keepdims=True) + NORM_EPS) * n2g_ref[...]
        h2_sc[...] = (hn * (1.0 + sc2_ref[...]) + sh2_ref[...]).reshape(M, D).astype(BF16)
        acc_sc[...] = jnp.zeros((M, D), F32)

    a = _dot(h2_sc[...], wup_ref[...])
    a = jnp.square(jnp.maximum(a, 0.0)).astype(BF16)
    acc_sc[...] += _dot(a, wdn_ref[...])

    @pl.when(f == pl.num_programs(2) - 1)
    def _():
        x2 = o_ref[...] + g2_ref[...] * acc_sc[...].reshape(bb, tt, D)
        if final:
            x2 = x2 * lax.rsqrt(jnp.mean(x2 * x2, -1, keepdims=True) + NORM_EPS) * fng_ref[...]
        o_ref[...] = x2


def _out_call(ys, x, g1, sc2, sh2, g2, n2g, fng, wout, wup, wdn, layer, final):
    B, T, D = x.shape
    bb, tt = _row_blocks(B, T, OUT_ROW_TILE)
    FF = wup.shape[2]
    row = lambda n: pl.BlockSpec((bb, tt, n), lambda b, j, f: (b, j, 0))
    mod = pl.BlockSpec((bb, 1, D), lambda b, j, f: (b, 0, 0))
    vec = pl.BlockSpec((1, D), lambda b, j, f: (0, 0))
    return pl.pallas_call(
        functools.partial(_out_kernel, final=final),
        grid=(B // bb, T // tt, FF // FF_TILE),
        in_specs=[row(GROUP_W)] * 4 + [row(D), mod, mod, mod, mod, vec, vec,
                  pl.BlockSpec((None, D, D), lambda b, j, f: (layer, 0, 0)),
                  pl.BlockSpec((None, D, FF_TILE), lambda b, j, f: (layer, 0, f)),
                  pl.BlockSpec((None, FF_TILE, D), lambda b, j, f: (layer, f, 0))],
        out_specs=row(D),
        out_shape=jax.ShapeDtypeStruct((B, T, D), F32),
        scratch_shapes=[pltpu.VMEM((bb * tt, D), BF16),
                        pltpu.VMEM((bb * tt, D), F32)],
        compiler_params=_params(("arbitrary", "arbitrary", "arbitrary"), OUT_VMEM_LIMIT),
        name="out_mlp",
    )(*ys, x, g1, sc2, sh2, g2, n2g, fng, wout, wup, wdn)


def _rw_perm(x):
    return jnp.concatenate([x[..., 0:256], x[..., 320:576], x[..., 576:832],
                            x[..., 256:320], x[..., 832:896], x[..., 896:1024]], axis=-1)


def _rw_unperm(x):
    return jnp.concatenate([x[..., 0:256], x[..., 768:832], x[..., 256:512],
                            x[..., 512:768], x[..., 832:896], x[..., 896:1024]], axis=-1)


def _layer(x, mods, st, pos0, attend, lw, final, transposed_kv, prev):
    sh1, sc1, g1, sh2, sc2, g2 = mods
    shift0, wkv0, conv0, c0, n0, m0, pool0 = st
    B, T, D = x.shape
    layer, depth = lw['layer'], lw['depth']
    prev_kv, prev_wkv, prev_c = prev if prev is not None else (None, None, None)
    outs = _in_call(x, sc1, sh1, lw['norm1_g'], lw['wpack'], lw['wkv_t'], transposed_kv, layer, depth, prev_kv)
    u_rw, u_ml, gates, u_pool, q, k = outs[:6]

    y_rw, wkv1 = _rwkv_call(u_rw, _rw_perm(shift0)[:, None, :], wkv0, lw['rw_mu'], lw['rw_vec'],
                            lw['rwkv_w_up'], lw['rwkv_a_up'], lw['rwkv_g_up'], layer, depth, prev_wkv)
    shift1 = _rw_unperm(u_rw[:, -1, :])

    m0p = jnp.pad(m0, ((0, 0), (0, GATE_PAD - N_HEADS)))[:, None, :]
    y_ml, c1, n1, m1p = _mlstm_call(u_ml, gates, conv0, c0, n0, m0p, lw['mlstm_conv_w'], lw['ml_cbias'],
                                    lw['ml_gbias'], lw['ml_ng'], layer, depth, prev_c)
    m1 = m1p[:, 0, :N_HEADS]
    zc = u_ml[:, :, :2 * GROUP_W] if T >= 3 else jnp.concatenate([conv0, u_ml[:, :, :2 * GROUP_W]], 1)
    conv1 = zc[:, -3:]

    y_pool = _pool_call(u_pool, pool0, lw['pool_wbd'], lw['pool_scale'], pos0)
    zp = u_pool if T >= POOL_HIST else jnp.concatenate([pool0, u_pool], 1)
    pool1 = zp[:, -POOL_HIST:]

    if transposed_kv:
        k_out, v_out = outs[6], outs[7]
        y_at = attend(q, k, v_out)
    else:
        kv_shape = (B, T, N_HEADS, HEAD_DIM)
        y_at = attend(q, k, outs[6])
        k_out, v_out = k.reshape(kv_shape), outs[6].reshape(kv_shape)

    x_new = _out_call((y_rw, y_ml, y_pool, y_at), x, g1, sc2, sh2, g2, lw['norm2_g'], lw['final_g'],
                      lw['w_out'], lw['mlp_up'], lw['mlp_down'], lw['layer'], final)
    return x_new, (shift1, wkv1, conv1, c1, n1, m1, pool1, k_out, v_out)


def kernel(x_prompt, x_sample, c_prompt, c_sample, state_rwkv_shift, state_rwkv_wkv, state_mlstm_conv, state_mlstm_c, state_mlstm_n, state_mlstm_m, state_pool, cache_k, cache_v, page_table, ada_w, ada_b, norm1_g, norm2_g, w_in, w_out, rwkv_mu, rwkv_w0, rwkv_w_up, rwkv_a0, rwkv_a_up, rwkv_g_up, rwkv_k_k, rwkv_k_a, rwkv_r_k, rwkv_ln_g, rwkv_ln_b, mlstm_conv_w, mlstm_conv_b, mlstm_i_b, mlstm_f_b, mlstm_norm_g, pool_w, pool_scale, mlp_up, mlp_down, final_norm_g):
    B, T, D = x_prompt.shape
    DB = x_sample.shape[0]
    depth = ada_w.shape[0]
    G = GROUP_W
    assert D == D_MODEL and w_in.shape[-1] == N_IN

    mod = _ada_call(jnp.concatenate([c_prompt, c_sample], 0), ada_w, ada_b)

    st_p0 = (jnp.zeros((B, RW_COLS), F32), jnp.zeros((B, N_HEADS, HEAD_DIM, HEAD_DIM), F32),
             jnp.zeros((B, 3, 2 * G), F32), jnp.zeros((B, N_HEADS, HEAD_DIM, HEAD_DIM), F32),
             jnp.zeros((B, N_HEADS, HEAD_DIM), F32), jnp.zeros((B, N_HEADS), F32),
             jnp.zeros((B, POOL_HIST, G), F32))
    past_len = page_table.shape[1] * cache_k.shape[2]

    w_out_b, mlp_up_b, mlp_down_b = w_out.astype(BF16), mlp_up.astype(BF16), mlp_down.astype(BF16)
    xp, xs = x_prompt, x_sample
    new_p, new_s = [], []
    prev_p = prev_s = None
    for l in range(depth):
        wl = w_in[l]
        wpack = jnp.concatenate([
            _rw_perm(wl[:, :RW_COLS]),
            wl[:, RW_COLS:RW_COLS + ML_MAIN],
            jnp.pad(wl[:, RW_COLS + ML_MAIN:RW_COLS + ML_MAIN + ML_GATES], ((0, 0), (0, GATE_PAD - ML_GATES))),
            wl[:, RW_COLS + ML_MAIN + ML_GATES:],
        ], axis=1).astype(BF16)
        pool_wbd = jnp.zeros((G, G), F32)
        gcw = G // len(POOL_WINDOWS)
        for gi in range(len(POOL_WINDOWS)):
            pool_wbd = pool_wbd.at[gi * gcw:(gi + 1) * gcw, gi * gcw:(gi + 1) * gcw].set(pool_w[l, gi])
        row = lambda a: a.reshape(1, -1)
        lw = {
            'norm1_g': row(norm1_g[l]), 'norm2_g': row(norm2_g[l]), 'final_g': row(final_norm_g),
            'wpack': wpack, 'wkv_t': wl[:, N_IN - 2 * G:].T.astype(BF16), 'layer': l, 'depth': depth,
            'w_out': w_out_b, 'mlp_up': mlp_up_b, 'mlp_down': mlp_down_b,
            'rw_mu': row(_rw_perm(rwkv_mu[l])),
            'rw_vec': jnp.stack([rwkv_w0[l], rwkv_a0[l], rwkv_k_k[l], rwkv_k_a[l], rwkv_ln_g[l], rwkv_ln_b[l],
                                 rwkv_r_k[l].reshape(-1), jnp.zeros((G,), F32)]),
            'rwkv_w_up': rwkv_w_up[l], 'rwkv_a_up': rwkv_a_up[l], 'rwkv_g_up': rwkv_g_up[l],
            'mlstm_conv_w': mlstm_conv_w[l], 'ml_cbias': row(mlstm_conv_b[l]),
            'ml_gbias': row(jnp.pad(jnp.concatenate([mlstm_i_b[l], mlstm_f_b[l]]), (0, GATE_PAD - ML_GATES))),
            'ml_ng': row(mlstm_norm_g[l]),
            'pool_wbd': pool_wbd, 'pool_scale': row(pool_scale[l]),
        }
        mods = [mod[l, :, i * D:(i + 1) * D][:, None, :] for i in range(6)]
        mods_p = [m[:B] for m in mods]
        mods_s = [m[B:] for m in mods]
        final = l == depth - 1

        xp, st_p = _layer(xp, mods_p, st_p0, 0, functools.partial(_moba_prompt_call, layer=l), lw, final, True,
                          prev_p)
        prev_p = (st_p[7:9], st_p[1], st_p[3])
        st_s_in = (state_rwkv_shift[l], state_rwkv_wkv[l], state_mlstm_conv[l], state_mlstm_c[l],
                   state_mlstm_n[l], state_mlstm_m[l], state_pool[l])
        attend_s = functools.partial(_moba_sample_call, cache_k=cache_k, cache_v=cache_v,
                                     page_table=page_table, layer=l)
        xs, st_s = _layer(xs, mods_s, st_s_in, past_len, attend_s, lw, final, False, prev_s)
        prev_s = (None, st_s[1], st_s[3])
        new_p.append(st_p)
        new_s.append(st_s)

    def collect(lst, n):
        return tuple(lst[-1][i] if i in (1, 3) else jnp.stack([st[i] for st in lst]) for i in range(n))

    kv_out = tuple(jnp.transpose(a.reshape(depth, B, N_HEADS, HEAD_DIM, T), (0, 1, 4, 2, 3)) for a in prev_p[0])
    return (xp, xs) + collect(new_p, 7) + kv_out + collect(new_s, 9)
```

```python
import functools
import math

import jax
import jax.numpy as jnp
from jax import lax
from jax.experimental import pallas as pl
from jax.experimental.pallas import tpu as pltpu

F32 = jnp.float32
BF16 = jnp.bfloat16

D_MODEL = 1024
GROUP_W = 256
HEAD_DIM = 64
N_HEADS = 4
RW_COLS = 1024
ML_MAIN = 1024
ML_GATES = 8
GATE_PAD = 128
N_IN = 3080
MOBA_BLOCK = 256
MOBA_TOPK = 3
ML_CHUNK = 256
RW_CHUNK = 64
RW_CHUNKS_PER_STEP = 8
ML_CHUNKS_PER_STEP = 4
MOBA_SAMPLE_ROWS_PER_STEP = 4
POOL_WINDOWS = (2, 4, 8, 16)
POOL_HIST = 15
NORM_EPS = 1e-6
RW_GN_EPS = 64e-5
ML_NORM_EPS = 1e-6
ROW_TILE = 512
FF_TILE = 1024
VMEM_LIMIT = 48 * 1024 * 1024
OUT_ROW_TILE = 1024
OUT_VMEM_LIMIT = 60 * 1024 * 1024

NN = (((1,), (0,)), ((), ()))
NT = (((1,), (1,)), ((), ()))
TN = (((0,), (0,)), ((), ()))


def _dot(a, b, dn=NN):
    return lax.dot_general(a, b, dn, preferred_element_type=F32)


def _bdot(a, b, dn=NN):
    return _dot(a.astype(BF16), b.astype(BF16), dn)


def _split2(x):
    hi = x.astype(BF16)
    lo = (x - hi.astype(F32)).astype(BF16)
    return hi, lo


def _split3(x):
    hi = x.astype(BF16)
    r = x - hi.astype(F32)
    mid = r.astype(BF16)
    lo = (r - mid.astype(F32)).astype(BF16)
    return hi, mid, lo


def _dot3(a, b, dn=NN):
    ah, al = _split2(a)
    bh, bl = _split2(b)
    return _dot(ah, bh, dn) + (_dot(ah, bl, dn) + _dot(al, bh, dn))


def _dot3s(a_s, b_s, dn=NN):
    return _dot(a_s[0], b_s[0], dn) + (_dot(a_s[0], b_s[1], dn) + _dot(a_s[1], b_s[0], dn))


def _dot_exact_rhs(a, b01, dn=NN):
    h, m, l = _split3(a)
    b = b01.astype(BF16)
    return _dot(h, b, dn) + (_dot(m, b, dn) + _dot(l, b, dn))


def _dot_exact_lhs(a01, b, dn=NN):
    h, m, l = _split3(b)
    a = a01.astype(BF16)
    return _dot(a, h, dn) + (_dot(a, m, dn) + _dot(a, l, dn))


def _iota(shape, dim):
    return lax.broadcasted_iota(jnp.int32, shape, dim)


def _sigmoid(x):
    return jax.nn.sigmoid(x)


def _softplus(x):
    return jnp.maximum(x, 0.0) + jnp.log(1.0 + jnp.exp(-jnp.abs(x)))


def _block_ones():
    return (_iota((GROUP_W, GROUP_W), 0) // HEAD_DIM == _iota((GROUP_W, GROUP_W), 1) // HEAD_DIM).astype(F32)


def _head_sum(x, bo):
    return _dot_exact_rhs(x, bo)


def _head_sum1(x, bo):
    return _bdot(x, bo)


def _params(sem, vmem_limit=VMEM_LIMIT):
    return pltpu.CompilerParams(dimension_semantics=sem, vmem_limit_bytes=vmem_limit)


def _row_blocks(B, T, tile=ROW_TILE):
    if T >= tile:
        assert T % tile == 0
        return 1, tile
    bb = max(1, min(B, tile // T))
    while B % bb:
        bb -= 1
    return bb, T


def _ada_kernel(c_ref, w_ref, b_ref, o_ref):
    c = c_ref[...]
    o_ref[...] = _bdot(c * _sigmoid(c), w_ref[...]) + b_ref[...]


def _ada_call(c_all, ada_w, ada_b):
    Ld, D, N6 = ada_w.shape
    NB = c_all.shape[0]
    tn = 1024
    return pl.pallas_call(
        _ada_kernel,
        grid=(Ld, N6 // tn),
        in_specs=[pl.BlockSpec((NB, D), lambda l, j: (0, 0)),
                  pl.BlockSpec((None, D, tn), lambda l, j: (l, 0, j)),
                  pl.BlockSpec((None, 1, tn), lambda l, j: (l, 0, j))],
        out_specs=pl.BlockSpec((None, NB, tn), lambda l, j: (l, 0, j)),
        out_shape=jax.ShapeDtypeStruct((Ld, NB, N6), F32),
        compiler_params=_params(("arbitrary", "arbitrary")),
        name="ada_mod",
    )(c_all, ada_w, ada_b.reshape(Ld, 1, N6))


IN_WIDTHS = (RW_COLS, ML_MAIN, GATE_PAD, GROUP_W, GROUP_W, GROUP_W, GROUP_W)


def _in_kernel(x_ref, sc_ref, sh_ref, g_ref, w_ref, wt_ref, *rest, n_rows_out, n_alias):
    out_refs = rest[n_alias:]
    x = x_ref[...]
    bb, tt, D = x.shape
    y = x * lax.rsqrt(jnp.mean(x * x, -1, keepdims=True) + NORM_EPS) * g_ref[...]
    h = y * (1.0 + sc_ref[...]) + sh_ref[...]
    hb = h.reshape(bb * tt, D).astype(BF16)
    off = 0
    for ref in out_refs[:n_rows_out]:
        n = ref.shape[-1]
        ref[...] = _dot(hb, w_ref[:, off:off + n]).reshape(bb, tt, n)
        off += n
    for i, ref in enumerate(out_refs[n_rows_out:]):
        ref[0] = _dot(wt_ref[i * GROUP_W:(i + 1) * GROUP_W, :], hb, NT)


def _in_call(x, sc, sh, g, wpack, wkv_t, transposed_kv, layer, depth, prev_kv):
    B, T, D = x.shape
    bb, tt = _row_blocks(B, T)
    row = lambda n: pl.BlockSpec((bb, tt, n), lambda b, j: (b, j, 0))
    mod = pl.BlockSpec((bb, 1, D), lambda b, j: (b, 0, 0))
    widths = IN_WIDTHS[:-1] if transposed_kv else IN_WIDTHS
    out_specs = [row(n) for n in widths]
    out_shape = [jax.ShapeDtypeStruct((B, T, n), F32) for n in widths]
    inputs = [x, sc, sh, g, wpack, wkv_t]
    in_specs = [row(D), mod, mod,
                pl.BlockSpec((1, D), lambda b, j: (0, 0)),
                pl.BlockSpec(wpack.shape, lambda b, j: (0, 0)),
                pl.BlockSpec(wkv_t.shape, lambda b, j: (0, 0))]
    aliases = {}
    if transposed_kv:
        assert bb == 1
        out_specs += [pl.BlockSpec((None, 1, GROUP_W, tt), lambda b, j: (layer, b, 0, j))] * 2
        out_shape += [jax.ShapeDtypeStruct((depth, B, GROUP_W, T), F32)] * 2
        if prev_kv is not None:
            aliases = {len(inputs) + i: len(widths) + i for i in range(2)}
            inputs += list(prev_kv)
            in_specs += [pl.BlockSpec(memory_space=pl.ANY)] * 2
    return pl.pallas_call(
        functools.partial(_in_kernel, n_rows_out=len(widths), n_alias=len(aliases)),
        grid=(B // bb, T // tt),
        in_specs=in_specs,
        out_specs=out_specs,
        out_shape=out_shape,
        input_output_aliases=aliases,
        compiler_params=_params(("arbitrary", "arbitrary")),
        name="in_proj",
    )(*inputs)


def _rwkv_kernel(u_ref, s0_ref, wkv0_ref, mu_ref, vec_ref, wup_ref, aup_ref, gup_ref, *rest, L, bb, nc):
    y_ref, wkv_ref, zbuf, s_sc = rest[-4:]
    j = pl.program_id(1)
    G, HD = GROUP_W, HEAD_DIM
    TT = nc * L
    R = bb * TT

    @pl.when(j == 0)
    def _():
        zbuf[:, 7:8, :] = s0_ref[...]
        s_sc[...] = wkv0_ref[...]

    u3 = u_ref[...]
    zbuf[:, 8:8 + TT, :] = u3
    prev = zbuf[:, 7:7 + TT, :]
    xs = (u3 + (prev - u3) * mu_ref[...]).reshape(R, RW_COLS)
    zbuf[:, 7:8, :] = u3[:, TT - 1:TT, :]

    r = xs[:, 0:G]
    k = xs[:, G:2 * G]
    v = xs[:, 2 * G:3 * G]
    wc = xs[:, 3 * G:3 * G + 64]
    ac = xs[:, 3 * G + 64:3 * G + 128]
    gc = xs[:, 3 * G + 128:4 * G]
    w0, a0, k_k, k_a = vec_ref[0:1, :], vec_ref[1:2, :], vec_ref[2:3, :], vec_ref[3:4, :]
    ln_g, ln_b, r_k = vec_ref[4:5, :], vec_ref[5:6, :], vec_ref[6:7, :]

    w_log = -_softplus(-(w0 + _bdot(jnp.tanh(wc), wup_ref[...]))) - 0.5
    lw = -jnp.exp(w_log)
    a = _sigmoid(a0 + _bdot(ac, aup_ref[...]))
    g = _bdot(_sigmoid(gc), gup_ref[...])
    bo = _block_ones()
    kk = k * k_k
    kk = kk / jnp.maximum(jnp.sqrt(_head_sum(kk * kk, bo)), 1e-12)
    k2 = k * (1.0 + (a - 1.0) * k_a)
    beta = kk * a

    row = _iota((L, L), 0)
    col = _iota((L, L), 1)
    incl = row >= col
    strict = row > col
    eye = (row == col).astype(F32)
    groups = range(bb * nc)
    c_parts = [_dot_exact_lhs(incl.astype(F32), lw[ci * L:(ci + 1) * L]) for ci in groups]
    c = jnp.concatenate(c_parts, axis=0) if len(c_parts) > 1 else c_parts[0]
    cl_parts = [jnp.broadcast_to(x[L - 1:L, :], (L, G)) for x in c_parts]
    cl = jnp.concatenate(cl_parts, axis=0) if len(cl_parts) > 1 else cl_parts[0]
    e_inv = jnp.exp(-c)
    e_tail = jnp.exp(cl - c)
    a_t = -kk * jnp.exp(c - lw)
    b_t = beta * e_inv
    k_t = k2 * e_inv
    r_t = r * jnp.exp(c)
    b_l = beta * e_tail
    k_l = k2 * e_tail
    e_cl = jnp.exp(cl)
    n_double = max(1, int(math.ceil(math.log2(L)))) - 1

    chains = [(ci, h) for ci in range(bb * nc) for h in range(N_HEADS)]

    def cut(x, ch):
        ci, h = ch
        return x[ci * L:(ci + 1) * L, h * HD:(h + 1) * HD]

    a_s = {ch: _split2(cut(a_t, ch)) for ch in chains}
    b_s = {ch: _split2(cut(b_t, ch)) for ch in chains}
    r_b = {ch: cut(r_t, ch).astype(BF16) for ch in chains}
    k_b = {ch: cut(k_t, ch).astype(BF16) for ch in chains}
    v_b = {ch: cut(v, ch).astype(BF16) for ch in chains}
    n_ab = {ch: jnp.where(strict, _dot3s(a_s[ch], b_s[ch], NT), 0.0) for ch in chains}
    n_ak = {ch: jnp.where(strict, _dot(a_s[ch][0], k_b[ch], NT), 0.0).astype(BF16) for ch in chains}
    n_rb = {ch: jnp.where(incl, _dot(r_b[ch], b_s[ch][0], NT), 0.0).astype(BF16) for ch in chains}
    n_rk = {ch: jnp.where(incl, _dot(r_b[ch], k_b[ch], NT), 0.0).astype(BF16) for ch in chains}
    p = {ch: eye + n_ab[ch] for ch in chains}
    m_s = {ch: _split2(n_ab[ch]) for ch in chains}
    for _ in range(n_double):
        m_s = {ch: _split2(_dot3s(m_s[ch], m_s[ch])) for ch in chains}
        p = {ch: p[ch] + _dot3s(m_s[ch], _split2(p[ch])) for ch in chains}
    p_b = {ch: p[ch].astype(BF16) for ch in chains}
    akv = {ch: _dot(n_ak[ch], v_b[ch]).astype(BF16) for ch in chains}
    y_v = {ch: _dot(n_rk[ch], v_b[ch]) for ch in chains}
    w_b = {ch: _dot(p_b[ch], a_s[ch][0]).astype(BF16) for ch in chains}
    u_t = {ch: _dot(p_b[ch], akv[ch]) for ch in chains}
    bl_b = {ch: cut(b_l, ch).astype(BF16) for ch in chains}
    kl_b = {ch: cut(k_l, ch).astype(BF16) for ch in chains}

    heads = range(N_HEADS)
    bh = [(b, h) for b in range(bb) for h in heads]
    s = {k_: s_sc[k_[0], k_[1]] for k_ in bh}
    y_chunk = {}
    for cc in range(nc):
        ch_of = {(b, h): (b * nc + cc, h) for b, h in bh}
        s_b = {k_: s[k_].astype(BF16) for k_ in bh}
        e_b = {k_: (_dot(w_b[ch_of[k_]], s_b[k_], NT) + u_t[ch_of[k_]]).astype(BF16) for k_ in bh}
        s = {k_: s[k_] * cut(e_cl, ch_of[k_])[0:1, :]
             + _dot(e_b[k_], bl_b[ch_of[k_]], TN) + _dot(v_b[ch_of[k_]], kl_b[ch_of[k_]], TN) for k_ in bh}
        yh = {k_: _dot(r_b[ch_of[k_]], s_b[k_], NT) + _dot(n_rb[ch_of[k_]], e_b[k_]) + y_v[ch_of[k_]] for k_ in bh}
        for b in range(bb):
            y_chunk[b * nc + cc] = jnp.concatenate([yh[b, h] for h in heads], axis=-1)
    for k_ in bh:
        s_sc[k_[0], k_[1]] = s[k_]

    ys = [y_chunk[ci] for ci in range(bb * nc)]
    y = jnp.concatenate(ys, axis=0) if len(ys) > 1 else ys[0]
    mu_y = _head_sum1(y, bo) * (1.0 / HD)
    yc = y - mu_y
    var = _head_sum1(yc * yc, bo) * (1.0 / HD)
    yn = yc * lax.rsqrt(var + RW_GN_EPS) * ln_g + ln_b
    bonus = _head_sum1(r * k2 * r_k, bo) * v
    y_ref[...] = ((yn + bonus) * g).reshape(bb, TT, G)

    @pl.when(j == pl.num_programs(1) - 1)
    def _():
        wkv_ref[...] = s_sc[...]


def _state_slot(B, bb, layer, depth, prev, n_inputs, out_index):
    spec = pl.BlockSpec((None, bb, N_HEADS, HEAD_DIM, HEAD_DIM), lambda b, j: (layer, b, 0, 0, 0))
    shape = jax.ShapeDtypeStruct((depth, B, N_HEADS, HEAD_DIM, HEAD_DIM), F32)
    if prev is None:
        return spec, shape, [], [], {}
    return spec, shape, [prev], [pl.BlockSpec(memory_space=pl.ANY)], {n_inputs: out_index}


def _rwkv_call(u_rw, shift0, wkv0_slot, mu, vec, wup, aup, gup, layer, depth, prev_wkv):
    B, T, _ = u_rw.shape
    wkv0, slot0 = wkv0_slot
    L = math.gcd(T, RW_CHUNK)
    bb = math.gcd(B, RW_CHUNKS_PER_STEP)
    nc = math.gcd(T // L, RW_CHUNKS_PER_STEP // bb)
    TT = nc * L
    full = lambda a: pl.BlockSpec(a.shape, lambda b, j: (0,) * a.ndim)
    inputs = [u_rw, shift0, wkv0, mu, vec, wup, aup, gup]
    st_spec, st_shape, extra_in, extra_specs, aliases = _state_slot(B, bb, layer, depth, prev_wkv, len(inputs), 1)
    return pl.pallas_call(
        functools.partial(_rwkv_kernel, L=L, bb=bb, nc=nc),
        grid=(B // bb, T // TT),
        in_specs=[pl.BlockSpec((bb, TT, RW_COLS), lambda b, j: (b, j, 0)),
                  pl.BlockSpec((bb, 1, RW_COLS), lambda b, j: (b, 0, 0)),
                  pl.BlockSpec((None, bb, N_HEADS, HEAD_DIM, HEAD_DIM), lambda b, j: (slot0, b, 0, 0, 0)),
                  full(mu), full(vec), full(wup), full(aup), full(gup)] + extra_specs,
        out_specs=[pl.BlockSpec((bb, TT, GROUP_W), lambda b, j: (b, j, 0)), st_spec],
        out_shape=[jax.ShapeDtypeStruct((B, T, GROUP_W), F32), st_shape],
        input_output_aliases=aliases,
        scratch_shapes=[pltpu.VMEM((bb, 8 + TT, RW_COLS), F32),
                        pltpu.VMEM((bb, N_HEADS, HEAD_DIM, HEAD_DIM), F32)],
        compiler_params=_params(("arbitrary", "arbitrary")),
        name="rwkv7",
    )(*inputs, *extra_in)


def _mlstm_kernel(u_ref, gt_ref, cb_ref, c0_ref, n0_ref, m0_ref, cw_ref, cbias_ref, gbias_ref, ng_ref,
                  *rest, L, bb, nc):
    y_ref, c_ref, n_ref, m_ref, zbuf, c_sc, n_sc, m_sc = rest[-8:]
    j = pl.program_id(1)
    G, HD = GROUP_W, HEAD_DIM
    TT = nc * L
    R = bb * TT

    @pl.when(j == 0)
    def _():
        zbuf[:, 5:8, :] = cb_ref[...]
        c_sc[...] = c0_ref[...]
        n_sc[...] = n0_ref[...]
        m_sc[...] = m0_ref[...]

    u3 = u_ref[...]
    zbuf[:, 8:8 + TT, :] = u3[:, :, 0:2 * G]
    conv = cbias_ref[...]
    for t in range(4):
        conv = conv + zbuf[:, 5 + t:5 + t + TT, :] * cw_ref[t:t + 1, :]
    tail = zbuf[:, 5 + TT:8 + TT, :]
    zbuf[:, 5:8, :] = tail
    conv = conv.reshape(R, 2 * G)
    u = u3.reshape(R, ML_MAIN)
    sq = conv * _sigmoid(conv)
    q = sq[:, 0:G]
    k = sq[:, G:2 * G] * (HD ** -0.5)
    v = u[:, 2 * G:3 * G]
    o = u[:, 3 * G:4 * G]

    gates = gt_ref[...].reshape(R, GATE_PAD) + gbias_ref[...]
    lane = _iota((1, GATE_PAD), 1)
    gl = jnp.where(lane < N_HEADS, gates, -_softplus(-gates))
    row = _iota((L, L), 0)
    col = _iota((L, L), 1)
    causal = row >= col
    groups = range(bb * nc)
    sel = jnp.concatenate([(row <= col).astype(F32), (row == col).astype(F32)], axis=1)
    bcol_parts = [_dot_exact_lhs(causal.astype(F32), gl[ci * L:(ci + 1) * L]) for ci in groups]
    brow_parts = [_dot_exact_rhs(gl[ci * L:(ci + 1) * L], sel, TN) for ci in groups]
    bcol = jnp.concatenate(bcol_parts, axis=0) if len(bcol_parts) > 1 else bcol_parts[0]

    heads = range(N_HEADS)
    chains = [(ci, h) for ci in range(bb * nc) for h in heads]

    def cut(x, ch):
        ci, h = ch
        return x[ci * L:(ci + 1) * L, h * HD:(h + 1) * HD]

    def rows_of(x, ch, lane0):
        ci, h = ch
        return x[ci * L:(ci + 1) * L, lane0 + h:lane0 + h + 1]

    bc = {ch: rows_of(bcol, ch, N_HEADS) for ch in chains}
    ic = {ch: rows_of(gl, ch, 0) for ch in chains}
    g_tot = {(ci, h): bcol_parts[ci][L - 1:L, N_HEADS + h:N_HEADS + h + 1] for ci, h in chains}
    br = {(ci, h): brow_parts[ci][N_HEADS + h:N_HEADS + h + 1, 0:L] for ci, h in chains}
    ir = {(ci, h): brow_parts[ci][h:h + 1, L:2 * L] for ci, h in chains}
    q_f = {ch: cut(q, ch) for ch in chains}
    k_f = {ch: cut(k, ch) for ch in chains}
    q_b = {ch: q_f[ch].astype(BF16) for ch in chains}
    k_b = {ch: k_f[ch].astype(BF16) for ch in chains}
    v_f = {ch: cut(v, ch) for ch in chains}
    v_b = {ch: v_f[ch].astype(BF16) for ch in chains}
    log_d = {ch: jnp.where(causal, bc[ch] - br[ch] + ir[ch], -jnp.inf) for ch in chains}
    m_loc = {ch: jnp.max(log_d[ch], -1, keepdims=True) for ch in chains}
    s0 = {ch: _dot(q_b[ch], k_b[ch], NT) * jnp.exp(log_d[ch] - m_loc[ch]) for ch in chains}
    s_sum = {ch: jnp.sum(s0[ch], -1, keepdims=True) for ch in chains}
    sv = {ch: _dot(s0[ch].astype(BF16), v_b[ch]) for ch in chains}
    logw = {ch: g_tot[ch] - bc[ch] + ic[ch] for ch in chains}
    m_w = {ch: jnp.max(logw[ch], 0, keepdims=True) for ch in chains}
    w_loc = {ch: jnp.exp(logw[ch] - m_w[ch]) for ch in chains}
    kv = {ch: _dot((v_f[ch] * w_loc[ch]).astype(BF16), k_b[ch], TN) for ch in chains}
    n_loc = {ch: jnp.sum(k_f[ch] * w_loc[ch], 0, keepdims=True) for ch in chains}

    bh = [(b, h) for b in range(bb) for h in heads]
    m_vec = {b: m_sc[b] for b in range(bb)}
    m = {(b, h): m_vec[b][:, h:h + 1] for b, h in bh}
    c = {k_: c_sc[k_[0], k_[1]] for k_ in bh}
    n = {(b, h): n_sc[b, h:h + 1, :] for b, h in bh}
    h_chunk = {}
    for cc in range(nc):
        ch_of = {(b, h): (b * nc + cc, h) for b, h in bh}
        qc = {k_: _dot(q_b[ch_of[k_]], c[k_].astype(BF16), NT) for k_ in bh}
        qn = {k_: jnp.sum(q_f[ch_of[k_]] * n[k_], -1, keepdims=True) for k_ in bh}
        inter = {k_: bc[ch_of[k_]] + m[k_] for k_ in bh}
        m_row = {k_: jnp.maximum(m_loc[ch_of[k_]], inter[k_]) for k_ in bh}
        m_new = {k_: jnp.maximum(g_tot[ch_of[k_]] + m[k_], m_w[ch_of[k_]]) for k_ in bh}
        dec = {k_: jnp.exp(g_tot[ch_of[k_]] + m[k_] - m_new[k_]) for k_ in bh}
        f2 = {k_: jnp.exp(m_w[ch_of[k_]] - m_new[k_]) for k_ in bh}
        c = {k_: dec[k_] * c[k_] + f2[k_] * kv[ch_of[k_]] for k_ in bh}
        n = {k_: dec[k_] * n[k_] + f2[k_] * n_loc[ch_of[k_]] for k_ in bh}
        m = m_new
        f1 = {k_: jnp.exp(m_loc[ch_of[k_]] - m_row[k_]) for k_ in bh}
        w_int = {k_: jnp.exp(inter[k_] - m_row[k_]) for k_ in bh}
        num = {k_: f1[k_] * sv[ch_of[k_]] + w_int[k_] * qc[k_] for k_ in bh}
        den = {k_: f1[k_] * s_sum[ch_of[k_]] + w_int[k_] * qn[k_] for k_ in bh}
        hh = {k_: num[k_] / jnp.maximum(jnp.abs(den[k_]), jnp.exp(-m_row[k_])) for k_ in bh}
        for b in range(bb):
            h_chunk[b * nc + cc] = jnp.concatenate([hh[b, h] for h in heads], axis=-1)
    for b in range(bb):
        m_out = m_vec[b]
        for h in heads:
            c_sc[b, h] = c[b, h]
            n_sc[b, h:h + 1, :] = n[b, h]
            m_out = jnp.where(lane == h, m[b, h], m_out)
        m_sc[b] = m_out

    hs = [h_chunk[ci] for ci in range(bb * nc)]
    hcat = jnp.concatenate(hs, axis=0) if len(hs) > 1 else hs[0]
    bo = _block_ones()
    mu_h = _head_sum1(hcat, bo) * (1.0 / HD)
    hc = hcat - mu_h
    var = _head_sum1(hc * hc, bo) * (1.0 / HD)
    y_ref[...] = (hc * lax.rsqrt(var + ML_NORM_EPS) * ng_ref[...] * _sigmoid(o)).reshape(bb, TT, G)

    @pl.when(j == pl.num_programs(1) - 1)
    def _():
        c_ref[...] = c_sc[...]
        n_ref[...] = n_sc[...]
        m_ref[...] = m_sc[...]


def _mlstm_call(u_ml, gates, conv0, c0_slot, n0, m0, cw, cbias, gbias, ng, layer, depth, prev_c):
    B, T, _ = u_ml.shape
    c0, slot0 = c0_slot
    L = math.gcd(T, ML_CHUNK)
    bb = math.gcd(B, ML_CHUNKS_PER_STEP)
    nc = math.gcd(T // L, ML_CHUNKS_PER_STEP // bb)
    TT = nc * L
    full = lambda a: pl.BlockSpec(a.shape, lambda b, j: (0,) * a.ndim)
    st3 = pl.BlockSpec((bb, N_HEADS, HEAD_DIM), lambda b, j: (b, 0, 0))
    stm = pl.BlockSpec((bb, 1, GATE_PAD), lambda b, j: (b, 0, 0))
    inputs = [u_ml, gates, conv0, c0, n0, m0, cw, cbias, gbias, ng]
    c_spec, c_shape, extra_in, extra_specs, aliases = _state_slot(B, bb, layer, depth, prev_c, len(inputs), 1)
    return pl.pallas_call(
        functools.partial(_mlstm_kernel, L=L, bb=bb, nc=nc),
        grid=(B // bb, T // TT),
        in_specs=[pl.BlockSpec((bb, TT, ML_MAIN), lambda b, j: (b, j, 0)),
                  pl.BlockSpec((bb, TT, GATE_PAD), lambda b, j: (b, j, 0)),
                  pl.BlockSpec((bb, 3, 2 * GROUP_W), lambda b, j: (b, 0, 0)),
                  pl.BlockSpec((None, bb, N_HEADS, HEAD_DIM, HEAD_DIM), lambda b, j: (slot0, b, 0, 0, 0)),
                  st3, stm, full(cw), full(cbias), full(gbias), full(ng)] + extra_specs,
        out_specs=[pl.BlockSpec((bb, TT, GROUP_W), lambda b, j: (b, j, 0)), c_spec, st3, stm],
        out_shape=[jax.ShapeDtypeStruct((B, T, GROUP_W), F32),
                   c_shape,
                   jax.ShapeDtypeStruct((B, N_HEADS, HEAD_DIM), F32),
                   jax.ShapeDtypeStruct((B, 1, GATE_PAD), F32)],
        input_output_aliases=aliases,
        scratch_shapes=[pltpu.VMEM((bb, 8 + TT, 2 * GROUP_W), F32),
                        pltpu.VMEM((bb, N_HEADS, HEAD_DIM, HEAD_DIM), F32),
                        pltpu.VMEM((bb, N_HEADS, HEAD_DIM), F32),
                        pltpu.VMEM((bb, 1, GATE_PAD), F32)],
        compiler_params=_params(("arbitrary", "arbitrary")),
        name="mlstm",
    )(*inputs, *extra_in)


def _pool_kernel(u_ref, hist_ref, w_ref, scale_ref, y_ref, zbuf, *, pos0):
    j = pl.program_id(1)
    bb, tt, G = u_ref.shape
    gc = G // len(POOL_WINDOWS)

    @pl.when(j == 0)
    def _():
        zbuf[:, 0:1, :] = jnp.zeros((bb, 1, G), F32)
        zbuf[:, 1:16, :] = hist_ref[...]

    u = u_ref[...]
    zbuf[:, 16:16 + tt, :] = u
    pos = pos0 + j * tt + _iota((1, tt, 1), 1)
    slab = 128
    parts = []
    for c0 in range(0, G, slab):
        windows = [w for gi, w in enumerate(POOL_WINDOWS) if c0 <= gi * gc < c0 + slab]
        lane = c0 + _iota((1, 1, slab), 2)
        acc = jnp.zeros((bb, tt, slab), F32)
        part = jnp.zeros((bb, tt, slab), F32)
        for t in range(max(windows)):
            acc = acc + zbuf[:, 16 - t:16 - t + tt, c0:c0 + slab]
            if (t + 1) in windows:
                inv_cnt = 1.0 / jnp.minimum(pos + 1, t + 1).astype(F32)
                part = jnp.where(lane // gc == POOL_WINDOWS.index(t + 1), acc * inv_cnt, part)
        parts.append(part)
    tail = zbuf[:, tt:tt + 16, :]
    zbuf[:, 0:16, :] = tail
    pooled = (jnp.concatenate(parts, axis=-1) - u).reshape(bb * tt, G)
    y_ref[...] = (_bdot(pooled, w_ref[...]) * scale_ref[...]).reshape(bb, tt, G)


def _pool_call(u_pool, hist, wbd, scale, pos0):
    B, T, G = u_pool.shape
    bb, tt = _row_blocks(B, T)
    return pl.pallas_call(
        functools.partial(_pool_kernel, pos0=pos0),
        grid=(B // bb, T // tt),
        in_specs=[pl.BlockSpec((bb, tt, G), lambda b, j: (b, j, 0)),
                  pl.BlockSpec((bb, POOL_HIST, G), lambda b, j: (b, 0, 0)),
                  pl.BlockSpec((G, G), lambda b, j: (0, 0)),
                  pl.BlockSpec((1, G), lambda b, j: (0, 0))],
        out_specs=pl.BlockSpec((bb, tt, G), lambda b, j: (b, j, 0)),
        out_shape=jax.ShapeDtypeStruct((B, T, G), F32),
        scratch_shapes=[pltpu.VMEM((bb, 16 + tt, G), F32)],
        compiler_params=_params(("arbitrary", "arbitrary")),
        name="pool",
    )(u_pool, hist, wbd, scale)


def _slope_of(head):
    return jnp.where(head == 0, 2.0 ** -2, jnp.where(head == 1, 2.0 ** -4, jnp.where(head == 2, 2.0 ** -6, 2.0 ** -8)))


def _topk_mask(gate, valid, nblk):
    blk = _iota((1, nblk), 1)
    cnt = jnp.zeros(gate.shape, F32)
    for n in range(nblk):
        gn = gate[:, n:n + 1]
        ahead = (gn > gate) | ((gn == gate) & (n < blk))
        cnt = cnt + jnp.where(ahead, 1.0, 0.0)
    return valid & (cnt < MOBA_TOPK)


def _moba_p_kernel(q_ref, k_ref, vt_ref, o_ref, mean_sc, kb_sc, vt_sc, sel_sc, *, nblk):
    i = pl.program_id(1)
    BLK, G, HD, H = MOBA_BLOCK, GROUP_W, HEAD_DIM, N_HEADS
    W = H * BLK

    SLAB = 2 * HD
    lane_s = _iota((1, SLAB), 1)

    def slab_of(x, h, extra):
        xs = x[:, (h // 2) * SLAB:(h // 2 + 1) * SLAB]
        own = (lane_s // HD) == (h % 2)
        spare = HD if h % 2 == 0 else 0
        return jnp.where(own, xs, jnp.where(lane_s == spare, extra, 0.0))

    @pl.when(i == 0)
    def _():
        pos = _iota((BLK, 1), 0).astype(F32)
        for n in range(nblk):
            kblk = k_ref[0, n * BLK:(n + 1) * BLK, :]
            mean_sc[n:n + 1, :] = jnp.mean(kblk, axis=0, keepdims=True)
            for h in range(H):
                kb_sc[n, h] = slab_of(kblk, h, pos).astype(BF16)
            vt_sc[n] = vt_ref[0, :, n * BLK:(n + 1) * BLK].astype(BF16)

    q = q_ref[0]
    lane_head = _iota((1, G), 1) // HD
    qbd = jnp.concatenate([jnp.where(lane_head == h, q, 0.0) for h in range(H)], axis=0)
    slopes = [2.0 ** (-2.0 * (h + 1)) for h in range(H)]
    scale = HD ** -0.5
    qs = [slab_of(q * scale, h, slopes[h]).astype(BF16) for h in range(H)]

    blk_row = _iota((nblk, 1), 0)
    valid = blk_row < i
    gate = jnp.where(valid, _dot3(mean_sc[...], qbd, NT), -jnp.inf)
    cnt = jnp.zeros((nblk, W), F32)
    for n in range(nblk):
        gn = gate[n:n + 1, :]
        ahead = (gn > gate) | ((gn == gate) & (n < blk_row))
        cnt = cnt + jnp.where(ahead, 1.0, 0.0)
    sel = jnp.where(valid & (cnt < MOBA_TOPK), 1.0, 0.0)
    for n in range(nblk):
        sel_sc[n] = sel[n:n + 1, :]

    heads = range(H)

    def scores(jb):
        return [_dot(kb_sc[jb, h], qs[h], NT) for h in heads]

    def pv(jb, p):
        vt = vt_sc[jb]
        return [_dot(vt[h * HD:(h + 1) * HD, :], p[h].astype(BF16)) for h in heads]

    causal = _iota((BLK, BLK), 0) <= _iota((BLK, BLK), 1)
    s = [jnp.where(causal, x, -jnp.inf) for x in scores(i)]
    m0 = [jnp.max(x, 0, keepdims=True) for x in s]
    p = [jnp.exp(s[h] - m0[h]) for h in heads]
    l0 = [jnp.sum(x, 0, keepdims=True) for x in p]
    acc0 = pv(i, p)

    def update(jbs, carry):
        m, l, acc = carry
        s = []
        for jb in jbs:
            s_raw = scores(jb)
            off = ((i - jb) * BLK).astype(F32)
            s.append([s_raw[h] + jnp.where(sel_sc[jb, :, h * BLK:(h + 1) * BLK] > 0.0, -slopes[h] * off, -jnp.inf)
                      for h in heads])
        m2 = m
        for x in s:
            m2 = [jnp.maximum(m2[h], jnp.max(x[h], 0, keepdims=True)) for h in heads]
        alpha = [jnp.exp(m[h] - m2[h]) for h in heads]
        p = [jnp.concatenate([jnp.exp(x[h] - m2[h]) for x in s], axis=0) for h in heads]
        vt = [jnp.concatenate([vt_sc[jb][h * HD:(h + 1) * HD, :] for jb in jbs], axis=1) for h in heads]
        new = [_dot(vt[h], p[h].astype(BF16)) for h in heads]
        l = [alpha[h] * l[h] + jnp.sum(p[h], 0, keepdims=True) for h in heads]
        acc = [alpha[h] * acc[h] + new[h] for h in heads]
        return m2, l, acc

    carry = lax.fori_loop(0, i // 2, lambda jp, c: update((2 * jp, 2 * jp + 1), c), (m0, l0, acc0))
    m, l, acc = lax.fori_loop(0, i % 2, lambda _, c: update((i - 1,), c), carry)
    out_t = jnp.concatenate([acc[h] / l[h] for h in heads], axis=0)
    o_ref[0] = out_t.T


def _moba_prompt_call(q, k, v_t, layer):
    B, T, G = q.shape
    assert T % MOBA_BLOCK == 0
    nblk = T // MOBA_BLOCK
    return pl.pallas_call(
        functools.partial(_moba_p_kernel, nblk=nblk),
        grid=(B, nblk),
        in_specs=[pl.BlockSpec((1, MOBA_BLOCK, G), lambda b, i: (b, i, 0)),
                  pl.BlockSpec((1, T, G), lambda b, i: (b, 0, 0)),
                  pl.BlockSpec((None, 1, G, T), lambda b, i: (layer, b, 0, 0))],
        out_specs=pl.BlockSpec((1, MOBA_BLOCK, G), lambda b, i: (b, i, 0)),
        out_shape=jax.ShapeDtypeStruct((B, T, G), F32),
        scratch_shapes=[pltpu.VMEM((nblk, G), F32),
                        pltpu.VMEM((nblk, N_HEADS, MOBA_BLOCK, 2 * HEAD_DIM), BF16),
                        pltpu.VMEM((nblk, G, MOBA_BLOCK), BF16),
                        pltpu.VMEM((nblk, 1, N_HEADS * MOBA_BLOCK), F32)],
        compiler_params=_params(("arbitrary", "arbitrary")),
        name="moba_prompt",
    )(q, k, v_t)


def _moba_s_kernel(pt_ref, q_ref, kn_ref, vn_ref, *refs, n_pages, page, past_len, rb):
    kp = refs[:rb * n_pages]
    vp = refs[rb * n_pages:2 * rb * n_pages]
    o_ref = refs[2 * rb * n_pages]
    G, HD = GROUP_W, HEAD_DIM
    ts = q_ref.shape[1]
    R = N_HEADS * ts
    ppb = MOBA_BLOCK // page
    nb = n_pages // ppb
    rows = range(rb)
    pages = range(n_pages)
    lane_head = _iota((1, G), 1) // HD
    row_head = _iota((R, 1), 0) // ts
    row_t = _iota((R, 1), 0) % ts
    slope = _slope_of(row_head)
    scale = HD ** -0.5
    qbd = [jnp.concatenate([jnp.where(lane_head == h, q_ref[r], 0.0) for h in range(N_HEADS)], axis=0)
           for r in rows]
    qb = [x.astype(BF16) for x in qbd]

    kpg = {(r, p): kp[r * n_pages + p][...] for r in rows for p in pages}
    raw = {(r, p): _dot(qb[r], kpg[r, p].astype(BF16)) for r in rows for p in pages}
    rsum = {(r, p): jnp.sum(raw[r, p], axis=1, keepdims=True) for r in rows for p in pages}
    gate = [jnp.concatenate(
        [sum(rsum[r, p] for p in range(n * ppb, (n + 1) * ppb)) * (1.0 / MOBA_BLOCK) for n in range(nb)], axis=1)
        for r in rows]
    all_valid = _iota((1, nb), 1) >= 0
    sel = [jnp.where(_topk_mask(gate[r], all_valid, nb), 1.0, 0.0) for r in rows]

    q_pos = (past_len + row_t).astype(F32)
    off = _iota((1, page), 1).astype(F32)
    scores = {(r, p): jnp.where(sel[r][:, p // ppb:p // ppb + 1] > 0.0,
                                raw[r, p] * scale - slope * (q_pos - (p * page + off)), -jnp.inf)
              for r in rows for p in pages}
    t_new = _iota((1, ts), 1)
    s_own = [jnp.where(t_new <= row_t,
                       _dot(qb[r], kn_ref[r].astype(BF16), NT) * scale - slope * (row_t - t_new).astype(F32),
                       -jnp.inf) for r in rows]

    m = [jnp.max(s_own[r], -1, keepdims=True) for r in rows]
    for p in pages:
        m = [jnp.maximum(m[r], jnp.max(scores[r, p], -1, keepdims=True)) for r in rows]
    p_own = [jnp.exp(s_own[r] - m[r]) for r in rows]
    l = [jnp.sum(p_own[r], -1, keepdims=True) for r in rows]
    acc = [_dot(p_own[r].astype(BF16), vn_ref[r].astype(BF16)) for r in rows]
    for p in pages:
        pr = [jnp.exp(scores[r, p] - m[r]) for r in rows]
        l = [l[r] + jnp.sum(pr[r], -1, keepdims=True) for r in rows]
        acc = [acc[r] + _dot(pr[r].astype(BF16), vp[r * n_pages + p][...].astype(BF16), NT) for r in rows]
    for r in rows:
        out = acc[r] / l[r]
        y = jnp.zeros((ts, G), F32)
        for h in range(N_HEADS):
            y = jnp.where(lane_head == h, out[h * ts:(h + 1) * ts, :], y)
        o_ref[r] = y


def _moba_sample_call(q, k, v, cache_k, cache_v, page_table, layer):
    DB, TS, G = q.shape
    Ld, n_phys, page, H, d = cache_k.shape
    n_pages = page_table.shape[1]
    past_len = n_pages * page
    assert past_len % MOBA_BLOCK == 0 and MOBA_BLOCK % page == 0 and past_len // MOBA_BLOCK >= MOBA_TOPK
    ck = jnp.transpose(cache_k, (0, 1, 3, 4, 2)).reshape(Ld, n_phys, H * d, page)
    cv = jnp.transpose(cache_v, (0, 1, 3, 4, 2)).reshape(Ld, n_phys, H * d, page)
    rb = math.gcd(DB, MOBA_SAMPLE_ROWS_PER_STEP)
    new = pl.BlockSpec((rb, TS, G), lambda b, pt: (b, 0, 0))
    pg = [pl.BlockSpec((None, None, G, page),
                       functools.partial(lambda b, pt, r, p: (layer, pt[b * rb + r, p], 0, 0), r=r, p=p))
          for r in range(rb) for p in range(n_pages)]
    return pl.pallas_call(
        functools.partial(_moba_s_kernel, n_pages=n_pages, page=page, past_len=past_len, rb=rb),
        grid_spec=pltpu.PrefetchScalarGridSpec(
            num_scalar_prefetch=1, grid=(DB // rb,),
            in_specs=[new, new, new] + pg + pg,
            out_specs=new),
        out_shape=jax.ShapeDtypeStruct((DB, TS, G), F32),
        compiler_params=_params(("arbitrary",)),
        name="moba_sample",
    )(page_table, q, k, v, *([ck] * (rb * n_pages)), *([cv] * (rb * n_pages)))


def _out_kernel(yrw_ref, yml_ref, ypl_ref, yat_ref, x_ref, g1_ref, sc2_ref, sh2_ref, g2_ref, n2g_ref, fng_ref,
                wout_ref, wup_ref, wdn_ref, o_ref, h2_sc, acc_sc, *, final):
    f = pl.program_id(2)
    bb, tt, D = x_ref.shape
    M = bb * tt

    @pl.when(f == 0)
    def _():
        ycat = jnp.concatenate([r[...].reshape(M, GROUP_W) for r in (yrw_ref, yml_ref, ypl_ref, yat_ref)], axis=-1)
        y = _dot(ycat.astype(BF16), wout_ref[...]).reshape(bb, tt, D)
        x1 = x_ref[...] + g1_ref[...] * y
        o_ref[...] = x1
        hn = x1 * lax.rsqrt(jnp.mean(x1 * x1, -1, keepdims=True) + NORM_EPS) * n2g_ref[...]
        h2_sc[...] = (hn * (1.0 + sc2_ref[...]) + sh2_ref[...]).reshape(M, D).astype(BF16)
        acc_sc[...] = jnp.zeros((M, D), F32)

    a = _dot(h2_sc[...], wup_ref[...])
    a = jnp.square(jnp.maximum(a, 0.0)).astype(BF16)
    acc_sc[...] += _dot(a, wdn_ref[...])

    @pl.when(f == pl.num_programs(2) - 1)
    def _():
        x2 = o_ref[...] + g2_ref[...] * acc_sc[...].reshape(bb, tt, D)
        if final:
            x2 = x2 * lax.rsqrt(jnp.mean(x2 * x2, -1, keepdims=True) + NORM_EPS) * fng_ref[...]
        o_ref[...] = x2


def _out_call(ys, x, g1, sc2, sh2, g2, n2g, fng, wout, wup, wdn, layer, final):
    B, T, D = x.shape
    bb, tt = _row_blocks(B, T, OUT_ROW_TILE)
    FF = wup.shape[2]
    row = lambda n: pl.BlockSpec((bb, tt, n), lambda b, j, f: (b, j, 0))
    mod = pl.BlockSpec((bb, 1, D), lambda b, j, f: (b, 0, 0))
    vec = pl.BlockSpec((1, D), lambda b, j, f: (0, 0))
    return pl.pallas_call(
        functools.partial(_out_kernel, final=final),
        grid=(B // bb, T // tt, FF // FF_TILE),
        in_specs=[row(GROUP_W)] * 4 + [row(D), mod, mod, mod, mod, vec, vec,
                  pl.BlockSpec((None, D, D), lambda b, j, f: (layer, 0, 0)),
                  pl.BlockSpec((None, D, FF_TILE), lambda b, j, f: (layer, 0, f)),
                  pl.BlockSpec((None, FF_TILE, D), lambda b, j, f: (layer, f, 0))],
        out_specs=row(D),
        out_shape=jax.ShapeDtypeStruct((B, T, D), F32),
        scratch_shapes=[pltpu.VMEM((bb * tt, D), BF16),
                        pltpu.VMEM((bb * tt, D), F32)],
        compiler_params=_params(("arbitrary", "arbitrary", "arbitrary"), OUT_VMEM_LIMIT),
        name="out_mlp",
    )(*ys, x, g1, sc2, sh2, g2, n2g, fng, wout, wup, wdn)


def _rw_perm(x):
    return jnp.concatenate([x[..., 0:256], x[..., 320:576], x[..., 576:832],
                            x[..., 256:320], x[..., 832:896], x[..., 896:1024]], axis=-1)


def _rw_unperm(x):
    return jnp.concatenate([x[..., 0:256], x[..., 768:832], x[..., 256:512],
                            x[..., 512:768], x[..., 832:896], x[..., 896:1024]], axis=-1)


def _layer(x, mods, st, pos0, attend, lw, final, transposed_kv, prev):
    sh1, sc1, g1, sh2, sc2, g2 = mods
    shift0, wkv0, conv0, c0, n0, m0, pool0 = st
    B, T, D = x.shape
    layer, depth = lw['layer'], lw['depth']
    prev_kv, prev_wkv, prev_c = prev if prev is not None else (None, None, None)
    outs = _in_call(x, sc1, sh1, lw['norm1_g'], lw['wpack'], lw['wkv_t'], transposed_kv, layer, depth, prev_kv)
    u_rw, u_ml, gates, u_pool, q, k = outs[:6]

    y_rw, wkv1 = _rwkv_call(u_rw, _rw_perm(shift0)[:, None, :], wkv0, lw['rw_mu'], lw['rw_vec'],
                            lw['rwkv_w_up'], lw['rwkv_a_up'], lw['rwkv_g_up'], layer, depth, prev_wkv)
    shift1 = _rw_unperm(u_rw[:, -1, :])

    m0p = jnp.pad(m0, ((0, 0), (0, GATE_PAD - N_HEADS)))[:, None, :]
    y_ml, c1, n1, m1p = _mlstm_call(u_ml, gates, conv0, c0, n0, m0p, lw['mlstm_conv_w'], lw['ml_cbias'],
                                    lw['ml_gbias'], lw['ml_ng'], layer, depth, prev_c)
    m1 = m1p[:, 0, :N_HEADS]
    zc = u_ml[:, :, :2 * GROUP_W] if T >= 3 else jnp.concatenate([conv0, u_ml[:, :, :2 * GROUP_W]], 1)
    conv1 = zc[:, -3:]

    y_pool = _pool_call(u_pool, pool0, lw['pool_wbd'], lw['pool_scale'], pos0)
    zp = u_pool if T >= POOL_HIST else jnp.concatenate([pool0, u_pool], 1)
    pool1 = zp[:, -POOL_HIST:]

    if transposed_kv:
        k_out, v_out = outs[6], outs[7]
        y_at = attend(q, k, v_out)
    else:
        kv_shape = (B, T, N_HEADS, HEAD_DIM)
        y_at = attend(q, k, outs[6])
        k_out, v_out = k.reshape(kv_shape), outs[6].reshape(kv_shape)

    x_new = _out_call((y_rw, y_ml, y_pool, y_at), x, g1, sc2, sh2, g2, lw['norm2_g'], lw['final_g'],
                      lw['w_out'], lw['mlp_up'], lw['mlp_down'], lw['layer'], final)
    return x_new, (shift1, wkv1, conv1, c1, n1, m1, pool1, k_out, v_out)


def kernel(x_prompt, x_sample, c_prompt, c_sample, state_rwkv_shift, state_rwkv_wkv, state_mlstm_conv, state_mlstm_c, state_mlstm_n, state_mlstm_m, state_pool, cache_k, cache_v, page_table, ada_w, ada_b, norm1_g, norm2_g, w_in, w_out, rwkv_mu, rwkv_w0, rwkv_w_up, rwkv_a0, rwkv_a_up, rwkv_g_up, rwkv_k_k, rwkv_k_a, rwkv_r_k, rwkv_ln_g, rwkv_ln_b, mlstm_conv_w, mlstm_conv_b, mlstm_i_b, mlstm_f_b, mlstm_norm_g, pool_w, pool_scale, mlp_up, mlp_down, final_norm_g):
    B, T, D = x_prompt.shape
    DB = x_sample.shape[0]
    depth = ada_w.shape[0]
    G = GROUP_W
    assert D == D_MODEL and w_in.shape[-1] == N_IN

    mod = _ada_call(jnp.concatenate([c_prompt, c_sample], 0), ada_w, ada_b)

    zero_mat = (jnp.zeros((1, B, N_HEADS, HEAD_DIM, HEAD_DIM), F32), 0)
    st_p0 = (jnp.zeros((B, RW_COLS), F32), zero_mat,
             jnp.zeros((B, 3, 2 * G), F32), zero_mat,
             jnp.zeros((B, N_HEADS, HEAD_DIM), F32), jnp.zeros((B, N_HEADS), F32),
             jnp.zeros((B, POOL_HIST, G), F32))
    past_len = page_table.shape[1] * cache_k.shape[2]

    w_out_b, mlp_up_b, mlp_down_b = w_out.astype(BF16), mlp_up.astype(BF16), mlp_down.astype(BF16)
    xp, xs = x_prompt, x_sample
    new_p, new_s = [], []
    prev_p = prev_s = None
    for l in range(depth):
        wl = w_in[l]
        wpack = jnp.concatenate([
            _rw_perm(wl[:, :RW_COLS]),
            wl[:, RW_COLS:RW_COLS + ML_MAIN],
            jnp.pad(wl[:, RW_COLS + ML_MAIN:RW_COLS + ML_MAIN + ML_GATES], ((0, 0), (0, GATE_PAD - ML_GATES))),
            wl[:, RW_COLS + ML_MAIN + ML_GATES:],
        ], axis=1).astype(BF16)
        pool_wbd = jnp.zeros((G, G), F32)
        gcw = G // len(POOL_WINDOWS)
        for gi in range(len(POOL_WINDOWS)):
            pool_wbd = pool_wbd.at[gi * gcw:(gi + 1) * gcw, gi * gcw:(gi + 1) * gcw].set(pool_w[l, gi])
        row = lambda a: a.reshape(1, -1)
        lw = {
            'norm1_g': row(norm1_g[l]), 'norm2_g': row(norm2_g[l]), 'final_g': row(final_norm_g),
            'wpack': wpack, 'wkv_t': wl[:, N_IN - 2 * G:].T.astype(BF16), 'layer': l, 'depth': depth,
            'w_out': w_out_b, 'mlp_up': mlp_up_b, 'mlp_down': mlp_down_b,
            'rw_mu': row(_rw_perm(rwkv_mu[l])),
            'rw_vec': jnp.stack([rwkv_w0[l], rwkv_a0[l], rwkv_k_k[l], rwkv_k_a[l], rwkv_ln_g[l], rwkv_ln_b[l],
                                 rwkv_r_k[l].reshape(-1), jnp.zeros((G,), F32)]),
            'rwkv_w_up': rwkv_w_up[l], 'rwkv_a_up': rwkv_a_up[l], 'rwkv_g_up': rwkv_g_up[l],
            'mlstm_conv_w': mlstm_conv_w[l], 'ml_cbias': row(mlstm_conv_b[l]),
            'ml_gbias': row(jnp.pad(jnp.concatenate([mlstm_i_b[l], mlstm_f_b[l]]), (0, GATE_PAD - ML_GATES))),
            'ml_ng': row(mlstm_norm_g[l]),
            'pool_wbd': pool_wbd, 'pool_scale': row(pool_scale[l]),
        }
        mods = [mod[l, :, i * D:(i + 1) * D][:, None, :] for i in range(6)]
        mods_p = [m[:B] for m in mods]
        mods_s = [m[B:] for m in mods]
        final = l == depth - 1

        xp, st_p = _layer(xp, mods_p, st_p0, 0, functools.partial(_moba_prompt_call, layer=l), lw, final, True,
                          prev_p)
        prev_p = (st_p[7:9], st_p[1], st_p[3])
        st_s_in = (state_rwkv_shift[l], (state_rwkv_wkv, l), state_mlstm_conv[l], (state_mlstm_c, l),
                   state_mlstm_n[l], state_mlstm_m[l], state_pool[l])
        attend_s = functools.partial(_moba_sample_call, cache_k=cache_k, cache_v=cache_v,
                                     page_table=page_table, layer=l)
        xs, st_s = _layer(xs, mods_s, st_s_in, past_len, attend_s, lw, final, False, prev_s)
        prev_s = (None, st_s[1], st_s[3])
        new_p.append(st_p)
        new_s.append(st_s)

    def collect(lst, n):
        return tuple(lst[-1][i] if i in (1, 3) else jnp.stack([st[i] for st in lst]) for i in range(n))

    kv_out = tuple(jnp.transpose(a.reshape(depth, B, N_HEADS, HEAD_DIM, T), (0, 1, 4, 2, 3)) for a in prev_p[0])
    return (xp, xs) + collect(new_p, 7) + kv_out + collect(new_s, 9)
```

```python
import functools
import math

import jax
import jax.numpy as jnp
from jax import lax
from jax.experimental import pallas as pl
from jax.experimental.pallas import tpu as pltpu

F32 = jnp.float32
BF16 = jnp.bfloat16

D_MODEL = 1024
GROUP_W = 256
HEAD_DIM = 64
N_HEADS = 4
RW_COLS = 1024
ML_MAIN = 1024
ML_GATES = 8
GATE_PAD = 128
N_IN = 3080
MOBA_BLOCK = 256
MOBA_TOPK = 3
ML_CHUNK = 256
RW_CHUNK = 64
RW_CHUNKS_PER_STEP = 8
ML_CHUNKS_PER_STEP = 4
MOBA_SAMPLE_ROWS_PER_STEP = 4
POOL_WINDOWS = (2, 4, 8, 16)
POOL_HIST = 15
NORM_EPS = 1e-6
RW_GN_EPS = 64e-5
ML_NORM_EPS = 1e-6
ROW_TILE = 512
FF_TILE = 1024
VMEM_LIMIT = 48 * 1024 * 1024
OUT_ROW_TILE = 1024
OUT_VMEM_LIMIT = 60 * 1024 * 1024

NN = (((1,), (0,)), ((), ()))
NT = (((1,), (1,)), ((), ()))
TN = (((0,), (0,)), ((), ()))


def _dot(a, b, dn=NN):
    return lax.dot_general(a, b, dn, preferred_element_type=F32)


def _bdot(a, b, dn=NN):
    return _dot(a.astype(BF16), b.astype(BF16), dn)


def _split2(x):
    hi = x.astype(BF16)
    lo = (x - hi.astype(F32)).astype(BF16)
    return hi, lo


def _split3(x):
    hi = x.astype(BF16)
    r = x - hi.astype(F32)
    mid = r.astype(BF16)
    lo = (r - mid.astype(F32)).astype(BF16)
    return hi, mid, lo


def _dot3(a, b, dn=NN):
    ah, al = _split2(a)
    bh, bl = _split2(b)
    return _dot(ah, bh, dn) + (_dot(ah, bl, dn) + _dot(al, bh, dn))


def _dot3s(a_s, b_s, dn=NN):
    return _dot(a_s[0], b_s[0], dn) + (_dot(a_s[0], b_s[1], dn) + _dot(a_s[1], b_s[0], dn))


def _dot_exact_rhs(a, b01, dn=NN):
    h, m, l = _split3(a)
    b = b01.astype(BF16)
    return _dot(h, b, dn) + (_dot(m, b, dn) + _dot(l, b, dn))


def _dot_exact_lhs(a01, b, dn=NN):
    h, m, l = _split3(b)
    a = a01.astype(BF16)
    return _dot(a, h, dn) + (_dot(a, m, dn) + _dot(a, l, dn))


def _iota(shape, dim):
    return lax.broadcasted_iota(jnp.int32, shape, dim)


def _sigmoid(x):
    return jax.nn.sigmoid(x)


def _softplus(x):
    return jnp.maximum(x, 0.0) + jnp.log(1.0 + jnp.exp(-jnp.abs(x)))


def _block_ones():
    return (_iota((GROUP_W, GROUP_W), 0) // HEAD_DIM == _iota((GROUP_W, GROUP_W), 1) // HEAD_DIM).astype(F32)


def _head_sum(x, bo):
    return _dot_exact_rhs(x, bo)


def _head_sum1(x, bo):
    return _bdot(x, bo)


def _params(sem, vmem_limit=VMEM_LIMIT):
    return pltpu.CompilerParams(dimension_semantics=sem, vmem_limit_bytes=vmem_limit)


def _row_blocks(B, T, tile=ROW_TILE):
    if T >= tile:
        assert T % tile == 0
        return 1, tile
    bb = max(1, min(B, tile // T))
    while B % bb:
        bb -= 1
    return bb, T


def _ada_kernel(c_ref, w_ref, b_ref, o_ref):
    c = c_ref[...]
    o_ref[...] = _bdot(c * _sigmoid(c), w_ref[...]) + b_ref[...]


def _ada_call(c_all, ada_w, ada_b):
    Ld, D, N6 = ada_w.shape
    NB = c_all.shape[0]
    tn = 1024
    return pl.pallas_call(
        _ada_kernel,
        grid=(Ld, N6 // tn),
        in_specs=[pl.BlockSpec((NB, D), lambda l, j: (0, 0)),
                  pl.BlockSpec((None, D, tn), lambda l, j: (l, 0, j)),
                  pl.BlockSpec((None, 1, tn), lambda l, j: (l, 0, j))],
        out_specs=pl.BlockSpec((None, NB, tn), lambda l, j: (l, 0, j)),
        out_shape=jax.ShapeDtypeStruct((Ld, NB, N6), F32),
        compiler_params=_params(("arbitrary", "arbitrary")),
        name="ada_mod",
    )(c_all, ada_w, ada_b.reshape(Ld, 1, N6))


IN_WIDTHS = (RW_COLS, ML_MAIN, GATE_PAD, GROUP_W, GROUP_W, GROUP_W, GROUP_W)


def _in_kernel(x_ref, sc_ref, sh_ref, g_ref, w_ref, wt_ref, *rest, n_rows_out, n_alias):
    out_refs = rest[n_alias:]
    x = x_ref[...]
    bb, tt, D = x.shape
    y = x * lax.rsqrt(jnp.mean(x * x, -1, keepdims=True) + NORM_EPS) * g_ref[...]
    h = y * (1.0 + sc_ref[...]) + sh_ref[...]
    hb = h.reshape(bb * tt, D).astype(BF16)
    off = 0
    for ref in out_refs[:n_rows_out]:
        n = ref.shape[-1]
        ref[...] = _dot(hb, w_ref[:, off:off + n]).reshape(bb, tt, n)
        off += n
    for i, ref in enumerate(out_refs[n_rows_out:]):
        ref[0] = _dot(wt_ref[i * GROUP_W:(i + 1) * GROUP_W, :], hb, NT)


def _in_call(x, sc, sh, g, wpack, wkv_t, transposed_kv, layer, depth, prev_kv):
    B, T, D = x.shape
    bb, tt = _row_blocks(B, T)
    row = lambda n: pl.BlockSpec((bb, tt, n), lambda b, j: (b, j, 0))
    mod = pl.BlockSpec((bb, 1, D), lambda b, j: (b, 0, 0))
    widths = IN_WIDTHS[:-1] if transposed_kv else IN_WIDTHS
    out_specs = [row(n) for n in widths]
    out_shape = [jax.ShapeDtypeStruct((B, T, n), F32) for n in widths]
    inputs = [x, sc, sh, g, wpack, wkv_t]
    in_specs = [row(D), mod, mod,
                pl.BlockSpec((1, D), lambda b, j: (0, 0)),
                pl.BlockSpec(wpack.shape, lambda b, j: (0, 0)),
                pl.BlockSpec(wkv_t.shape, lambda b, j: (0, 0))]
    aliases = {}
    if transposed_kv:
        assert bb == 1
        out_specs += [pl.BlockSpec((None, 1, GROUP_W, tt), lambda b, j: (layer, b, 0, j))] * 2
        out_shape += [jax.ShapeDtypeStruct((depth, B, GROUP_W, T), F32)] * 2
        if prev_kv is not None:
            aliases = {len(inputs) + i: len(widths) + i for i in range(2)}
            inputs += list(prev_kv)
            in_specs += [pl.BlockSpec(memory_space=pl.ANY)] * 2
    return pl.pallas_call(
        functools.partial(_in_kernel, n_rows_out=len(widths), n_alias=len(aliases)),
        grid=(B // bb, T // tt),
        in_specs=in_specs,
        out_specs=out_specs,
        out_shape=out_shape,
        input_output_aliases=aliases,
        compiler_params=_params(("arbitrary", "arbitrary")),
        name="in_proj",
    )(*inputs)


def _rwkv_kernel(u_ref, s0_ref, wkv0_ref, mu_ref, vec_ref, wup_ref, aup_ref, gup_ref, *rest, L, bb, nc):
    y_ref, wkv_ref, zbuf, s_sc = rest[-4:]
    j = pl.program_id(1)
    G, HD = GROUP_W, HEAD_DIM
    TT = nc * L
    R = bb * TT

    @pl.when(j == 0)
    def _():
        zbuf[:, 7:8, :] = s0_ref[...]
        s_sc[...] = wkv0_ref[...]

    u3 = u_ref[...]
    zbuf[:, 8:8 + TT, :] = u3
    prev = zbuf[:, 7:7 + TT, :]
    xs = (u3 + (prev - u3) * mu_ref[...]).reshape(R, RW_COLS)
    zbuf[:, 7:8, :] = u3[:, TT - 1:TT, :]

    r = xs[:, 0:G]
    k = xs[:, G:2 * G]
    v = xs[:, 2 * G:3 * G]
    wc = xs[:, 3 * G:3 * G + 64]
    ac = xs[:, 3 * G + 64:3 * G + 128]
    gc = xs[:, 3 * G + 128:4 * G]
    w0, a0, k_k, k_a = vec_ref[0:1, :], vec_ref[1:2, :], vec_ref[2:3, :], vec_ref[3:4, :]
    ln_g, ln_b, r_k = vec_ref[4:5, :], vec_ref[5:6, :], vec_ref[6:7, :]

    w_log = -_softplus(-(w0 + _bdot(jnp.tanh(wc), wup_ref[...]))) - 0.5
    lw = -jnp.exp(w_log)
    a = _sigmoid(a0 + _bdot(ac, aup_ref[...]))
    g = _bdot(_sigmoid(gc), gup_ref[...])
    bo = _block_ones()
    kk = k * k_k
    kk = kk / jnp.maximum(jnp.sqrt(_head_sum(kk * kk, bo)), 1e-12)
    k2 = k * (1.0 + (a - 1.0) * k_a)
    beta = kk * a

    row = _iota((L, L), 0)
    col = _iota((L, L), 1)
    incl = row >= col
    strict = row > col
    eye = (row == col).astype(F32)
    groups = range(bb * nc)
    c_parts = [_dot_exact_lhs(incl.astype(F32), lw[ci * L:(ci + 1) * L]) for ci in groups]
    c = jnp.concatenate(c_parts, axis=0) if len(c_parts) > 1 else c_parts[0]
    cl_parts = [jnp.broadcast_to(x[L - 1:L, :], (L, G)) for x in c_parts]
    cl = jnp.concatenate(cl_parts, axis=0) if len(cl_parts) > 1 else cl_parts[0]
    e_inv = jnp.exp(-c)
    e_tail = jnp.exp(cl - c)
    a_t = -kk * jnp.exp(c - lw)
    b_t = beta * e_inv
    k_t = k2 * e_inv
    r_t = r * jnp.exp(c)
    b_l = beta * e_tail
    k_l = k2 * e_tail
    e_cl = jnp.exp(cl)
    n_double = max(1, int(math.ceil(math.log2(L)))) - 1

    chains = [(ci, h) for ci in range(bb * nc) for h in range(N_HEADS)]

    def cut(x, ch):
        ci, h = ch
        return x[ci * L:(ci + 1) * L, h * HD:(h + 1) * HD]

    a_s = {ch: _split2(cut(a_t, ch)) for ch in chains}
    b_s = {ch: _split2(cut(b_t, ch)) for ch in chains}
    r_b = {ch: cut(r_t, ch).astype(BF16) for ch in chains}
    k_b = {ch: cut(k_t, ch).astype(BF16) for ch in chains}
    v_b = {ch: cut(v, ch).astype(BF16) for ch in chains}
    n_ab = {ch: jnp.where(strict, _dot3s(a_s[ch], b_s[ch], NT), 0.0) for ch in chains}
    n_ak = {ch: jnp.where(strict, _dot(a_s[ch][0], k_b[ch], NT), 0.0).astype(BF16) for ch in chains}
    n_rb = {ch: jnp.where(incl, _dot(r_b[ch], b_s[ch][0], NT), 0.0).astype(BF16) for ch in chains}
    n_rk = {ch: jnp.where(incl, _dot(r_b[ch], k_b[ch], NT), 0.0).astype(BF16) for ch in chains}
    p = {ch: eye + n_ab[ch] for ch in chains}
    m_s = {ch: _split2(n_ab[ch]) for ch in chains}
    for _ in range(n_double):
        m_s = {ch: _split2(_dot3s(m_s[ch], m_s[ch])) for ch in chains}
        p = {ch: p[ch] + _dot3s(m_s[ch], _split2(p[ch])) for ch in chains}
    p_b = {ch: p[ch].astype(BF16) for ch in chains}
    akv = {ch: _dot(n_ak[ch], v_b[ch]).astype(BF16) for ch in chains}
    y_v = {ch: _dot(n_rk[ch], v_b[ch]) for ch in chains}
    w_b = {ch: _dot(p_b[ch], a_s[ch][0]).astype(BF16) for ch in chains}
    u_t = {ch: _dot(p_b[ch], akv[ch]) for ch in chains}
    bl_b = {ch: cut(b_l, ch).astype(BF16) for ch in chains}
    kl_b = {ch: cut(k_l, ch).astype(BF16) for ch in chains}

    heads = range(N_HEADS)
    bh = [(b, h) for b in range(bb) for h in heads]
    s = {k_: s_sc[k_[0], k_[1]] for k_ in bh}
    y_chunk = {}
    for cc in range(nc):
        ch_of = {(b, h): (b * nc + cc, h) for b, h in bh}
        s_b = {k_: s[k_].astype(BF16) for k_ in bh}
        e_b = {k_: (_dot(w_b[ch_of[k_]], s_b[k_], NT) + u_t[ch_of[k_]]).astype(BF16) for k_ in bh}
        s = {k_: s[k_] * cut(e_cl, ch_of[k_])[0:1, :]
             + _dot(e_b[k_], bl_b[ch_of[k_]], TN) + _dot(v_b[ch_of[k_]], kl_b[ch_of[k_]], TN) for k_ in bh}
        yh = {k_: _dot(r_b[ch_of[k_]], s_b[k_], NT) + _dot(n_rb[ch_of[k_]], e_b[k_]) + y_v[ch_of[k_]] for k_ in bh}
        for b in range(bb):
            y_chunk[b * nc + cc] = jnp.concatenate([yh[b, h] for h in heads], axis=-1)
    for k_ in bh:
        s_sc[k_[0], k_[1]] = s[k_]

    ys = [y_chunk[ci] for ci in range(bb * nc)]
    y = jnp.concatenate(ys, axis=0) if len(ys) > 1 else ys[0]
    mu_y = _head_sum1(y, bo) * (1.0 / HD)
    yc = y - mu_y
    var = _head_sum1(yc * yc, bo) * (1.0 / HD)
    yn = yc * lax.rsqrt(var + RW_GN_EPS) * ln_g + ln_b
    bonus = _head_sum1(r * k2 * r_k, bo) * v
    y_ref[...] = ((yn + bonus) * g).reshape(bb, TT, G)

    @pl.when(j == pl.num_programs(1) - 1)
    def _():
        wkv_ref[...] = s_sc[...]


def _state_slot(B, bb, layer, depth, prev, n_inputs, out_index):
    spec = pl.BlockSpec((None, bb, N_HEADS, HEAD_DIM, HEAD_DIM), lambda b, j: (layer, b, 0, 0, 0))
    shape = jax.ShapeDtypeStruct((depth, B, N_HEADS, HEAD_DIM, HEAD_DIM), F32)
    if prev is None:
        return spec, shape, [], [], {}
    return spec, shape, [prev], [pl.BlockSpec(memory_space=pl.ANY)], {n_inputs: out_index}


def _rwkv_call(u_rw, shift0, wkv0, mu, vec, wup, aup, gup, layer, depth, prev_wkv):
    B, T, _ = u_rw.shape
    L = math.gcd(T, RW_CHUNK)
    bb = math.gcd(B, RW_CHUNKS_PER_STEP)
    nc = math.gcd(T // L, RW_CHUNKS_PER_STEP // bb)
    TT = nc * L
    full = lambda a: pl.BlockSpec(a.shape, lambda b, j: (0,) * a.ndim)
    inputs = [u_rw, shift0, wkv0, mu, vec, wup, aup, gup]
    st_spec, st_shape, extra_in, extra_specs, aliases = _state_slot(B, bb, layer, depth, prev_wkv, len(inputs), 1)
    return pl.pallas_call(
        functools.partial(_rwkv_kernel, L=L, bb=bb, nc=nc),
        grid=(B // bb, T // TT),
        in_specs=[pl.BlockSpec((bb, TT, RW_COLS), lambda b, j: (b, j, 0)),
                  pl.BlockSpec((bb, 1, RW_COLS), lambda b, j: (b, 0, 0)),
                  pl.BlockSpec((bb, N_HEADS, HEAD_DIM, HEAD_DIM), lambda b, j: (b, 0, 0, 0)),
                  full(mu), full(vec), full(wup), full(aup), full(gup)] + extra_specs,
        out_specs=[pl.BlockSpec((bb, TT, GROUP_W), lambda b, j: (b, j, 0)), st_spec],
        out_shape=[jax.ShapeDtypeStruct((B, T, GROUP_W), F32), st_shape],
        input_output_aliases=aliases,
        scratch_shapes=[pltpu.VMEM((bb, 8 + TT, RW_COLS), F32),
                        pltpu.VMEM((bb, N_HEADS, HEAD_DIM, HEAD_DIM), F32)],
        compiler_params=_params(("arbitrary", "arbitrary")),
        name="rwkv7",
    )(*inputs, *extra_in)


def _mlstm_kernel(u_ref, gt_ref, cb_ref, c0_ref, n0_ref, m0_ref, cw_ref, cbias_ref, gbias_ref, ng_ref,
                  *rest, L, bb, nc):
    y_ref, c_ref, n_ref, m_ref, zbuf, c_sc, n_sc, m_sc = rest[-8:]
    j = pl.program_id(1)
    G, HD = GROUP_W, HEAD_DIM
    TT = nc * L
    R = bb * TT

    @pl.when(j == 0)
    def _():
        zbuf[:, 5:8, :] = cb_ref[...]
        c_sc[...] = c0_ref[...]
        n_sc[...] = n0_ref[...]
        m_sc[...] = m0_ref[...]

    u3 = u_ref[...]
    zbuf[:, 8:8 + TT, :] = u3[:, :, 0:2 * G]
    conv = cbias_ref[...]
    for t in range(4):
        conv = conv + zbuf[:, 5 + t:5 + t + TT, :] * cw_ref[t:t + 1, :]
    tail = zbuf[:, 5 + TT:8 + TT, :]
    zbuf[:, 5:8, :] = tail
    conv = conv.reshape(R, 2 * G)
    u = u3.reshape(R, ML_MAIN)
    sq = conv * _sigmoid(conv)
    q = sq[:, 0:G]
    k = sq[:, G:2 * G] * (HD ** -0.5)
    v = u[:, 2 * G:3 * G]
    o = u[:, 3 * G:4 * G]

    gates = gt_ref[...].reshape(R, GATE_PAD) + gbias_ref[...]
    lane = _iota((1, GATE_PAD), 1)
    gl = jnp.where(lane < N_HEADS, gates, -_softplus(-gates))
    row = _iota((L, L), 0)
    col = _iota((L, L), 1)
    causal = row >= col
    groups = range(bb * nc)
    sel = jnp.concatenate([(row <= col).astype(F32), (row == col).astype(F32)], axis=1)
    bcol_parts = [_dot_exact_lhs(causal.astype(F32), gl[ci * L:(ci + 1) * L]) for ci in groups]
    brow_parts = [_dot_exact_rhs(gl[ci * L:(ci + 1) * L], sel, TN) for ci in groups]
    bcol = jnp.concatenate(bcol_parts, axis=0) if len(bcol_parts) > 1 else bcol_parts[0]

    heads = range(N_HEADS)
    chains = [(ci, h) for ci in range(bb * nc) for h in heads]

    def cut(x, ch):
        ci, h = ch
        return x[ci * L:(ci + 1) * L, h * HD:(h + 1) * HD]

    def rows_of(x, ch, lane0):
        ci, h = ch
        return x[ci * L:(ci + 1) * L, lane0 + h:lane0 + h + 1]

    bc = {ch: rows_of(bcol, ch, N_HEADS) for ch in chains}
    ic = {ch: rows_of(gl, ch, 0) for ch in chains}
    g_tot = {(ci, h): bcol_parts[ci][L - 1:L, N_HEADS + h:N_HEADS + h + 1] for ci, h in chains}
    br = {(ci, h): brow_parts[ci][N_HEADS + h:N_HEADS + h + 1, 0:L] for ci, h in chains}
    ir = {(ci, h): brow_parts[ci][h:h + 1, L:2 * L] for ci, h in chains}
    q_f = {ch: cut(q, ch) for ch in chains}
    k_f = {ch: cut(k, ch) for ch in chains}
    q_b = {ch: q_f[ch].astype(BF16) for ch in chains}
    k_b = {ch: k_f[ch].astype(BF16) for ch in chains}
    v_f = {ch: cut(v, ch) for ch in chains}
    v_b = {ch: v_f[ch].astype(BF16) for ch in chains}
    log_d = {ch: jnp.where(causal, bc[ch] - br[ch] + ir[ch], -jnp.inf) for ch in chains}
    m_loc = {ch: jnp.max(log_d[ch], -1, keepdims=True) for ch in chains}
    s0 = {ch: _dot(q_b[ch], k_b[ch], NT) * jnp.exp(log_d[ch] - m_loc[ch]) for ch in chains}
    s_sum = {ch: jnp.sum(s0[ch], -1, keepdims=True) for ch in chains}
    sv = {ch: _dot(s0[ch].astype(BF16), v_b[ch]) for ch in chains}
    logw = {ch: g_tot[ch] - bc[ch] + ic[ch] for ch in chains}
    m_w = {ch: jnp.max(logw[ch], 0, keepdims=True) for ch in chains}
    w_loc = {ch: jnp.exp(logw[ch] - m_w[ch]) for ch in chains}
    kv = {ch: _dot((v_f[ch] * w_loc[ch]).astype(BF16), k_b[ch], TN) for ch in chains}
    n_loc = {ch: jnp.sum(k_f[ch] * w_loc[ch], 0, keepdims=True) for ch in chains}

    bh = [(b, h) for b in range(bb) for h in heads]
    m_vec = {b: m_sc[b] for b in range(bb)}
    m = {(b, h): m_vec[b][:, h:h + 1] for b, h in bh}
    c = {k_: c_sc[k_[0], k_[1]] for k_ in bh}
    n = {(b, h): n_sc[b, h:h + 1, :] for b, h in bh}
    h_chunk = {}
    for cc in range(nc):
        ch_of = {(b, h): (b * nc + cc, h) for b, h in bh}
        qc = {k_: _dot(q_b[ch_of[k_]], c[k_].astype(BF16), NT) for k_ in bh}
        qn = {k_: jnp.sum(q_f[ch_of[k_]] * n[k_], -1, keepdims=True) for k_ in bh}
        inter = {k_: bc[ch_of[k_]] + m[k_] for k_ in bh}
        m_row = {k_: jnp.maximum(m_loc[ch_of[k_]], inter[k_]) for k_ in bh}
        m_new = {k_: jnp.maximum(g_tot[ch_of[k_]] + m[k_], m_w[ch_of[k_]]) for k_ in bh}
        dec = {k_: jnp.exp(g_tot[ch_of[k_]] + m[k_] - m_new[k_]) for k_ in bh}
        f2 = {k_: jnp.exp(m_w[ch_of[k_]] - m_new[k_]) for k_ in bh}
        c = {k_: dec[k_] * c[k_] + f2[k_] * kv[ch_of[k_]] for k_ in bh}
        n = {k_: dec[k_] * n[k_] + f2[k_] * n_loc[ch_of[k_]] for k_ in bh}
        m = m_new
        f1 = {k_: jnp.exp(m_loc[ch_of[k_]] - m_row[k_]) for k_ in bh}
        w_int = {k_: jnp.exp(inter[k_] - m_row[k_]) for k_ in bh}
        num = {k_: f1[k_] * sv[ch_of[k_]] + w_int[k_] * qc[k_] for k_ in bh}
        den = {k_: f1[k_] * s_sum[ch_of[k_]] + w_int[k_] * qn[k_] for k_ in bh}
        hh = {k_: num[k_] / jnp.maximum(jnp.abs(den[k_]), jnp.exp(-m_row[k_])) for k_ in bh}
        for b in range(bb):
            h_chunk[b * nc + cc] = jnp.concatenate([hh[b, h] for h in heads], axis=-1)
    for b in range(bb):
        m_out = m_vec[b]
        for h in heads:
            c_sc[b, h] = c[b, h]
            n_sc[b, h:h + 1, :] = n[b, h]
            m_out = jnp.where(lane == h, m[b, h], m_out)
        m_sc[b] = m_out

    hs = [h_chunk[ci] for ci in range(bb * nc)]
    hcat = jnp.concatenate(hs, axis=0) if len(hs) > 1 else hs[0]
    bo = _block_ones()
    mu_h = _head_sum1(hcat, bo) * (1.0 / HD)
    hc = hcat - mu_h
    var = _head_sum1(hc * hc, bo) * (1.0 / HD)
    y_ref[...] = (hc * lax.rsqrt(var + ML_NORM_EPS) * ng_ref[...] * _sigmoid(o)).reshape(bb, TT, G)

    @pl.when(j == pl.num_programs(1) - 1)
    def _():
        c_ref[...] = c_sc[...]
        n_ref[...] = n_sc[...]
        m_ref[...] = m_sc[...]


def _mlstm_call(u_ml, gates, conv0, c0, n0, m0, cw, cbias, gbias, ng, layer, depth, prev_c):
    B, T, _ = u_ml.shape
    L = math.gcd(T, ML_CHUNK)
    bb = math.gcd(B, ML_CHUNKS_PER_STEP)
    nc = math.gcd(T // L, ML_CHUNKS_PER_STEP // bb)
    TT = nc * L
    full = lambda a: pl.BlockSpec(a.shape, lambda b, j: (0,) * a.ndim)
    st3 = pl.BlockSpec((bb, N_HEADS, HEAD_DIM), lambda b, j: (b, 0, 0))
    stm = pl.BlockSpec((bb, 1, GATE_PAD), lambda b, j: (b, 0, 0))
    inputs = [u_ml, gates, conv0, c0, n0, m0, cw, cbias, gbias, ng]
    c_spec, c_shape, extra_in, extra_specs, aliases = _state_slot(B, bb, layer, depth, prev_c, len(inputs), 1)
    return pl.pallas_call(
        functools.partial(_mlstm_kernel, L=L, bb=bb, nc=nc),
        grid=(B // bb, T // TT),
        in_specs=[pl.BlockSpec((bb, TT, ML_MAIN), lambda b, j: (b, j, 0)),
                  pl.BlockSpec((bb, TT, GATE_PAD), lambda b, j: (b, j, 0)),
                  pl.BlockSpec((bb, 3, 2 * GROUP_W), lambda b, j: (b, 0, 0)),
                  pl.BlockSpec((bb, N_HEADS, HEAD_DIM, HEAD_DIM), lambda b, j: (b, 0, 0, 0)),
                  st3, stm, full(cw), full(cbias), full(gbias), full(ng)] + extra_specs,
        out_specs=[pl.BlockSpec((bb, TT, GROUP_W), lambda b, j: (b, j, 0)), c_spec, st3, stm],
        out_shape=[jax.ShapeDtypeStruct((B, T, GROUP_W), F32),
                   c_shape,
                   jax.ShapeDtypeStruct((B, N_HEADS, HEAD_DIM), F32),
                   jax.ShapeDtypeStruct((B, 1, GATE_PAD), F32)],
        input_output_aliases=aliases,
        scratch_shapes=[pltpu.VMEM((bb, 8 + TT, 2 * GROUP_W), F32),
                        pltpu.VMEM((bb, N_HEADS, HEAD_DIM, HEAD_DIM), F32),
                        pltpu.VMEM((bb, N_HEADS, HEAD_DIM), F32),
                        pltpu.VMEM((bb, 1, GATE_PAD), F32)],
        compiler_params=_params(("arbitrary", "arbitrary")),
        name="mlstm",
    )(*inputs, *extra_in)


def _pool_kernel(u_ref, hist_ref, w_ref, scale_ref, y_ref, zbuf, *, pos0):
    j = pl.program_id(1)
    bb, tt, G = u_ref.shape
    gc = G // len(POOL_WINDOWS)

    @pl.when(j == 0)
    def _():
        zbuf[:, 0:1, :] = jnp.zeros((bb, 1, G), F32)
        zbuf[:, 1:16, :] = hist_ref[...]

    u = u_ref[...]
    zbuf[:, 16:16 + tt, :] = u
    pos = pos0 + j * tt + _iota((1, tt, 1), 1)
    slab = 128
    parts = []
    for c0 in range(0, G, slab):
        windows = [w for gi, w in enumerate(POOL_WINDOWS) if c0 <= gi * gc < c0 + slab]
        lane = c0 + _iota((1, 1, slab), 2)
        acc = jnp.zeros((bb, tt, slab), F32)
        part = jnp.zeros((bb, tt, slab), F32)
        for t in range(max(windows)):
            acc = acc + zbuf[:, 16 - t:16 - t + tt, c0:c0 + slab]
            if (t + 1) in windows:
                inv_cnt = 1.0 / jnp.minimum(pos + 1, t + 1).astype(F32)
                part = jnp.where(lane // gc == POOL_WINDOWS.index(t + 1), acc * inv_cnt, part)
        parts.append(part)
    tail = zbuf[:, tt:tt + 16, :]
    zbuf[:, 0:16, :] = tail
    pooled = (jnp.concatenate(parts, axis=-1) - u).reshape(bb * tt, G)
    y_ref[...] = (_bdot(pooled, w_ref[...]) * scale_ref[...]).reshape(bb, tt, G)


def _pool_call(u_pool, hist, wbd, scale, pos0):
    B, T, G = u_pool.shape
    bb, tt = _row_blocks(B, T)
    return pl.pallas_call(
        functools.partial(_pool_kernel, pos0=pos0),
        grid=(B // bb, T // tt),
        in_specs=[pl.BlockSpec((bb, tt, G), lambda b, j: (b, j, 0)),
                  pl.BlockSpec((bb, POOL_HIST, G), lambda b, j: (b, 0, 0)),
                  pl.BlockSpec((G, G), lambda b, j: (0, 0)),
                  pl.BlockSpec((1, G), lambda b, j: (0, 0))],
        out_specs=pl.BlockSpec((bb, tt, G), lambda b, j: (b, j, 0)),
        out_shape=jax.ShapeDtypeStruct((B, T, G), F32),
        scratch_shapes=[pltpu.VMEM((bb, 16 + tt, G), F32)],
        compiler_params=_params(("arbitrary", "arbitrary")),
        name="pool",
    )(u_pool, hist, wbd, scale)


def _slope_of(head):
    return jnp.where(head == 0, 2.0 ** -2, jnp.where(head == 1, 2.0 ** -4, jnp.where(head == 2, 2.0 ** -6, 2.0 ** -8)))


def _topk_mask(gate, valid, nblk):
    blk = _iota((1, nblk), 1)
    cnt = jnp.zeros(gate.shape, F32)
    for n in range(nblk):
        gn = gate[:, n:n + 1]
        ahead = (gn > gate) | ((gn == gate) & (n < blk))
        cnt = cnt + jnp.where(ahead, 1.0, 0.0)
    return valid & (cnt < MOBA_TOPK)


def _moba_p_kernel(q_ref, k_ref, vt_ref, o_ref, mean_sc, kb_sc, vt_sc, sel_sc, *, nblk):
    i = pl.program_id(1)
    BLK, G, HD, H = MOBA_BLOCK, GROUP_W, HEAD_DIM, N_HEADS
    W = H * BLK

    SLAB = 2 * HD
    lane_s = _iota((1, SLAB), 1)

    def slab_of(x, h, extra):
        xs = x[:, (h // 2) * SLAB:(h // 2 + 1) * SLAB]
        own = (lane_s // HD) == (h % 2)
        spare = HD if h % 2 == 0 else 0
        return jnp.where(own, xs, jnp.where(lane_s == spare, extra, 0.0))

    @pl.when(i == 0)
    def _():
        pos = _iota((BLK, 1), 0).astype(F32)
        for n in range(nblk):
            kblk = k_ref[0, n * BLK:(n + 1) * BLK, :]
            mean_sc[n:n + 1, :] = jnp.mean(kblk, axis=0, keepdims=True)
            for h in range(H):
                kb_sc[n, h] = slab_of(kblk, h, pos).astype(BF16)
            vt_sc[n] = vt_ref[0, :, n * BLK:(n + 1) * BLK].astype(BF16)

    q = q_ref[0]
    lane_head = _iota((1, G), 1) // HD
    qbd = jnp.concatenate([jnp.where(lane_head == h, q, 0.0) for h in range(H)], axis=0)
    slopes = [2.0 ** (-2.0 * (h + 1)) for h in range(H)]
    scale = HD ** -0.5
    qs = [slab_of(q * scale, h, slopes[h]).astype(BF16) for h in range(H)]

    blk_row = _iota((nblk, 1), 0)
    valid = blk_row < i
    gate = jnp.where(valid, _dot3(mean_sc[...], qbd, NT), -jnp.inf)
    cnt = jnp.zeros((nblk, W), F32)
    for n in range(nblk):
        gn = gate[n:n + 1, :]
        ahead = (gn > gate) | ((gn == gate) & (n < blk_row))
        cnt = cnt + jnp.where(ahead, 1.0, 0.0)
    sel = jnp.where(valid & (cnt < MOBA_TOPK), 1.0, 0.0)
    for n in range(nblk):
        sel_sc[n] = sel[n:n + 1, :]

    heads = range(H)

    def scores(jb):
        return [_dot(kb_sc[jb, h], qs[h], NT) for h in heads]

    def pv(jb, p):
        vt = vt_sc[jb]
        return [_dot(vt[h * HD:(h + 1) * HD, :], p[h].astype(BF16)) for h in heads]

    causal = _iota((BLK, BLK), 0) <= _iota((BLK, BLK), 1)
    s = [jnp.where(causal, x, -jnp.inf) for x in scores(i)]
    m0 = [jnp.max(x, 0, keepdims=True) for x in s]
    p = [jnp.exp(s[h] - m0[h]) for h in heads]
    l0 = [jnp.sum(x, 0, keepdims=True) for x in p]
    acc0 = pv(i, p)

    def update(jbs, carry):
        m, l, acc = carry
        s = []
        for jb in jbs:
            s_raw = scores(jb)
            off = ((i - jb) * BLK).astype(F32)
            s.append([s_raw[h] + jnp.where(sel_sc[jb, :, h * BLK:(h + 1) * BLK] > 0.0, -slopes[h] * off, -jnp.inf)
                      for h in heads])
        m2 = m
        for x in s:
            m2 = [jnp.maximum(m2[h], jnp.max(x[h], 0, keepdims=True)) for h in heads]
        alpha = [jnp.exp(m[h] - m2[h]) for h in heads]
        p = [jnp.concatenate([jnp.exp(x[h] - m2[h]) for x in s], axis=0) for h in heads]
        vt = [jnp.concatenate([vt_sc[jb][h * HD:(h + 1) * HD, :] for jb in jbs], axis=1) for h in heads]
        new = [_dot(vt[h], p[h].astype(BF16)) for h in heads]
        l = [alpha[h] * l[h] + jnp.sum(p[h], 0, keepdims=True) for h in heads]
        acc = [alpha[h] * acc[h] + new[h] for h in heads]
        return m2, l, acc

    carry = lax.fori_loop(0, i // 2, lambda jp, c: update((2 * jp, 2 * jp + 1), c), (m0, l0, acc0))
    m, l, acc = lax.fori_loop(0, i % 2, lambda _, c: update((i - 1,), c), carry)
    out_t = jnp.concatenate([acc[h] / l[h] for h in heads], axis=0)
    o_ref[0] = out_t.T


def _moba_prompt_call(q, k, v_t, layer):
    B, T, G = q.shape
    assert T % MOBA_BLOCK == 0
    nblk = T // MOBA_BLOCK
    return pl.pallas_call(
        functools.partial(_moba_p_kernel, nblk=nblk),
        grid=(B, nblk),
        in_specs=[pl.BlockSpec((1, MOBA_BLOCK, G), lambda b, i: (b, i, 0)),
                  pl.BlockSpec((1, T, G), lambda b, i: (b, 0, 0)),
                  pl.BlockSpec((None, 1, G, T), lambda b, i: (layer, b, 0, 0))],
        out_specs=pl.BlockSpec((1, MOBA_BLOCK, G), lambda b, i: (b, i, 0)),
        out_shape=jax.ShapeDtypeStruct((B, T, G), F32),
        scratch_shapes=[pltpu.VMEM((nblk, G), F32),
                        pltpu.VMEM((nblk, N_HEADS, MOBA_BLOCK, 2 * HEAD_DIM), BF16),
                        pltpu.VMEM((nblk, G, MOBA_BLOCK), BF16),
                        pltpu.VMEM((nblk, 1, N_HEADS * MOBA_BLOCK), F32)],
        compiler_params=_params(("arbitrary", "arbitrary")),
        name="moba_prompt",
    )(q, k, v_t)


def _moba_s_kernel(pt_ref, q_ref, kn_ref, vn_ref, *refs, n_pages, page, past_len, rb):
    kp = refs[:rb * n_pages]
    vp = refs[rb * n_pages:2 * rb * n_pages]
    o_ref = refs[2 * rb * n_pages]
    G, HD = GROUP_W, HEAD_DIM
    ts = q_ref.shape[1]
    R = N_HEADS * ts
    ppb = MOBA_BLOCK // page
    nb = n_pages // ppb
    rows = range(rb)
    pages = range(n_pages)
    lane_head = _iota((1, G), 1) // HD
    row_head = _iota((R, 1), 0) // ts
    row_t = _iota((R, 1), 0) % ts
    slope = _slope_of(row_head)
    scale = HD ** -0.5
    qbd = [jnp.concatenate([jnp.where(lane_head == h, q_ref[r], 0.0) for h in range(N_HEADS)], axis=0)
           for r in rows]
    qb = [x.astype(BF16) for x in qbd]

    kpg = {(r, p): kp[r * n_pages + p][...] for r in rows for p in pages}
    raw = {(r, p): _dot(qb[r], kpg[r, p].astype(BF16)) for r in rows for p in pages}
    rsum = {(r, p): jnp.sum(raw[r, p], axis=1, keepdims=True) for r in rows for p in pages}
    gate = [jnp.concatenate(
        [sum(rsum[r, p] for p in range(n * ppb, (n + 1) * ppb)) * (1.0 / MOBA_BLOCK) for n in range(nb)], axis=1)
        for r in rows]
    all_valid = _iota((1, nb), 1) >= 0
    sel = [jnp.where(_topk_mask(gate[r], all_valid, nb), 1.0, 0.0) for r in rows]

    q_pos = (past_len + row_t).astype(F32)
    off = _iota((1, page), 1).astype(F32)
    scores = {(r, p): jnp.where(sel[r][:, p // ppb:p // ppb + 1] > 0.0,
                                raw[r, p] * scale - slope * (q_pos - (p * page + off)), -jnp.inf)
              for r in rows for p in pages}
    t_new = _iota((1, ts), 1)
    s_own = [jnp.where(t_new <= row_t,
                       _dot(qb[r], kn_ref[r].astype(BF16), NT) * scale - slope * (row_t - t_new).astype(F32),
                       -jnp.inf) for r in rows]

    m = [jnp.max(s_own[r], -1, keepdims=True) for r in rows]
    for p in pages:
        m = [jnp.maximum(m[r], jnp.max(scores[r, p], -1, keepdims=True)) for r in rows]
    p_own = [jnp.exp(s_own[r] - m[r]) for r in rows]
    l = [jnp.sum(p_own[r], -1, keepdims=True) for r in rows]
    acc = [_dot(p_own[r].astype(BF16), vn_ref[r].astype(BF16)) for r in rows]
    for p in pages:
        pr = [jnp.exp(scores[r, p] - m[r]) for r in rows]
        l = [l[r] + jnp.sum(pr[r], -1, keepdims=True) for r in rows]
        acc = [acc[r] + _dot(pr[r].astype(BF16), vp[r * n_pages + p][...].astype(BF16), NT) for r in rows]
    for r in rows:
        out = acc[r] / l[r]
        y = jnp.zeros((ts, G), F32)
        for h in range(N_HEADS):
            y = jnp.where(lane_head == h, out[h * ts:(h + 1) * ts, :], y)
        o_ref[r] = y


def _moba_sample_call(q, k, v, cache_k, cache_v, page_table, layer):
    DB, TS, G = q.shape
    Ld, n_phys, page, H, d = cache_k.shape
    n_pages = page_table.shape[1]
    past_len = n_pages * page
    assert past_len % MOBA_BLOCK == 0 and MOBA_BLOCK % page == 0 and past_len // MOBA_BLOCK >= MOBA_TOPK
    ck = jnp.transpose(cache_k, (0, 1, 3, 4, 2)).reshape(Ld, n_phys, H * d, page)
    cv = jnp.transpose(cache_v, (0, 1, 3, 4, 2)).reshape(Ld, n_phys, H * d, page)
    rb = math.gcd(DB, MOBA_SAMPLE_ROWS_PER_STEP)
    new = pl.BlockSpec((rb, TS, G), lambda b, pt: (b, 0, 0))
    pg = [pl.BlockSpec((None, None, G, page),
                       functools.partial(lambda b, pt, r, p: (layer, pt[b * rb + r, p], 0, 0), r=r, p=p))
          for r in range(rb) for p in range(n_pages)]
    return pl.pallas_call(
        functools.partial(_moba_s_kernel, n_pages=n_pages, page=page, past_len=past_len, rb=rb),
        grid_spec=pltpu.PrefetchScalarGridSpec(
            num_scalar_prefetch=1, grid=(DB // rb,),
            in_specs=[new, new, new] + pg + pg,
            out_specs=new),
        out_shape=jax.ShapeDtypeStruct((DB, TS, G), F32),
        compiler_params=_params(("arbitrary",)),
        name="moba_sample",
    )(page_table, q, k, v, *([ck] * (rb * n_pages)), *([cv] * (rb * n_pages)))


def _out_kernel(yrw_ref, yml_ref, ypl_ref, yat_ref, x_ref, g1_ref, sc2_ref, sh2_ref, g2_ref, n2g_ref, fng_ref,
                wout_ref, wup_ref, wdn_ref, o_ref, h2_sc, acc_sc, *, final):
    f = pl.program_id(2)
    bb, tt, D = x_ref.shape
    M = bb * tt

    @pl.when(f == 0)
    def _():
        ycat = jnp.concatenate([r[...].reshape(M, GROUP_W) for r in (yrw_ref, yml_ref, ypl_ref, yat_ref)], axis=-1)
        y = _dot(ycat.astype(BF16), wout_ref[...]).reshape(bb, tt, D)
        x1 = x_ref[...] + g1_ref[...] * y
        o_ref[...] = x1
        hn = x1 * lax.rsqrt(jnp.mean(x1 * x1, -1, keepdims=True) + NORM_EPS) * n2g_ref[...]
        h2_sc[...] = (hn * (1.0 + sc2_ref[...]) + sh2_ref[...]).reshape(M, D).astype(BF16)
        acc_sc[...] = jnp.zeros((M, D), F32)

    a = _dot(h2_sc[...], wup_ref[...])
    a = jnp.square(jnp.maximum(a, 0.0)).astype(BF16)
    acc_sc[...] += _dot(a, wdn_ref[...])

    @pl.when(f == pl.num_programs(2) - 1)
    def _():
        x2 = o_ref[...] + g2_ref[...] * acc_sc[...].reshape(bb, tt, D)
        if final:
            x2 = x2 * lax.rsqrt(jnp.mean(x2 * x2, -1, keepdims=True) + NORM_EPS) * fng_ref[...]
        o_ref[...] = x2


def _out_call(ys, x, g1, sc2, sh2, g2, n2g, fng, wout, wup, wdn, layer, final):
    B, T, D = x.shape
    bb, tt = _row_blocks(B, T, OUT_ROW_TILE)
    FF = wup.shape[2]
    row = lambda n: pl.BlockSpec((bb, tt, n), lambda b, j, f: (b, j, 0))
    mod = pl.BlockSpec((bb, 1, D), lambda b, j, f: (b, 0, 0))
    vec = pl.BlockSpec((1, D), lambda b, j, f: (0, 0))
    return pl.pallas_call(
        functools.partial(_out_kernel, final=final),
        grid=(B // bb, T // tt, FF // FF_TILE),
        in_specs=[row(GROUP_W)] * 4 + [row(D), mod, mod, mod, mod, vec, vec,
                  pl.BlockSpec((None, D, D), lambda b, j, f: (layer, 0, 0)),
                  pl.BlockSpec((None, D, FF_TILE), lambda b, j, f: (layer, 0, f)),
                  pl.BlockSpec((None, FF_TILE, D), lambda b, j, f: (layer, f, 0))],
        out_specs=row(D),
        out_shape=jax.ShapeDtypeStruct((B, T, D), F32),
        scratch_shapes=[pltpu.VMEM((bb * tt, D), BF16),
                        pltpu.VMEM((bb * tt, D), F32)],
        compiler_params=_params(("arbitrary", "arbitrary", "arbitrary"), OUT_VMEM_LIMIT),
        name="out_mlp",
    )(*ys, x, g1, sc2, sh2, g2, n2g, fng, wout, wup, wdn)


def _rw_perm(x):
    return jnp.concatenate([x[..., 0:256], x[..., 320:576], x[..., 576:832],
                            x[..., 256:320], x[..., 832:896], x[..., 896:1024]], axis=-1)


def _rw_unperm(x):
    return jnp.concatenate([x[..., 0:256], x[..., 768:832], x[..., 256:512],
                            x[..., 512:768], x[..., 832:896], x[..., 896:1024]], axis=-1)


def _layer(x, mods, st, pos0, attend, lw, final, transposed_kv, prev):
    sh1, sc1, g1, sh2, sc2, g2 = mods
    shift0, wkv0, conv0, c0, n0, m0, pool0 = st
    B, T, D = x.shape
    layer, depth = lw['layer'], lw['depth']
    prev_kv, prev_wkv, prev_c = prev if prev is not None else (None, None, None)
    outs = _in_call(x, sc1, sh1, lw['norm1_g'], lw['wpack'], lw['wkv_t'], transposed_kv, layer, depth, prev_kv)
    u_rw, u_ml, gates, u_pool, q, k = outs[:6]

    y_rw, wkv1 = _rwkv_call(u_rw, _rw_perm(shift0)[:, None, :], wkv0, lw['rw_mu'], lw['rw_vec'],
                            lw['rwkv_w_up'], lw['rwkv_a_up'], lw['rwkv_g_up'], layer, depth, prev_wkv)
    shift1 = _rw_unperm(u_rw[:, -1, :])

    m0p = jnp.pad(m0, ((0, 0), (0, GATE_PAD - N_HEADS)))[:, None, :]
    y_ml, c1, n1, m1p = _mlstm_call(u_ml, gates, conv0, c0, n0, m0p, lw['mlstm_conv_w'], lw['ml_cbias'],
                                    lw['ml_gbias'], lw['ml_ng'], layer, depth, prev_c)
    m1 = m1p[:, 0, :N_HEADS]
    zc = u_ml[:, :, :2 * GROUP_W] if T >= 3 else jnp.concatenate([conv0, u_ml[:, :, :2 * GROUP_W]], 1)
    conv1 = zc[:, -3:]

    y_pool = _pool_call(u_pool, pool0, lw['pool_wbd'], lw['pool_scale'], pos0)
    zp = u_pool if T >= POOL_HIST else jnp.concatenate([pool0, u_pool], 1)
    pool1 = zp[:, -POOL_HIST:]

    if transposed_kv:
        k_out, v_out = outs[6], outs[7]
        y_at = attend(q, k, v_out)
    else:
        kv_shape = (B, T, N_HEADS, HEAD_DIM)
        y_at = attend(q, k, outs[6])
        k_out, v_out = k.reshape(kv_shape), outs[6].reshape(kv_shape)

    x_new = _out_call((y_rw, y_ml, y_pool, y_at), x, g1, sc2, sh2, g2, lw['norm2_g'], lw['final_g'],
                      lw['w_out'], lw['mlp_up'], lw['mlp_down'], lw['layer'], final)
    return x_new, (shift1, wkv1, conv1, c1, n1, m1, pool1, k_out, v_out)


def kernel(x_prompt, x_sample, c_prompt, c_sample, state_rwkv_shift, state_rwkv_wkv, state_mlstm_conv, state_mlstm_c, state_mlstm_n, state_mlstm_m, state_pool, cache_k, cache_v, page_table, ada_w, ada_b, norm1_g, norm2_g, w_in, w_out, rwkv_mu, rwkv_w0, rwkv_w_up, rwkv_a0, rwkv_a_up, rwkv_g_up, rwkv_k_k, rwkv_k_a, rwkv_r_k, rwkv_ln_g, rwkv_ln_b, mlstm_conv_w, mlstm_conv_b, mlstm_i_b, mlstm_f_b, mlstm_norm_g, pool_w, pool_scale, mlp_up, mlp_down, final_norm_g):
    B, T, D = x_prompt.shape
    DB = x_sample.shape[0]
    depth = ada_w.shape[0]
    G = GROUP_W
    assert D == D_MODEL and w_in.shape[-1] == N_IN

    mod = _ada_call(jnp.concatenate([c_prompt, c_sample], 0), ada_w, ada_b)

    st_p0 = (jnp.zeros((B, RW_COLS), F32), jnp.zeros((B, N_HEADS, HEAD_DIM, HEAD_DIM), F32),
             jnp.zeros((B, 3, 2 * G), F32), jnp.zeros((B, N_HEADS, HEAD_DIM, HEAD_DIM), F32),
             jnp.zeros((B, N_HEADS, HEAD_DIM), F32), jnp.zeros((B, N_HEADS), F32),
             jnp.zeros((B, POOL_HIST, G), F32))
    past_len = page_table.shape[1] * cache_k.shape[2]

    w_out_b, mlp_up_b, mlp_down_b = w_out.astype(BF16), mlp_up.astype(BF16), mlp_down.astype(BF16)
    xp, xs = x_prompt, x_sample
    new_p, new_s = [], []
    prev_p = prev_s = None
    for l in range(depth):
        wl = w_in[l]
        wpack = jnp.concatenate([
            _rw_perm(wl[:, :RW_COLS]),
            wl[:, RW_COLS:RW_COLS + ML_MAIN],
            jnp.pad(wl[:, RW_COLS + ML_MAIN:RW_COLS + ML_MAIN + ML_GATES], ((0, 0), (0, GATE_PAD - ML_GATES))),
            wl[:, RW_COLS + ML_MAIN + ML_GATES:],
        ], axis=1).astype(BF16)
        pool_wbd = jnp.zeros((G, G), F32)
        gcw = G // len(POOL_WINDOWS)
        for gi in range(len(POOL_WINDOWS)):
            pool_wbd = pool_wbd.at[gi * gcw:(gi + 1) * gcw, gi * gcw:(gi + 1) * gcw].set(pool_w[l, gi])
        row = lambda a: a.reshape(1, -1)
        lw = {
            'norm1_g': row(norm1_g[l]), 'norm2_g': row(norm2_g[l]), 'final_g': row(final_norm_g),
            'wpack': wpack, 'wkv_t': wl[:, N_IN - 2 * G:].T.astype(BF16), 'layer': l, 'depth': depth,
            'w_out': w_out_b, 'mlp_up': mlp_up_b, 'mlp_down': mlp_down_b,
            'rw_mu': row(_rw_perm(rwkv_mu[l])),
            'rw_vec': jnp.stack([rwkv_w0[l], rwkv_a0[l], rwkv_k_k[l], rwkv_k_a[l], rwkv_ln_g[l], rwkv_ln_b[l],
                                 rwkv_r_k[l].reshape(-1), jnp.zeros((G,), F32)]),
            'rwkv_w_up': rwkv_w_up[l], 'rwkv_a_up': rwkv_a_up[l], 'rwkv_g_up': rwkv_g_up[l],
            'mlstm_conv_w': mlstm_conv_w[l], 'ml_cbias': row(mlstm_conv_b[l]),
            'ml_gbias': row(jnp.pad(jnp.concatenate([mlstm_i_b[l], mlstm_f_b[l]]), (0, GATE_PAD - ML_GATES))),
            'ml_ng': row(mlstm_norm_g[l]),
            'pool_wbd': pool_wbd, 'pool_scale': row(pool_scale[l]),
        }
        mods = [mod[l, :, i * D:(i + 1) * D][:, None, :] for i in range(6)]
        mods_p = [m[:B] for m in mods]
        mods_s = [m[B:] for m in mods]
        final = l == depth - 1

        xp, st_p = _layer(xp, mods_p, st_p0, 0, functools.partial(_moba_prompt_call, layer=l), lw, final, True,
                          prev_p)
        prev_p = (st_p[7:9], st_p[1], st_p[3])
        st_s_in = (state_rwkv_shift[l], state_rwkv_wkv[l], state_mlstm_conv[l], state_mlstm_c[l],
                   state_mlstm_n[l], state_mlstm_m[l], state_pool[l])
        attend_s = functools.partial(_moba_sample_call, cache_k=cache_k, cache_v=cache_v,
                                     page_table=page_table, layer=l)
        xs, st_s = _layer(xs, mods_s, st_s_in, past_len, attend_s, lw, final, False, prev_s)
        prev_s = (None, st_s[1], st_s[3])
        new_p.append(st_p)
        new_s.append(st_s)

    def collect(lst, n):
        return tuple(lst[-1][i] if i in (1, 3) else jnp.stack([st[i] for st in lst]) for i in range(n))

    kv_out = tuple(jnp.transpose(a.reshape(depth, B, N_HEADS, HEAD_DIM, T), (0, 1, 4, 2, 3)) for a in prev_p[0])
    return (xp, xs) + collect(new_p, 7) + kv_out + collect(new_s, 9)
```

```python
import functools
import math

import jax
import jax.numpy as jnp
from jax import lax
from jax.experimental import pallas as pl
from jax.experimental.pallas import tpu as pltpu

F32 = jnp.float32
BF16 = jnp.bfloat16

D_MODEL = 1024
GROUP_W = 256
HEAD_DIM = 64
N_HEADS = 4
RW_COLS = 1024
ML_MAIN = 1024
ML_GATES = 8
GATE_PAD = 128
N_IN = 3080
MOBA_BLOCK = 256
MOBA_TOPK = 3
ML_CHUNK = 256
RW_CHUNK = 64
RW_CHUNKS_PER_STEP = 8
ML_CHUNKS_PER_STEP = 4
MOBA_SAMPLE_ROWS_PER_STEP = 4
POOL_WINDOWS = (2, 4, 8, 16)
POOL_HIST = 15
NORM_EPS = 1e-6
RW_GN_EPS = 64e-5
ML_NORM_EPS = 1e-6
ROW_TILE = 512
FF_TILE = 1024
VMEM_LIMIT = 48 * 1024 * 1024
OUT_ROW_TILE = 1024
OUT_VMEM_LIMIT = 60 * 1024 * 1024

NN = (((1,), (0,)), ((), ()))
NT = (((1,), (1,)), ((), ()))
TN = (((0,), (0,)), ((), ()))


def _dot(a, b, dn=NN):
    return lax.dot_general(a, b, dn, preferred_element_type=F32)


def _bdot(a, b, dn=NN):
    return _dot(a.astype(BF16), b.astype(BF16), dn)


def _split2(x):
    hi = x.astype(BF16)
    lo = (x - hi.astype(F32)).astype(BF16)
    return hi, lo


def _split3(x):
    hi = x.astype(BF16)
    r = x - hi.astype(F32)
    mid = r.astype(BF16)
    lo = (r - mid.astype(F32)).astype(BF16)
    return hi, mid, lo


def _dot3(a, b, dn=NN):
    ah, al = _split2(a)
    bh, bl = _split2(b)
    return _dot(ah, bh, dn) + (_dot(ah, bl, dn) + _dot(al, bh, dn))


def _dot3s(a_s, b_s, dn=NN):
    return _dot(a_s[0], b_s[0], dn) + (_dot(a_s[0], b_s[1], dn) + _dot(a_s[1], b_s[0], dn))


def _dot_exact_rhs(a, b01, dn=NN):
    h, m, l = _split3(a)
    b = b01.astype(BF16)
    return _dot(h, b, dn) + (_dot(m, b, dn) + _dot(l, b, dn))


def _dot_exact_lhs(a01, b, dn=NN):
    h, m, l = _split3(b)
    a = a01.astype(BF16)
    return _dot(a, h, dn) + (_dot(a, m, dn) + _dot(a, l, dn))


def _iota(shape, dim):
    return lax.broadcasted_iota(jnp.int32, shape, dim)


def _sigmoid(x):
    return jax.nn.sigmoid(x)


def _softplus(x):
    return jnp.maximum(x, 0.0) + jnp.log(1.0 + jnp.exp(-jnp.abs(x)))


def _block_ones():
    return (_iota((GROUP_W, GROUP_W), 0) // HEAD_DIM == _iota((GROUP_W, GROUP_W), 1) // HEAD_DIM).astype(F32)


def _head_sum(x, bo):
    return _dot_exact_rhs(x, bo)


def _head_sum1(x, bo):
    return _bdot(x, bo)


def _params(sem, vmem_limit=VMEM_LIMIT):
    return pltpu.CompilerParams(dimension_semantics=sem, vmem_limit_bytes=vmem_limit)


def _row_blocks(B, T, tile=ROW_TILE):
    if T >= tile:
        assert T % tile == 0
        return 1, tile
    bb = max(1, min(B, tile // T))
    while B % bb:
        bb -= 1
    return bb, T


def _ada_kernel(c_ref, w_ref, b_ref, o_ref):
    c = c_ref[...]
    o_ref[...] = _bdot(c * _sigmoid(c), w_ref[...]) + b_ref[...]


def _ada_call(c_all, ada_w, ada_b):
    Ld, D, N6 = ada_w.shape
    NB = c_all.shape[0]
    tn = 1024
    return pl.pallas_call(
        _ada_kernel,
        grid=(Ld, N6 // tn),
        in_specs=[pl.BlockSpec((NB, D), lambda l, j: (0, 0)),
                  pl.BlockSpec((None, D, tn), lambda l, j: (l, 0, j)),
                  pl.BlockSpec((None, 1, tn), lambda l, j: (l, 0, j))],
        out_specs=pl.BlockSpec((None, NB, tn), lambda l, j: (l, 0, j)),
        out_shape=jax.ShapeDtypeStruct((Ld, NB, N6), F32),
        compiler_params=_params(("arbitrary", "arbitrary")),
        name="ada_mod",
    )(c_all, ada_w, ada_b.reshape(Ld, 1, N6))


IN_WIDTHS = (RW_COLS, ML_MAIN, GATE_PAD, GROUP_W, GROUP_W, GROUP_W, GROUP_W)


def _in_kernel(x_ref, sc_ref, sh_ref, g_ref, w_ref, wt_ref, *rest, n_rows_out, n_alias):
    out_refs = rest[n_alias:]
    x = x_ref[...]
    bb, tt, D = x.shape
    y = x * lax.rsqrt(jnp.mean(x * x, -1, keepdims=True) + NORM_EPS) * g_ref[...]
    h = y * (1.0 + sc_ref[...]) + sh_ref[...]
    hb = h.reshape(bb * tt, D).astype(BF16)
    off = 0
    for ref in out_refs[:n_rows_out]:
        n = ref.shape[-1]
        ref[...] = _dot(hb, w_ref[:, off:off + n]).reshape(bb, tt, n)
        off += n
    for i, ref in enumerate(out_refs[n_rows_out:]):
        ref[0] = _dot(wt_ref[i * GROUP_W:(i + 1) * GROUP_W, :], hb, NT)


def _mod_spec(bb, D, layer, row0, chunk, n_grid):
    assert row0 % bb == 0
    if n_grid == 2:
        return pl.BlockSpec((None, bb, 1, D), lambda b, j: (layer, row0 // bb + b, 0, chunk))
    return pl.BlockSpec((None, bb, 1, D), lambda b, j, f: (layer, row0 // bb + b, 0, chunk))


def _in_call(x, mod4, row0, g, wpack, wkv_t, transposed_kv, layer, depth, prev_kv):
    B, T, D = x.shape
    bb, tt = _row_blocks(B, T)
    row = lambda n: pl.BlockSpec((bb, tt, n), lambda b, j: (b, j, 0))
    widths = IN_WIDTHS[:-1] if transposed_kv else IN_WIDTHS
    out_specs = [row(n) for n in widths]
    out_shape = [jax.ShapeDtypeStruct((B, T, n), F32) for n in widths]
    inputs = [x, mod4, mod4, g, wpack, wkv_t]
    in_specs = [row(D), _mod_spec(bb, D, layer, row0, 1, 2), _mod_spec(bb, D, layer, row0, 0, 2),
                pl.BlockSpec((1, D), lambda b, j: (0, 0)),
                pl.BlockSpec(wpack.shape, lambda b, j: (0, 0)),
                pl.BlockSpec(wkv_t.shape, lambda b, j: (0, 0))]
    aliases = {}
    if transposed_kv:
        assert bb == 1
        out_specs += [pl.BlockSpec((None, 1, GROUP_W, tt), lambda b, j: (layer, b, 0, j))] * 2
        out_shape += [jax.ShapeDtypeStruct((depth, B, GROUP_W, T), F32)] * 2
        if prev_kv is not None:
            aliases = {len(inputs) + i: len(widths) + i for i in range(2)}
            inputs += list(prev_kv)
            in_specs += [pl.BlockSpec(memory_space=pl.ANY)] * 2
    return pl.pallas_call(
        functools.partial(_in_kernel, n_rows_out=len(widths), n_alias=len(aliases)),
        grid=(B // bb, T // tt),
        in_specs=in_specs,
        out_specs=out_specs,
        out_shape=out_shape,
        input_output_aliases=aliases,
        compiler_params=_params(("arbitrary", "arbitrary")),
        name="in_proj",
    )(*inputs)


def _rwkv_kernel(u_ref, s0_ref, wkv0_ref, mu_ref, vec_ref, wup_ref, aup_ref, gup_ref, *rest, L, bb, nc):
    y_ref, wkv_ref, zbuf, s_sc = rest[-4:]
    j = pl.program_id(1)
    G, HD = GROUP_W, HEAD_DIM
    TT = nc * L
    R = bb * TT

    @pl.when(j == 0)
    def _():
        zbuf[:, 7:8, :] = s0_ref[...]
        s_sc[...] = wkv0_ref[...]

    u3 = u_ref[...]
    zbuf[:, 8:8 + TT, :] = u3
    prev = zbuf[:, 7:7 + TT, :]
    xs = (u3 + (prev - u3) * mu_ref[...]).reshape(R, RW_COLS)
    zbuf[:, 7:8, :] = u3[:, TT - 1:TT, :]

    r = xs[:, 0:G]
    k = xs[:, G:2 * G]
    v = xs[:, 2 * G:3 * G]
    wc = xs[:, 3 * G:3 * G + 64]
    ac = xs[:, 3 * G + 64:3 * G + 128]
    gc = xs[:, 3 * G + 128:4 * G]
    w0, a0, k_k, k_a = vec_ref[0:1, :], vec_ref[1:2, :], vec_ref[2:3, :], vec_ref[3:4, :]
    ln_g, ln_b, r_k = vec_ref[4:5, :], vec_ref[5:6, :], vec_ref[6:7, :]

    w_log = -_softplus(-(w0 + _bdot(jnp.tanh(wc), wup_ref[...]))) - 0.5
    lw = -jnp.exp(w_log)
    a = _sigmoid(a0 + _bdot(ac, aup_ref[...]))
    g = _bdot(_sigmoid(gc), gup_ref[...])
    bo = _block_ones()
    kk = k * k_k
    kk = kk / jnp.maximum(jnp.sqrt(_head_sum(kk * kk, bo)), 1e-12)
    k2 = k * (1.0 + (a - 1.0) * k_a)
    beta = kk * a

    row = _iota((L, L), 0)
    col = _iota((L, L), 1)
    incl = row >= col
    strict = row > col
    eye = (row == col).astype(F32)
    groups = range(bb * nc)
    c_parts = [_dot_exact_lhs(incl.astype(F32), lw[ci * L:(ci + 1) * L]) for ci in groups]
    c = jnp.concatenate(c_parts, axis=0) if len(c_parts) > 1 else c_parts[0]
    cl_parts = [jnp.broadcast_to(x[L - 1:L, :], (L, G)) for x in c_parts]
    cl = jnp.concatenate(cl_parts, axis=0) if len(cl_parts) > 1 else cl_parts[0]
    e_inv = jnp.exp(-c)
    e_tail = jnp.exp(cl - c)
    a_t = -kk * jnp.exp(c - lw)
    b_t = beta * e_inv
    k_t = k2 * e_inv
    r_t = r * jnp.exp(c)
    b_l = beta * e_tail
    k_l = k2 * e_tail
    e_cl = jnp.exp(cl)
    n_double = max(1, int(math.ceil(math.log2(L)))) - 1

    chains = [(ci, h) for ci in range(bb * nc) for h in range(N_HEADS)]

    def cut(x, ch):
        ci, h = ch
        return x[ci * L:(ci + 1) * L, h * HD:(h + 1) * HD]

    a_s = {ch: _split2(cut(a_t, ch)) for ch in chains}
    b_s = {ch: _split2(cut(b_t, ch)) for ch in chains}
    r_b = {ch: cut(r_t, ch).astype(BF16) for ch in chains}
    k_b = {ch: cut(k_t, ch).astype(BF16) for ch in chains}
    v_b = {ch: cut(v, ch).astype(BF16) for ch in chains}
    n_ab = {ch: jnp.where(strict, _dot3s(a_s[ch], b_s[ch], NT), 0.0) for ch in chains}
    n_ak = {ch: jnp.where(strict, _dot(a_s[ch][0], k_b[ch], NT), 0.0).astype(BF16) for ch in chains}
    n_rb = {ch: jnp.where(incl, _dot(r_b[ch], b_s[ch][0], NT), 0.0).astype(BF16) for ch in chains}
    n_rk = {ch: jnp.where(incl, _dot(r_b[ch], k_b[ch], NT), 0.0).astype(BF16) for ch in chains}
    p = {ch: eye + n_ab[ch] for ch in chains}
    m_s = {ch: _split2(n_ab[ch]) for ch in chains}
    for _ in range(n_double):
        m_s = {ch: _split2(_dot3s(m_s[ch], m_s[ch])) for ch in chains}
        p = {ch: p[ch] + _dot3s(m_s[ch], _split2(p[ch])) for ch in chains}
    p_b = {ch: p[ch].astype(BF16) for ch in chains}
    akv = {ch: _dot(n_ak[ch], v_b[ch]).astype(BF16) for ch in chains}
    y_v = {ch: _dot(n_rk[ch], v_b[ch]) for ch in chains}
    w_b = {ch: _dot(p_b[ch], a_s[ch][0]).astype(BF16) for ch in chains}
    u_t = {ch: _dot(p_b[ch], akv[ch]) for ch in chains}
    bl_b = {ch: cut(b_l, ch).astype(BF16) for ch in chains}
    kl_b = {ch: cut(k_l, ch).astype(BF16) for ch in chains}

    heads = range(N_HEADS)
    bh = [(b, h) for b in range(bb) for h in heads]
    s = {k_: s_sc[k_[0], k_[1]] for k_ in bh}
    y_chunk = {}
    for cc in range(nc):
        ch_of = {(b, h): (b * nc + cc, h) for b, h in bh}
        s_b = {k_: s[k_].astype(BF16) for k_ in bh}
        e_b = {k_: (_dot(w_b[ch_of[k_]], s_b[k_], NT) + u_t[ch_of[k_]]).astype(BF16) for k_ in bh}
        s = {k_: s[k_] * cut(e_cl, ch_of[k_])[0:1, :]
             + _dot(e_b[k_], bl_b[ch_of[k_]], TN) + _dot(v_b[ch_of[k_]], kl_b[ch_of[k_]], TN) for k_ in bh}
        yh = {k_: _dot(r_b[ch_of[k_]], s_b[k_], NT) + _dot(n_rb[ch_of[k_]], e_b[k_]) + y_v[ch_of[k_]] for k_ in bh}
        for b in range(bb):
            y_chunk[b * nc + cc] = jnp.concatenate([yh[b, h] for h in heads], axis=-1)
    for k_ in bh:
        s_sc[k_[0], k_[1]] = s[k_]

    ys = [y_chunk[ci] for ci in range(bb * nc)]
    y = jnp.concatenate(ys, axis=0) if len(ys) > 1 else ys[0]
    mu_y = _head_sum1(y, bo) * (1.0 / HD)
    yc = y - mu_y
    var = _head_sum1(yc * yc, bo) * (1.0 / HD)
    yn = yc * lax.rsqrt(var + RW_GN_EPS) * ln_g + ln_b
    bonus = _head_sum1(r * k2 * r_k, bo) * v
    y_ref[...] = ((yn + bonus) * g).reshape(bb, TT, G)

    @pl.when(j == pl.num_programs(1) - 1)
    def _():
        wkv_ref[...] = s_sc[...]


def _state_slot(B, bb, layer, depth, prev, n_inputs, out_index):
    spec = pl.BlockSpec((None, bb, N_HEADS, HEAD_DIM, HEAD_DIM), lambda b, j: (layer, b, 0, 0, 0))
    shape = jax.ShapeDtypeStruct((depth, B, N_HEADS, HEAD_DIM, HEAD_DIM), F32)
    if prev is None:
        return spec, shape, [], [], {}
    return spec, shape, [prev], [pl.BlockSpec(memory_space=pl.ANY)], {n_inputs: out_index}


def _rwkv_call(u_rw, shift0, wkv0, mu, vec, wup, aup, gup, layer, depth, prev_wkv):
    B, T, _ = u_rw.shape
    L = math.gcd(T, RW_CHUNK)
    bb = math.gcd(B, RW_CHUNKS_PER_STEP)
    nc = math.gcd(T // L, RW_CHUNKS_PER_STEP // bb)
    TT = nc * L
    full = lambda a: pl.BlockSpec(a.shape, lambda b, j: (0,) * a.ndim)
    inputs = [u_rw, shift0, wkv0, mu, vec, wup, aup, gup]
    st_spec, st_shape, extra_in, extra_specs, aliases = _state_slot(B, bb, layer, depth, prev_wkv, len(inputs), 1)
    return pl.pallas_call(
        functools.partial(_rwkv_kernel, L=L, bb=bb, nc=nc),
        grid=(B // bb, T // TT),
        in_specs=[pl.BlockSpec((bb, TT, RW_COLS), lambda b, j: (b, j, 0)),
                  pl.BlockSpec((bb, 1, RW_COLS), lambda b, j: (b, 0, 0)),
                  pl.BlockSpec((bb, N_HEADS, HEAD_DIM, HEAD_DIM), lambda b, j: (b, 0, 0, 0)),
                  full(mu), full(vec), full(wup), full(aup), full(gup)] + extra_specs,
        out_specs=[pl.BlockSpec((bb, TT, GROUP_W), lambda b, j: (b, j, 0)), st_spec],
        out_shape=[jax.ShapeDtypeStruct((B, T, GROUP_W), F32), st_shape],
        input_output_aliases=aliases,
        scratch_shapes=[pltpu.VMEM((bb, 8 + TT, RW_COLS), F32),
                        pltpu.VMEM((bb, N_HEADS, HEAD_DIM, HEAD_DIM), F32)],
        compiler_params=_params(("arbitrary", "arbitrary")),
        name="rwkv7",
    )(*inputs, *extra_in)


def _mlstm_kernel(u_ref, gt_ref, cb_ref, c0_ref, n0_ref, m0_ref, cw_ref, cbias_ref, gbias_ref, ng_ref,
                  *rest, L, bb, nc):
    y_ref, c_ref, n_ref, m_ref, zbuf, c_sc, n_sc, m_sc = rest[-8:]
    j = pl.program_id(1)
    G, HD = GROUP_W, HEAD_DIM
    TT = nc * L
    R = bb * TT

    @pl.when(j == 0)
    def _():
        zbuf[:, 5:8, :] = cb_ref[...]
        c_sc[...] = c0_ref[...]
        n_sc[...] = n0_ref[...]
        m_sc[...] = m0_ref[...]

    u3 = u_ref[...]
    zbuf[:, 8:8 + TT, :] = u3[:, :, 0:2 * G]
    conv = cbias_ref[...]
    for t in range(4):
        conv = conv + zbuf[:, 5 + t:5 + t + TT, :] * cw_ref[t:t + 1, :]
    tail = zbuf[:, 5 + TT:8 + TT, :]
    zbuf[:, 5:8, :] = tail
    conv = conv.reshape(R, 2 * G)
    u = u3.reshape(R, ML_MAIN)
    sq = conv * _sigmoid(conv)
    q = sq[:, 0:G]
    k = sq[:, G:2 * G] * (HD ** -0.5)
    v = u[:, 2 * G:3 * G]
    o = u[:, 3 * G:4 * G]

    gates = gt_ref[...].reshape(R, GATE_PAD) + gbias_ref[...]
    lane = _iota((1, GATE_PAD), 1)
    gl = jnp.where(lane < N_HEADS, gates, -_softplus(-gates))
    row = _iota((L, L), 0)
    col = _iota((L, L), 1)
    causal = row >= col
    groups = range(bb * nc)
    sel = jnp.concatenate([(row <= col).astype(F32), (row == col).astype(F32)], axis=1)
    bcol_parts = [_dot_exact_lhs(causal.astype(F32), gl[ci * L:(ci + 1) * L]) for ci in groups]
    brow_parts = [_dot_exact_rhs(gl[ci * L:(ci + 1) * L], sel, TN) for ci in groups]
    bcol = jnp.concatenate(bcol_parts, axis=0) if len(bcol_parts) > 1 else bcol_parts[0]

    heads = range(N_HEADS)
    chains = [(ci, h) for ci in range(bb * nc) for h in heads]

    def cut(x, ch):
        ci, h = ch
        return x[ci * L:(ci + 1) * L, h * HD:(h + 1) * HD]

    def rows_of(x, ch, lane0):
        ci, h = ch
        return x[ci * L:(ci + 1) * L, lane0 + h:lane0 + h + 1]

    bc = {ch: rows_of(bcol, ch, N_HEADS) for ch in chains}
    ic = {ch: rows_of(gl, ch, 0) for ch in chains}
    g_tot = {(ci, h): bcol_parts[ci][L - 1:L, N_HEADS + h:N_HEADS + h + 1] for ci, h in chains}
    br = {(ci, h): brow_parts[ci][N_HEADS + h:N_HEADS + h + 1, 0:L] for ci, h in chains}
    ir = {(ci, h): brow_parts[ci][h:h + 1, L:2 * L] for ci, h in chains}
    q_f = {ch: cut(q, ch) for ch in chains}
    k_f = {ch: cut(k, ch) for ch in chains}
    q_b = {ch: q_f[ch].astype(BF16) for ch in chains}
    k_b = {ch: k_f[ch].astype(BF16) for ch in chains}
    v_f = {ch: cut(v, ch) for ch in chains}
    v_b = {ch: v_f[ch].astype(BF16) for ch in chains}
    log_d = {ch: jnp.where(causal, bc[ch] - br[ch] + ir[ch], -jnp.inf) for ch in chains}
    m_loc = {ch: jnp.max(log_d[ch], -1, keepdims=True) for ch in chains}
    s0 = {ch: _dot(q_b[ch], k_b[ch], NT) * jnp.exp(log_d[ch] - m_loc[ch]) for ch in chains}
    s_sum = {ch: jnp.sum(s0[ch], -1, keepdims=True) for ch in chains}
    sv = {ch: _dot(s0[ch].astype(BF16), v_b[ch]) for ch in chains}
    logw = {ch: g_tot[ch] - bc[ch] + ic[ch] for ch in chains}
    m_w = {ch: jnp.max(logw[ch], 0, keepdims=True) for ch in chains}
    w_loc = {ch: jnp.exp(logw[ch] - m_w[ch]) for ch in chains}
    kv = {ch: _dot((v_f[ch] * w_loc[ch]).astype(BF16), k_b[ch], TN) for ch in chains}
    n_loc = {ch: jnp.sum(k_f[ch] * w_loc[ch], 0, keepdims=True) for ch in chains}

    bh = [(b, h) for b in range(bb) for h in heads]
    m_vec = {b: m_sc[b] for b in range(bb)}
    m = {(b, h): m_vec[b][:, h:h + 1] for b, h in bh}
    c = {k_: c_sc[k_[0], k_[1]] for k_ in bh}
    n = {(b, h): n_sc[b, h:h + 1, :] for b, h in bh}
    h_chunk = {}
    for cc in range(nc):
        ch_of = {(b, h): (b * nc + cc, h) for b, h in bh}
        qc = {k_: _dot(q_b[ch_of[k_]], c[k_].astype(BF16), NT) for k_ in bh}
        qn = {k_: jnp.sum(q_f[ch_of[k_]] * n[k_], -1, keepdims=True) for k_ in bh}
        inter = {k_: bc[ch_of[k_]] + m[k_] for k_ in bh}
        m_row = {k_: jnp.maximum(m_loc[ch_of[k_]], inter[k_]) for k_ in bh}
        m_new = {k_: jnp.maximum(g_tot[ch_of[k_]] + m[k_], m_w[ch_of[k_]]) for k_ in bh}
        dec = {k_: jnp.exp(g_tot[ch_of[k_]] + m[k_] - m_new[k_]) for k_ in bh}
        f2 = {k_: jnp.exp(m_w[ch_of[k_]] - m_new[k_]) for k_ in bh}
        c = {k_: dec[k_] * c[k_] + f2[k_] * kv[ch_of[k_]] for k_ in bh}
        n = {k_: dec[k_] * n[k_] + f2[k_] * n_loc[ch_of[k_]] for k_ in bh}
        m = m_new
        f1 = {k_: jnp.exp(m_loc[ch_of[k_]] - m_row[k_]) for k_ in bh}
        w_int = {k_: jnp.exp(inter[k_] - m_row[k_]) for k_ in bh}
        num = {k_: f1[k_] * sv[ch_of[k_]] + w_int[k_] * qc[k_] for k_ in bh}
        den = {k_: f1[k_] * s_sum[ch_of[k_]] + w_int[k_] * qn[k_] for k_ in bh}
        hh = {k_: num[k_] / jnp.maximum(jnp.abs(den[k_]), jnp.exp(-m_row[k_])) for k_ in bh}
        for b in range(bb):
            h_chunk[b * nc + cc] = jnp.concatenate([hh[b, h] for h in heads], axis=-1)
    for b in range(bb):
        m_out = m_vec[b]
        for h in heads:
            c_sc[b, h] = c[b, h]
            n_sc[b, h:h + 1, :] = n[b, h]
            m_out = jnp.where(lane == h, m[b, h], m_out)
        m_sc[b] = m_out

    hs = [h_chunk[ci] for ci in range(bb * nc)]
    hcat = jnp.concatenate(hs, axis=0) if len(hs) > 1 else hs[0]
    bo = _block_ones()
    mu_h = _head_sum1(hcat, bo) * (1.0 / HD)
    hc = hcat - mu_h
    var = _head_sum1(hc * hc, bo) * (1.0 / HD)
    y_ref[...] = (hc * lax.rsqrt(var + ML_NORM_EPS) * ng_ref[...] * _sigmoid(o)).reshape(bb, TT, G)

    @pl.when(j == pl.num_programs(1) - 1)
    def _():
        c_ref[...] = c_sc[...]
        n_ref[...] = n_sc[...]
        m_ref[...] = m_sc[...]


def _mlstm_call(u_ml, gates, conv0, c0, n0, m0, cw, cbias, gbias, ng, layer, depth, prev_c):
    B, T, _ = u_ml.shape
    L = math.gcd(T, ML_CHUNK)
    bb = math.gcd(B, ML_CHUNKS_PER_STEP)
    nc = math.gcd(T // L, ML_CHUNKS_PER_STEP // bb)
    TT = nc * L
    full = lambda a: pl.BlockSpec(a.shape, lambda b, j: (0,) * a.ndim)
    st3 = pl.BlockSpec((bb, N_HEADS, HEAD_DIM), lambda b, j: (b, 0, 0))
    stm = pl.BlockSpec((bb, 1, GATE_PAD), lambda b, j: (b, 0, 0))
    inputs = [u_ml, gates, conv0, c0, n0, m0, cw, cbias, gbias, ng]
    c_spec, c_shape, extra_in, extra_specs, aliases = _state_slot(B, bb, layer, depth, prev_c, len(inputs), 1)
    return pl.pallas_call(
        functools.partial(_mlstm_kernel, L=L, bb=bb, nc=nc),
        grid=(B // bb, T // TT),
        in_specs=[pl.BlockSpec((bb, TT, ML_MAIN), lambda b, j: (b, j, 0)),
                  pl.BlockSpec((bb, TT, GATE_PAD), lambda b, j: (b, j, 0)),
                  pl.BlockSpec((bb, 3, 2 * GROUP_W), lambda b, j: (b, 0, 0)),
                  pl.BlockSpec((bb, N_HEADS, HEAD_DIM, HEAD_DIM), lambda b, j: (b, 0, 0, 0)),
                  st3, stm, full(cw), full(cbias), full(gbias), full(ng)] + extra_specs,
        out_specs=[pl.BlockSpec((bb, TT, GROUP_W), lambda b, j: (b, j, 0)), c_spec, st3, stm],
        out_shape=[jax.ShapeDtypeStruct((B, T, GROUP_W), F32),
                   c_shape,
                   jax.ShapeDtypeStruct((B, N_HEADS, HEAD_DIM), F32),
                   jax.ShapeDtypeStruct((B, 1, GATE_PAD), F32)],
        input_output_aliases=aliases,
        scratch_shapes=[pltpu.VMEM((bb, 8 + TT, 2 * GROUP_W), F32),
                        pltpu.VMEM((bb, N_HEADS, HEAD_DIM, HEAD_DIM), F32),
                        pltpu.VMEM((bb, N_HEADS, HEAD_DIM), F32),
                        pltpu.VMEM((bb, 1, GATE_PAD), F32)],
        compiler_params=_params(("arbitrary", "arbitrary")),
        name="mlstm",
    )(*inputs, *extra_in)


def _pool_kernel(u_ref, hist_ref, w_ref, scale_ref, y_ref, zbuf, *, pos0):
    j = pl.program_id(1)
    bb, tt, G = u_ref.shape
    gc = G // len(POOL_WINDOWS)

    @pl.when(j == 0)
    def _():
        zbuf[:, 0:1, :] = jnp.zeros((bb, 1, G), F32)
        zbuf[:, 1:16, :] = hist_ref[...]

    u = u_ref[...]
    zbuf[:, 16:16 + tt, :] = u
    pos = pos0 + j * tt + _iota((1, tt, 1), 1)
    slab = 128
    parts = []
    for c0 in range(0, G, slab):
        windows = [w for gi, w in enumerate(POOL_WINDOWS) if c0 <= gi * gc < c0 + slab]
        lane = c0 + _iota((1, 1, slab), 2)
        acc = jnp.zeros((bb, tt, slab), F32)
        part = jnp.zeros((bb, tt, slab), F32)
        for t in range(max(windows)):
            acc = acc + zbuf[:, 16 - t:16 - t + tt, c0:c0 + slab]
            if (t + 1) in windows:
                inv_cnt = 1.0 / jnp.minimum(pos + 1, t + 1).astype(F32)
                part = jnp.where(lane // gc == POOL_WINDOWS.index(t + 1), acc * inv_cnt, part)
        parts.append(part)
    tail = zbuf[:, tt:tt + 16, :]
    zbuf[:, 0:16, :] = tail
    pooled = (jnp.concatenate(parts, axis=-1) - u).reshape(bb * tt, G)
    y_ref[...] = (_bdot(pooled, w_ref[...]) * scale_ref[...]).reshape(bb, tt, G)


def _pool_call(u_pool, hist, wbd, scale, pos0):
    B, T, G = u_pool.shape
    bb, tt = _row_blocks(B, T)
    return pl.pallas_call(
        functools.partial(_pool_kernel, pos0=pos0),
        grid=(B // bb, T // tt),
        in_specs=[pl.BlockSpec((bb, tt, G), lambda b, j: (b, j, 0)),
                  pl.BlockSpec((bb, POOL_HIST, G), lambda b, j: (b, 0, 0)),
                  pl.BlockSpec((G, G), lambda b, j: (0, 0)),
                  pl.BlockSpec((1, G), lambda b, j: (0, 0))],
        out_specs=pl.BlockSpec((bb, tt, G), lambda b, j: (b, j, 0)),
        out_shape=jax.ShapeDtypeStruct((B, T, G), F32),
        scratch_shapes=[pltpu.VMEM((bb, 16 + tt, G), F32)],
        compiler_params=_params(("arbitrary", "arbitrary")),
        name="pool",
    )(u_pool, hist, wbd, scale)


def _slope_of(head):
    return jnp.where(head == 0, 2.0 ** -2, jnp.where(head == 1, 2.0 ** -4, jnp.where(head == 2, 2.0 ** -6, 2.0 ** -8)))


def _topk_mask(gate, valid, nblk):
    blk = _iota((1, nblk), 1)
    cnt = jnp.zeros(gate.shape, F32)
    for n in range(nblk):
        gn = gate[:, n:n + 1]
        ahead = (gn > gate) | ((gn == gate) & (n < blk))
        cnt = cnt + jnp.where(ahead, 1.0, 0.0)
    return valid & (cnt < MOBA_TOPK)


def _moba_p_kernel(q_ref, k_ref, vt_ref, o_ref, mean_sc, kb_sc, vt_sc, sel_sc, *, nblk):
    i = pl.program_id(1)
    BLK, G, HD, H = MOBA_BLOCK, GROUP_W, HEAD_DIM, N_HEADS
    W = H * BLK

    SLAB = 2 * HD
    lane_s = _iota((1, SLAB), 1)

    def slab_of(x, h, extra):
        xs = x[:, (h // 2) * SLAB:(h // 2 + 1) * SLAB]
        own = (lane_s // HD) == (h % 2)
        spare = HD if h % 2 == 0 else 0
        return jnp.where(own, xs, jnp.where(lane_s == spare, extra, 0.0))

    @pl.when(i == 0)
    def _():
        pos = _iota((BLK, 1), 0).astype(F32)
        for n in range(nblk):
            kblk = k_ref[0, n * BLK:(n + 1) * BLK, :]
            mean_sc[n:n + 1, :] = jnp.mean(kblk, axis=0, keepdims=True)
            for h in range(H):
                kb_sc[n, h] = slab_of(kblk, h, pos).astype(BF16)
            vt_sc[n] = vt_ref[0, :, n * BLK:(n + 1) * BLK].astype(BF16)

    q = q_ref[0]
    lane_head = _iota((1, G), 1) // HD
    qbd = jnp.concatenate([jnp.where(lane_head == h, q, 0.0) for h in range(H)], axis=0)
    slopes = [2.0 ** (-2.0 * (h + 1)) for h in range(H)]
    scale = HD ** -0.5
    qs = [slab_of(q * scale, h, slopes[h]).astype(BF16) for h in range(H)]

    blk_row = _iota((nblk, 1), 0)
    valid = blk_row < i
    gate = jnp.where(valid, _dot3(mean_sc[...], qbd, NT), -jnp.inf)
    cnt = jnp.zeros((nblk, W), F32)
    for n in range(nblk):
        gn = gate[n:n + 1, :]
        ahead = (gn > gate) | ((gn == gate) & (n < blk_row))
        cnt = cnt + jnp.where(ahead, 1.0, 0.0)
    sel = jnp.where(valid & (cnt < MOBA_TOPK), 1.0, 0.0)
    for n in range(nblk):
        sel_sc[n] = sel[n:n + 1, :]

    heads = range(H)

    def scores(jb):
        return [_dot(kb_sc[jb, h], qs[h], NT) for h in heads]

    def pv(jb, p):
        vt = vt_sc[jb]
        return [_dot(vt[h * HD:(h + 1) * HD, :], p[h].astype(BF16)) for h in heads]

    causal = _iota((BLK, BLK), 0) <= _iota((BLK, BLK), 1)
    s = [jnp.where(causal, x, -jnp.inf) for x in scores(i)]
    m0 = [jnp.max(x, 0, keepdims=True) for x in s]
    p = [jnp.exp(s[h] - m0[h]) for h in heads]
    l0 = [jnp.sum(x, 0, keepdims=True) for x in p]
    acc0 = pv(i, p)

    def update(jbs, carry):
        m, l, acc = carry
        s = []
        for jb in jbs:
            s_raw = scores(jb)
            off = ((i - jb) * BLK).astype(F32)
            s.append([s_raw[h] + jnp.where(sel_sc[jb, :, h * BLK:(h + 1) * BLK] > 0.0, -slopes[h] * off, -jnp.inf)
                      for h in heads])
        m2 = m
        for x in s:
            m2 = [jnp.maximum(m2[h], jnp.max(x[h], 0, keepdims=True)) for h in heads]
        alpha = [jnp.exp(m[h] - m2[h]) for h in heads]
        p = [jnp.concatenate([jnp.exp(x[h] - m2[h]) for x in s], axis=0) for h in heads]
        vt = [jnp.concatenate([vt_sc[jb][h * HD:(h + 1) * HD, :] for jb in jbs], axis=1) for h in heads]
        new = [_dot(vt[h], p[h].astype(BF16)) for h in heads]
        l = [alpha[h] * l[h] + jnp.sum(p[h], 0, keepdims=True) for h in heads]
        acc = [alpha[h] * acc[h] + new[h] for h in heads]
        return m2, l, acc

    carry = lax.fori_loop(0, i // 2, lambda jp, c: update((2 * jp, 2 * jp + 1), c), (m0, l0, acc0))
    m, l, acc = lax.fori_loop(0, i % 2, lambda _, c: update((i - 1,), c), carry)
    out_t = jnp.concatenate([acc[h] / l[h] for h in heads], axis=0)
    o_ref[0] = out_t.T


def _moba_prompt_call(q, k, v_t, layer):
    B, T, G = q.shape
    assert T % MOBA_BLOCK == 0
    nblk = T // MOBA_BLOCK
    return pl.pallas_call(
        functools.partial(_moba_p_kernel, nblk=nblk),
        grid=(B, nblk),
        in_specs=[pl.BlockSpec((1, MOBA_BLOCK, G), lambda b, i: (b, i, 0)),
                  pl.BlockSpec((1, T, G), lambda b, i: (b, 0, 0)),
                  pl.BlockSpec((None, 1, G, T), lambda b, i: (layer, b, 0, 0))],
        out_specs=pl.BlockSpec((1, MOBA_BLOCK, G), lambda b, i: (b, i, 0)),
        out_shape=jax.ShapeDtypeStruct((B, T, G), F32),
        scratch_shapes=[pltpu.VMEM((nblk, G), F32),
                        pltpu.VMEM((nblk, N_HEADS, MOBA_BLOCK, 2 * HEAD_DIM), BF16),
                        pltpu.VMEM((nblk, G, MOBA_BLOCK), BF16),
                        pltpu.VMEM((nblk, 1, N_HEADS * MOBA_BLOCK), F32)],
        compiler_params=_params(("arbitrary", "arbitrary")),
        name="moba_prompt",
    )(q, k, v_t)


def _moba_s_kernel(pt_ref, q_ref, kn_ref, vn_ref, *refs, n_pages, page, past_len, rb):
    kp = refs[:rb * n_pages]
    vp = refs[rb * n_pages:2 * rb * n_pages]
    o_ref = refs[2 * rb * n_pages]
    G, HD = GROUP_W, HEAD_DIM
    ts = q_ref.shape[1]
    R = N_HEADS * ts
    ppb = MOBA_BLOCK // page
    nb = n_pages // ppb
    rows = range(rb)
    pages = range(n_pages)
    lane_head = _iota((1, G), 1) // HD
    row_head = _iota((R, 1), 0) // ts
    row_t = _iota((R, 1), 0) % ts
    slope = _slope_of(row_head)
    scale = HD ** -0.5
    qbd = [jnp.concatenate([jnp.where(lane_head == h, q_ref[r], 0.0) for h in range(N_HEADS)], axis=0)
           for r in rows]
    qb = [x.astype(BF16) for x in qbd]

    kpg = {(r, p): kp[r * n_pages + p][...] for r in rows for p in pages}
    raw = {(r, p): _dot(qb[r], kpg[r, p].astype(BF16)) for r in rows for p in pages}
    rsum = {(r, p): jnp.sum(raw[r, p], axis=1, keepdims=True) for r in rows for p in pages}
    gate = [jnp.concatenate(
        [sum(rsum[r, p] for p in range(n * ppb, (n + 1) * ppb)) * (1.0 / MOBA_BLOCK) for n in range(nb)], axis=1)
        for r in rows]
    all_valid = _iota((1, nb), 1) >= 0
    sel = [jnp.where(_topk_mask(gate[r], all_valid, nb), 1.0, 0.0) for r in rows]

    q_pos = (past_len + row_t).astype(F32)
    off = _iota((1, page), 1).astype(F32)
    scores = {(r, p): jnp.where(sel[r][:, p // ppb:p // ppb + 1] > 0.0,
                                raw[r, p] * scale - slope * (q_pos - (p * page + off)), -jnp.inf)
              for r in rows for p in pages}
    t_new = _iota((1, ts), 1)
    s_own = [jnp.where(t_new <= row_t,
                       _dot(qb[r], kn_ref[r].astype(BF16), NT) * scale - slope * (row_t - t_new).astype(F32),
                       -jnp.inf) for r in rows]

    m = [jnp.max(s_own[r], -1, keepdims=True) for r in rows]
    for p in pages:
        m = [jnp.maximum(m[r], jnp.max(scores[r, p], -1, keepdims=True)) for r in rows]
    p_own = [jnp.exp(s_own[r] - m[r]) for r in rows]
    l = [jnp.sum(p_own[r], -1, keepdims=True) for r in rows]
    acc = [_dot(p_own[r].astype(BF16), vn_ref[r].astype(BF16)) for r in rows]
    for p in pages:
        pr = [jnp.exp(scores[r, p] - m[r]) for r in rows]
        l = [l[r] + jnp.sum(pr[r], -1, keepdims=True) for r in rows]
        acc = [acc[r] + _dot(pr[r].astype(BF16), vp[r * n_pages + p][...].astype(BF16), NT) for r in rows]
    for r in rows:
        out = acc[r] / l[r]
        y = jnp.zeros((ts, G), F32)
        for h in range(N_HEADS):
            y = jnp.where(lane_head == h, out[h * ts:(h + 1) * ts, :], y)
        o_ref[r] = y


def _moba_sample_call(q, k, v, cache_k, cache_v, page_table, layer):
    DB, TS, G = q.shape
    Ld, n_phys, page, H, d = cache_k.shape
    n_pages = page_table.shape[1]
    past_len = n_pages * page
    assert past_len % MOBA_BLOCK == 0 and MOBA_BLOCK % page == 0 and past_len // MOBA_BLOCK >= MOBA_TOPK
    ck = jnp.transpose(cache_k, (0, 1, 3, 4, 2)).reshape(Ld, n_phys, H * d, page)
    cv = jnp.transpose(cache_v, (0, 1, 3, 4, 2)).reshape(Ld, n_phys, H * d, page)
    rb = math.gcd(DB, MOBA_SAMPLE_ROWS_PER_STEP)
    new = pl.BlockSpec((rb, TS, G), lambda b, pt: (b, 0, 0))
    pg = [pl.BlockSpec((None, None, G, page),
                       functools.partial(lambda b, pt, r, p: (layer, pt[b * rb + r, p], 0, 0), r=r, p=p))
          for r in range(rb) for p in range(n_pages)]
    return pl.pallas_call(
        functools.partial(_moba_s_kernel, n_pages=n_pages, page=page, past_len=past_len, rb=rb),
        grid_spec=pltpu.PrefetchScalarGridSpec(
            num_scalar_prefetch=1, grid=(DB // rb,),
            in_specs=[new, new, new] + pg + pg,
            out_specs=new),
        out_shape=jax.ShapeDtypeStruct((DB, TS, G), F32),
        compiler_params=_params(("arbitrary",)),
        name="moba_sample",
    )(page_table, q, k, v, *([ck] * (rb * n_pages)), *([cv] * (rb * n_pages)))


def _out_kernel(yrw_ref, yml_ref, ypl_ref, yat_ref, x_ref, g1_ref, sc2_ref, sh2_ref, g2_ref, n2g_ref, fng_ref,
                wout_ref, wup_ref, wdn_ref, o_ref, h2_sc, acc_sc, *, final):
    f = pl.program_id(2)
    bb, tt, D = x_ref.shape
    M = bb * tt

    @pl.when(f == 0)
    def _():
        ycat = jnp.concatenate([r[...].reshape(M, GROUP_W) for r in (yrw_ref, yml_ref, ypl_ref, yat_ref)], axis=-1)
        y = _dot(ycat.astype(BF16), wout_ref[...]).reshape(bb, tt, D)
        x1 = x_ref[...] + g1_ref[...] * y
        o_ref[...] = x1
        hn = x1 * lax.rsqrt(jnp.mean(x1 * x1, -1, keepdims=True) + NORM_EPS) * n2g_ref[...]
        h2_sc[...] = (hn * (1.0 + sc2_ref[...]) + sh2_ref[...]).reshape(M, D).astype(BF16)
        acc_sc[...] = jnp.zeros((M, D), F32)

    a = _dot(h2_sc[...], wup_ref[...])
    a = jnp.square(jnp.maximum(a, 0.0)).astype(BF16)
    acc_sc[...] += _dot(a, wdn_ref[...])

    @pl.when(f == pl.num_programs(2) - 1)
    def _():
        x2 = o_ref[...] + g2_ref[...] * acc_sc[...].reshape(bb, tt, D)
        if final:
            x2 = x2 * lax.rsqrt(jnp.mean(x2 * x2, -1, keepdims=True) + NORM_EPS) * fng_ref[...]
        o_ref[...] = x2


def _out_call(ys, x, mod4, row0, n2g, fng, wout, wup, wdn, layer, final):
    B, T, D = x.shape
    bb, tt = _row_blocks(B, T, OUT_ROW_TILE)
    FF = wup.shape[2]
    row = lambda n: pl.BlockSpec((bb, tt, n), lambda b, j, f: (b, j, 0))
    mod = lambda chunk: _mod_spec(bb, D, layer, row0, chunk, 3)
    vec = pl.BlockSpec((1, D), lambda b, j, f: (0, 0))
    return pl.pallas_call(
        functools.partial(_out_kernel, final=final),
        grid=(B // bb, T // tt, FF // FF_TILE),
        in_specs=[row(GROUP_W)] * 4 + [row(D), mod(2), mod(4), mod(3), mod(5), vec, vec,
                  pl.BlockSpec((None, D, D), lambda b, j, f: (layer, 0, 0)),
                  pl.BlockSpec((None, D, FF_TILE), lambda b, j, f: (layer, 0, f)),
                  pl.BlockSpec((None, FF_TILE, D), lambda b, j, f: (layer, f, 0))],
        out_specs=row(D),
        out_shape=jax.ShapeDtypeStruct((B, T, D), F32),
        scratch_shapes=[pltpu.VMEM((bb * tt, D), BF16),
                        pltpu.VMEM((bb * tt, D), F32)],
        compiler_params=_params(("arbitrary", "arbitrary", "arbitrary"), OUT_VMEM_LIMIT),
        name="out_mlp",
    )(*ys, x, mod4, mod4, mod4, mod4, n2g, fng, wout, wup, wdn)


def _rw_perm(x):
    return jnp.concatenate([x[..., 0:256], x[..., 320:576], x[..., 576:832],
                            x[..., 256:320], x[..., 832:896], x[..., 896:1024]], axis=-1)


def _rw_unperm(x):
    return jnp.concatenate([x[..., 0:256], x[..., 768:832], x[..., 256:512],
                            x[..., 512:768], x[..., 832:896], x[..., 896:1024]], axis=-1)


def _layer(x, mods, st, pos0, attend, lw, final, transposed_kv, prev):
    mod4, row0 = mods
    shift0, wkv0, conv0, c0, n0, m0, pool0 = st
    B, T, D = x.shape
    layer, depth = lw['layer'], lw['depth']
    prev_kv, prev_wkv, prev_c = prev if prev is not None else (None, None, None)
    outs = _in_call(x, mod4, row0, lw['norm1_g'], lw['wpack'], lw['wkv_t'], transposed_kv, layer, depth, prev_kv)
    u_rw, u_ml, gates, u_pool, q, k = outs[:6]

    y_rw, wkv1 = _rwkv_call(u_rw, _rw_perm(shift0)[:, None, :], wkv0, lw['rw_mu'], lw['rw_vec'],
                            lw['rwkv_w_up'], lw['rwkv_a_up'], lw['rwkv_g_up'], layer, depth, prev_wkv)
    shift1 = _rw_unperm(u_rw[:, -1, :])

    m0p = jnp.pad(m0, ((0, 0), (0, GATE_PAD - N_HEADS)))[:, None, :]
    y_ml, c1, n1, m1p = _mlstm_call(u_ml, gates, conv0, c0, n0, m0p, lw['mlstm_conv_w'], lw['ml_cbias'],
                                    lw['ml_gbias'], lw['ml_ng'], layer, depth, prev_c)
    m1 = m1p[:, 0, :N_HEADS]
    zc = u_ml[:, :, :2 * GROUP_W] if T >= 3 else jnp.concatenate([conv0, u_ml[:, :, :2 * GROUP_W]], 1)
    conv1 = zc[:, -3:]

    y_pool = _pool_call(u_pool, pool0, lw['pool_wbd'], lw['pool_scale'], pos0)
    zp = u_pool if T >= POOL_HIST else jnp.concatenate([pool0, u_pool], 1)
    pool1 = zp[:, -POOL_HIST:]

    if transposed_kv:
        k_out, v_out = outs[6], outs[7]
        y_at = attend(q, k, v_out)
    else:
        kv_shape = (B, T, N_HEADS, HEAD_DIM)
        y_at = attend(q, k, outs[6])
        k_out, v_out = k.reshape(kv_shape), outs[6].reshape(kv_shape)

    x_new = _out_call((y_rw, y_ml, y_pool, y_at), x, mod4, row0, lw['norm2_g'], lw['final_g'],
                      lw['w_out'], lw['mlp_up'], lw['mlp_down'], lw['layer'], final)
    return x_new, (shift1, wkv1, conv1, c1, n1, m1, pool1, k_out, v_out)


def kernel(x_prompt, x_sample, c_prompt, c_sample, state_rwkv_shift, state_rwkv_wkv, state_mlstm_conv, state_mlstm_c, state_mlstm_n, state_mlstm_m, state_pool, cache_k, cache_v, page_table, ada_w, ada_b, norm1_g, norm2_g, w_in, w_out, rwkv_mu, rwkv_w0, rwkv_w_up, rwkv_a0, rwkv_a_up, rwkv_g_up, rwkv_k_k, rwkv_k_a, rwkv_r_k, rwkv_ln_g, rwkv_ln_b, mlstm_conv_w, mlstm_conv_b, mlstm_i_b, mlstm_f_b, mlstm_norm_g, pool_w, pool_scale, mlp_up, mlp_down, final_norm_g):
    B, T, D = x_prompt.shape
    DB = x_sample.shape[0]
    depth = ada_w.shape[0]
    G = GROUP_W
    assert D == D_MODEL and w_in.shape[-1] == N_IN

    mod = _ada_call(jnp.concatenate([c_sample, c_prompt], 0), ada_w, ada_b)
    mod4 = mod.reshape(depth, DB + B, 1, 6 * D)

    st_p0 = (jnp.zeros((B, RW_COLS), F32), jnp.zeros((B, N_HEADS, HEAD_DIM, HEAD_DIM), F32),
             jnp.zeros((B, 3, 2 * G), F32), jnp.zeros((B, N_HEADS, HEAD_DIM, HEAD_DIM), F32),
             jnp.zeros((B, N_HEADS, HEAD_DIM), F32), jnp.zeros((B, N_HEADS), F32),
             jnp.zeros((B, POOL_HIST, G), F32))
    past_len = page_table.shape[1] * cache_k.shape[2]

    w_out_b, mlp_up_b, mlp_down_b = w_out.astype(BF16), mlp_up.astype(BF16), mlp_down.astype(BF16)
    xp, xs = x_prompt, x_sample
    new_p, new_s = [], []
    prev_p = prev_s = None
    for l in range(depth):
        wl = w_in[l]
        wpack = jnp.concatenate([
            _rw_perm(wl[:, :RW_COLS]),
            wl[:, RW_COLS:RW_COLS + ML_MAIN],
            jnp.pad(wl[:, RW_COLS + ML_MAIN:RW_COLS + ML_MAIN + ML_GATES], ((0, 0), (0, GATE_PAD - ML_GATES))),
            wl[:, RW_COLS + ML_MAIN + ML_GATES:],
        ], axis=1).astype(BF16)
        pool_wbd = jnp.zeros((G, G), F32)
        gcw = G // len(POOL_WINDOWS)
        for gi in range(len(POOL_WINDOWS)):
            pool_wbd = pool_wbd.at[gi * gcw:(gi + 1) * gcw, gi * gcw:(gi + 1) * gcw].set(pool_w[l, gi])
        row = lambda a: a.reshape(1, -1)
        lw = {
            'norm1_g': row(norm1_g[l]), 'norm2_g': row(norm2_g[l]), 'final_g': row(final_norm_g),
            'wpack': wpack, 'wkv_t': wl[:, N_IN - 2 * G:].T.astype(BF16), 'layer': l, 'depth': depth,
            'w_out': w_out_b, 'mlp_up': mlp_up_b, 'mlp_down': mlp_down_b,
            'rw_mu': row(_rw_perm(rwkv_mu[l])),
            'rw_vec': jnp.stack([rwkv_w0[l], rwkv_a0[l], rwkv_k_k[l], rwkv_k_a[l], rwkv_ln_g[l], rwkv_ln_b[l],
                                 rwkv_r_k[l].reshape(-1), jnp.zeros((G,), F32)]),
            'rwkv_w_up': rwkv_w_up[l], 'rwkv_a_up': rwkv_a_up[l], 'rwkv_g_up': rwkv_g_up[l],
            'mlstm_conv_w': mlstm_conv_w[l], 'ml_cbias': row(mlstm_conv_b[l]),
            'ml_gbias': row(jnp.pad(jnp.concatenate([mlstm_i_b[l], mlstm_f_b[l]]), (0, GATE_PAD - ML_GATES))),
            'ml_ng': row(mlstm_norm_g[l]),
            'pool_wbd': pool_wbd, 'pool_scale': row(pool_scale[l]),
        }
        mods_p, mods_s = (mod4, DB), (mod4, 0)
        final = l == depth - 1

        xp, st_p = _layer(xp, mods_p, st_p0, 0, functools.partial(_moba_prompt_call, layer=l), lw, final, True,
                          prev_p)
        prev_p = (st_p[7:9], st_p[1], st_p[3])
        st_s_in = (state_rwkv_shift[l], state_rwkv_wkv[l], state_mlstm_conv[l], state_mlstm_c[l],
                   state_mlstm_n[l], state_mlstm_m[l], state_pool[l])
        attend_s = functools.partial(_moba_sample_call, cache_k=cache_k, cache_v=cache_v,
                                     page_table=page_table, layer=l)
        xs, st_s = _layer(xs, mods_s, st_s_in, past_len, attend_s, lw, final, False, prev_s)
        prev_s = (None, st_s[1], st_s[3])
        new_p.append(st_p)
        new_s.append(st_s)

    def collect(lst, n):
        return tuple(lst[-1][i] if i in (1, 3) else jnp.stack([st[i] for st in lst]) for i in range(n))

    kv_out = tuple(jnp.transpose(a.reshape(depth, B, N_HEADS, HEAD_DIM, T), (0, 1, 4, 2, 3)) for a in prev_p[0])
    return (xp, xs) + collect(new_p, 7) + kv_out + collect(new_s, 9)
```
